```python
import jax, jax.numpy as jnp
from jax import lax
import numpy as np

D_MODEL = 2048
BATCH = 8
SEQ = 4096
DEPTH = 4

D_MIX = D_MODEL
D_CONV = D_MIX // 2
CONV_HEADS = 16
D_POOL = D_MIX - D_CONV
POOL_WINDOWS = (2, 4, 8, 16)
N_POOL_GROUPS = len(POOL_WINDOWS)
POOL_GROUP_DIM = D_POOL // N_POOL_GROUPS
CONV_WIDTH = 3
D_IN_PROJ = 3 * D_CONV + D_POOL
D_FF = 4 * D_MODEL
EPS = 1e-6

kernel_name = "hybrid_shortconv_pool_sqrelu_trunk"


def _rmsnorm(x, g):
    xf = x.astype(jnp.float32)
    r = lax.rsqrt(jnp.mean(xf * xf, axis=-1, keepdims=True) + EPS)
    return (xf * r).astype(x.dtype) * g


def _short_conv_mixer(b, c, xt, conv_w):
    u = c * xt
    s = u.shape[1]
    u_pad = jnp.pad(u, ((0, 0), (CONV_WIDTH - 1, 0), (0, 0)))
    conv = conv_w[0] * u_pad[:, 0:s] + conv_w[1] * u_pad[:, 1:s + 1] + conv_w[2] * u_pad[:, 2:s + 2]
    return b * conv


def _pool_mixer(v, pool_w, pool_scale):
    bsz, s, _ = v.shape
    vg = v.reshape(bsz, s, N_POOL_GROUPS, POOL_GROUP_DIM)
    csum = jnp.cumsum(vg.astype(jnp.float32), axis=1)
    pos = jnp.arange(s, dtype=jnp.float32)
    outs = []
    for g, w in enumerate(POOL_WINDOWS):
        cs = csum[:, :, g]
        lagged = jnp.pad(cs[:, :s - w], ((0, 0), (w, 0), (0, 0)))
        count = jnp.minimum(pos + 1.0, float(w))[None, :, None]
        mean = (cs - lagged) / count
        outs.append(mean.astype(v.dtype) - vg[:, :, g])
    d = jnp.stack(outs, axis=2)
    y = jnp.einsum('bsgc,gcd->bsgd', d, pool_w).reshape(bsz, s, D_POOL)
    return y * pool_scale


def _fwd_setup_inputs(seed: int = 0) -> dict:
    key = jax.random.key(seed)
    ks = jax.random.split(key, 12)
    f32 = jnp.float32
    x = jax.random.normal(ks[0], (BATCH, SEQ, D_MODEL), f32)
    w_in = jax.random.normal(ks[1], (DEPTH, D_MODEL, D_IN_PROJ), f32) * D_MODEL ** -0.5
    conv_w = jax.random.normal(ks[2], (DEPTH, CONV_WIDTH, D_CONV), f32) * CONV_WIDTH ** -0.5
    pool_w = jax.random.normal(ks[3], (DEPTH, N_POOL_GROUPS, POOL_GROUP_DIM, POOL_GROUP_DIM), f32) * POOL_GROUP_DIM ** -0.5
    pool_scale = 1.0 + 0.1 * jax.random.normal(ks[4], (DEPTH, D_POOL), f32)
    w_out = jax.random.normal(ks[5], (DEPTH, D_MIX, D_MODEL), f32) * D_MIX ** -0.5
    norm_mix = 1.0 + 0.05 * jax.random.normal(ks[6], (DEPTH, D_MODEL), f32)
    norm_mlp = 1.0 + 0.05 * jax.random.normal(ks[7], (DEPTH, D_MODEL), f32)
    w_up = jax.random.normal(ks[8], (DEPTH, D_MODEL, D_FF), f32) * D_MODEL ** -0.5
    w_down = jax.random.normal(ks[9], (DEPTH, D_FF, D_MODEL), f32) * D_FF ** -0.5
    norm_final = 1.0 + 0.05 * jax.random.normal(ks[10], (D_MODEL,), f32)
    return {"x": x, "w_in": w_in, "conv_w": conv_w, "pool_w": pool_w, "pool_scale": pool_scale,
            "w_out": w_out, "norm_mix": norm_mix, "norm_mlp": norm_mlp, "w_up": w_up,
            "w_down": w_down, "norm_final": norm_final}


def _fwd_reference(x, w_in, conv_w, pool_w, pool_scale, w_out, norm_mix, norm_mlp, w_up, w_down, norm_final):
    for l in range(DEPTH):
        h = _rmsnorm(x, norm_mix[l])
        proj = jnp.einsum('bsd,de->bse', h, w_in[l])
        b = proj[..., 0:D_CONV]
        c = proj[..., D_CONV:2 * D_CONV]
        xt = proj[..., 2 * D_CONV:3 * D_CONV]
        v = proj[..., 3 * D_CONV:]
        y_conv = _short_conv_mixer(b, c, xt, conv_w[l])
        y_pool = _pool_mixer(v, pool_w[l], pool_scale[l])
        y = jnp.concatenate([y_conv, y_pool], axis=-1)
        x = x + jnp.einsum('bse,ed->bsd', y, w_out[l])
        h = _rmsnorm(x, norm_mlp[l])
        u = jax.nn.relu(jnp.einsum('bsd,df->bsf', h, w_up[l]))
        x = x + jnp.einsum('bsf,fd->bsd', u * u, w_down[l])
    return _rmsnorm(x, norm_final)


import jax as _jax
import jax.numpy as _jnp

TWIN_FORMAT = 'train_step'
FWD_PARAMS = ['x', 'w_in', 'conv_w', 'pool_w', 'pool_scale', 'w_out', 'norm_mix', 'norm_mlp', 'w_up', 'w_down', 'norm_final']
TWIN_WEIGHTS = ['w_in', 'conv_w', 'pool_w', 'pool_scale', 'w_out', 'norm_mix', 'norm_mlp', 'w_up', 'w_down', 'norm_final']
TWIN_DIFF_INPUT = 'x'
TWIN_INPUTS = ['x', 'w_in', 'conv_w', 'pool_w', 'pool_scale', 'w_out', 'norm_mix', 'norm_mlp', 'w_up', 'w_down', 'norm_final', 'loss_target', 'm_w_in', 'm_conv_w', 'm_pool_w', 'm_pool_scale', 'm_w_out', 'm_norm_mix', 'm_norm_mlp', 'm_w_up', 'm_w_down', 'm_norm_final', 'v_w_in', 'v_conv_w', 'v_pool_w', 'v_pool_scale', 'v_w_out', 'v_norm_mix', 'v_norm_mlp', 'v_w_up', 'v_w_down', 'v_norm_final']
TWIN_OUTPUTS = ['loss', 'grad_x', 'grad_w_in', 'grad_conv_w', 'grad_pool_w', 'grad_pool_scale', 'grad_w_out', 'grad_norm_mix', 'grad_norm_mlp', 'grad_w_up', 'grad_w_down', 'grad_norm_final', 'delta_w_in', 'delta_conv_w', 'delta_pool_w', 'delta_pool_scale', 'delta_w_out', 'delta_norm_mix', 'delta_norm_mlp', 'delta_w_up', 'delta_w_down', 'delta_norm_final', 'new_m_w_in', 'new_m_conv_w', 'new_m_pool_w', 'new_m_pool_scale', 'new_m_w_out', 'new_m_norm_mix', 'new_m_norm_mlp', 'new_m_w_up', 'new_m_w_down', 'new_m_norm_final', 'new_v_w_in', 'new_v_conv_w', 'new_v_pool_w', 'new_v_pool_scale', 'new_v_w_out', 'new_v_norm_mix', 'new_v_norm_mlp', 'new_v_w_up', 'new_v_w_down', 'new_v_norm_final']
TWIN_LEAF_KINDS = {'loss': 'loss', 'grad_x': 'grad_x', 'grad_w_in': 'grad_w', 'grad_conv_w': 'grad_w', 'grad_pool_w': 'grad_w', 'grad_pool_scale': 'grad_w', 'grad_w_out': 'grad_w', 'grad_norm_mix': 'grad_w', 'grad_norm_mlp': 'grad_w', 'grad_w_up': 'grad_w', 'grad_w_down': 'grad_w', 'grad_norm_final': 'grad_w', 'delta_w_in': 'delta_w', 'delta_conv_w': 'delta_w', 'delta_pool_w': 'delta_w', 'delta_pool_scale': 'delta_w', 'delta_w_out': 'delta_w', 'delta_norm_mix': 'delta_w', 'delta_norm_mlp': 'delta_w', 'delta_w_up': 'delta_w', 'delta_w_down': 'delta_w', 'delta_norm_final': 'delta_w', 'new_m_w_in': 'new_m', 'new_m_conv_w': 'new_m', 'new_m_pool_w': 'new_m', 'new_m_pool_scale': 'new_m', 'new_m_w_out': 'new_m', 'new_m_norm_mix': 'new_m', 'new_m_norm_mlp': 'new_m', 'new_m_w_up': 'new_m', 'new_m_w_down': 'new_m', 'new_m_norm_final': 'new_m', 'new_v_w_in': 'new_v', 'new_v_conv_w': 'new_v', 'new_v_pool_w': 'new_v', 'new_v_pool_scale': 'new_v', 'new_v_w_out': 'new_v', 'new_v_norm_mix': 'new_v', 'new_v_norm_mlp': 'new_v', 'new_v_w_up': 'new_v', 'new_v_w_down': 'new_v', 'new_v_norm_final': 'new_v'}


def _forward(args):
    return _fwd_reference(*[args[k] for k in FWD_PARAMS])


def _output_shape():
    out = _jax.eval_shape(lambda: _forward(_fwd_setup_inputs(0)))
    return out.shape, out.dtype

N_MICROBATCH = 1
ADAM_LR = 0.001
ADAM_B1 = 0.9
ADAM_B2 = 0.999
ADAM_EPS = 1e-08
ADAM_WD = 0.01
ADAM_STEP = 10
PER_EXAMPLE_BATCH_AXIS = {'x': 0, 'loss_target': 0}
SHARED_INPUTS = []
_WEIGHT_DTYPES = {'w_in': _jnp.float32, 'conv_w': _jnp.float32, 'pool_w': _jnp.float32, 'pool_scale': _jnp.float32, 'w_out': _jnp.float32, 'norm_mix': _jnp.float32, 'norm_mlp': _jnp.float32, 'w_up': _jnp.float32, 'w_down': _jnp.float32, 'norm_final': _jnp.float32}
MOMENT_SCALE = {'w_in': 5.931662e-02, 'conv_w': 6.102598e-02, 'pool_w': 5.322460e-02, 'pool_scale': 5.562003e-02, 'w_out': 5.736983e-02, 'norm_mix': 8.858274e-02, 'norm_mlp': 6.734380e-02, 'w_up': 3.398450e-02, 'w_down': 7.993900e-02, 'norm_final': 1.622971e+01}


def _to_microbatches(a, axis):
    t = _jnp.moveaxis(a, axis, 0)
    t = t.reshape((N_MICROBATCH, t.shape[0] // N_MICROBATCH) + t.shape[1:])
    return _jnp.moveaxis(t, 1, axis + 1)


def setup_inputs(seed: int = 0) -> dict:
    inp = _fwd_setup_inputs(seed)
    key = _jax.random.fold_in(_jax.random.key(seed), 7919)
    shape, _ = _output_shape()
    out = dict(inp)
    out["loss_target"] = _jax.random.normal(_jax.random.fold_in(key, 0), shape, _jnp.float32)
    for i, name in enumerate(TWIN_WEIGHTS):
        w = inp[name].astype(_jnp.float32)
        if MOMENT_SCALE is None:
            s = _jnp.sqrt(_jnp.mean(_jnp.square(w)) + 1e-30)
        else:
            s = MOMENT_SCALE[name]
        km, kv = _jax.random.split(_jax.random.fold_in(key, i + 1))
        out[name] = w
        out["m_" + name] = s * _jax.random.normal(km, w.shape, _jnp.float32)
        out["v_" + name] = (s * s) * _jax.random.uniform(kv, w.shape, _jnp.float32, 0.5, 1.5)
    if N_MICROBATCH > 1:
        for name, axis in PER_EXAMPLE_BATCH_AXIS.items():
            out[name] = _to_microbatches(out[name], axis)
    return {'x': out['x'], 'w_in': out['w_in'], 'conv_w': out['conv_w'], 'pool_w': out['pool_w'], 'pool_scale': out['pool_scale'], 'w_out': out['w_out'], 'norm_mix': out['norm_mix'], 'norm_mlp': out['norm_mlp'], 'w_up': out['w_up'], 'w_down': out['w_down'], 'norm_final': out['norm_final'], 'loss_target': out['loss_target'], 'm_w_in': out['m_w_in'], 'm_conv_w': out['m_conv_w'], 'm_pool_w': out['m_pool_w'], 'm_pool_scale': out['m_pool_scale'], 'm_w_out': out['m_w_out'], 'm_norm_mix': out['m_norm_mix'], 'm_norm_mlp': out['m_norm_mlp'], 'm_w_up': out['m_w_up'], 'm_w_down': out['m_w_down'], 'm_norm_final': out['m_norm_final'], 'v_w_in': out['v_w_in'], 'v_conv_w': out['v_conv_w'], 'v_pool_w': out['v_pool_w'], 'v_pool_scale': out['v_pool_scale'], 'v_w_out': out['v_w_out'], 'v_norm_mix': out['v_norm_mix'], 'v_norm_mlp': out['v_norm_mlp'], 'v_w_up': out['v_w_up'], 'v_w_down': out['v_w_down'], 'v_norm_final': out['v_norm_final']}


def _loss(weights, diff, rest, loss_target):
    with _jax.named_scope("forward"):
        args = {**rest, TWIN_DIFF_INPUT: diff, **{k: w.astype(_WEIGHT_DTYPES[k]) for k, w in weights.items()}}
        y = _forward(args)
    with _jax.named_scope("loss_head"):
        err = _jnp.square(y.astype(_jnp.float32) - loss_target)
        return 0.5 * _jnp.sum(_jnp.mean(err, axis=-1)) if err.ndim else 0.5 * err


def _adamw(w, g, m, v):
    m = ADAM_B1 * m + (1.0 - ADAM_B1) * g
    v = ADAM_B2 * v + (1.0 - ADAM_B2) * _jnp.square(g)
    m_hat = m / (1.0 - ADAM_B1 ** ADAM_STEP)
    v_hat = v / (1.0 - ADAM_B2 ** ADAM_STEP)
    delta = -ADAM_LR * (m_hat / (_jnp.sqrt(v_hat) + ADAM_EPS) + ADAM_WD * w)
    return delta, m, v


def reference(x, w_in, conv_w, pool_w, pool_scale, w_out, norm_mix, norm_mlp, w_up, w_down, norm_final, loss_target, m_w_in, m_conv_w, m_pool_w, m_pool_scale, m_w_out, m_norm_mix, m_norm_mlp, m_w_up, m_w_down, m_norm_final, v_w_in, v_conv_w, v_pool_w, v_pool_scale, v_w_out, v_norm_mix, v_norm_mlp, v_w_up, v_w_down, v_norm_final):
    given = dict(x=x, w_in=w_in, conv_w=conv_w, pool_w=pool_w, pool_scale=pool_scale, w_out=w_out, norm_mix=norm_mix, norm_mlp=norm_mlp, w_up=w_up, w_down=w_down, norm_final=norm_final, loss_target=loss_target, m_w_in=m_w_in, m_conv_w=m_conv_w, m_pool_w=m_pool_w, m_pool_scale=m_pool_scale, m_w_out=m_w_out, m_norm_mix=m_norm_mix, m_norm_mlp=m_norm_mlp, m_w_up=m_w_up, m_w_down=m_w_down, m_norm_final=m_norm_final, v_w_in=v_w_in, v_conv_w=v_conv_w, v_pool_w=v_pool_w, v_pool_scale=v_pool_scale, v_w_out=v_w_out, v_norm_mix=v_norm_mix, v_norm_mlp=v_norm_mlp, v_w_up=v_w_up, v_w_down=v_w_down, v_norm_final=v_norm_final)
    weights = {n: given[n] for n in TWIN_WEIGHTS}
    shared = {n: given[n] for n in SHARED_INPUTS}
    per_example = {n: given[n] for n in ['x']}
    grad_fn = _jax.value_and_grad(_loss, argnums=(0, 1))

    def one_microbatch(ex, loss_target):
        ex = dict(ex)
        diff = ex.pop(TWIN_DIFF_INPUT)
        return grad_fn(weights, diff, {**shared, **ex}, loss_target)

    if N_MICROBATCH == 1:
        loss, (grad_w, grad_x) = one_microbatch(per_example, given["loss_target"])
    else:
        def body(carry, xs):
            loss_sum, grad_sum = carry
            l_k, (gw_k, gx_k) = one_microbatch(xs[0], xs[1])
            with _jax.named_scope("update"):
                return (loss_sum + l_k, _jax.tree.map(_jnp.add, grad_sum, gw_k)), gx_k

        init = (_jnp.zeros((), _jnp.float32), _jax.tree.map(_jnp.zeros_like, weights))
        (loss, grad_w), grad_x = _jax.lax.scan(body, init, (per_example, given["loss_target"]))
    with _jax.named_scope("update"):
        delta_w, new_m, new_v = {}, {}, {}
        for n in TWIN_WEIGHTS:
            delta_w[n], new_m[n], new_v[n] = _adamw(weights[n], grad_w[n], given["m_" + n], given["v_" + n])
    return (loss, grad_x, *[grad_w[n] for n in TWIN_WEIGHTS], *[delta_w[n] for n in TWIN_WEIGHTS],
            *[new_m[n] for n in TWIN_WEIGHTS], *[new_v[n] for n in TWIN_WEIGHTS])
```

```python
import functools

import jax
import jax.numpy as jnp
from jax import lax
from jax.experimental import pallas as pl
from jax.experimental.pallas import tpu as pltpu

F32 = jnp.float32
BF16 = jnp.bfloat16

EPS = 1e-6
POOL_WINDOWS = (2, 4, 8, 16)
CONV_TAPS = 3
HALO = 16

ADAM_LR = 0.001
ADAM_B1 = 0.9
ADAM_B2 = 0.999
ADAM_EPS = 1e-08
ADAM_WD = 0.01
ADAM_STEP = 10

N_CHIPS = 4
MESH_AXES = ("x", "y", "c")
V7X_VMEM_LIMIT_BYTES = 56 * 1024 * 1024
SUBLANES_BF16 = 16
LANES = 128

HBM_SPEC = pl.BlockSpec(memory_space=pltpu.HBM)
VMEM_SPEC = pl.BlockSpec(memory_space=pltpu.VMEM)
MESH_ID = pl.DeviceIdType.MESH


def _tile(dim, target, align):
    if dim <= target:
        return dim
    t = (target // align) * align
    while dim % t:
        t -= align
    assert t > 0, (dim, target, align)
    return t


def _compute_params(*semantics):
    return pltpu.CompilerParams(dimension_semantics=semantics, vmem_limit_bytes=V7X_VMEM_LIMIT_BYTES)


def _shape(shape, dtype):
    return jax.ShapeDtypeStruct(shape, dtype)


def _rmsnorm_bwd(dh, x, gain, dres):
    r = lax.rsqrt(jnp.mean(x * x, axis=-1, keepdims=True) + EPS)
    xn = x * r
    dgain = jnp.sum(dh * xn, axis=0, keepdims=True)
    dxn = dh * gain
    dx = r * (dxn - xn * jnp.mean(dxn * xn, axis=-1, keepdims=True))
    if dres is not None:
        dx = dx + dres
    return dx, dgain


def _norm_matmul(x, gain, w, *, relu2, name):
    s, d = x.shape
    nsh, _, c = w.shape
    tm = _tile(s, 512, SUBLANES_BF16)
    tn = _tile(c, 1024, LANES)
    per_shard = c // tn

    def body(x_ref, g_ref, w_ref, h_ref, *out_refs):
        @pl.when(pl.program_id(1) == 0)
        def _():
            xf = x_ref[...]
            r = lax.rsqrt(jnp.mean(xf * xf, axis=-1, keepdims=True) + EPS)
            h_ref[...] = ((xf * r) * g_ref[...]).astype(BF16)

        acc = jnp.dot(h_ref[...], w_ref[...], preferred_element_type=F32)
        if relu2:
            u = jnp.maximum(acc, 0.0)
            out_refs[0][...] = u.astype(BF16)
            out_refs[1][...] = (u * u).astype(BF16)
        else:
            out_refs[0][...] = acc

    tile_out = pl.BlockSpec((tm, tn), lambda i, j: (i, j))
    n = nsh * c
    outs = [_shape((s, d), BF16)] + ([_shape((s, n), BF16)] * 2 if relu2 else [_shape((s, n), F32)])
    return pl.pallas_call(
        body,
        name=name,
        grid=(s // tm, n // tn),
        in_specs=[
            pl.BlockSpec((tm, d), lambda i, j: (i, 0)),
            pl.BlockSpec((1, d), lambda i, j: (0, 0)),
            pl.BlockSpec((None, d, tn), lambda i, j: (j // per_shard, 0, j % per_shard)),
        ],
        out_specs=[pl.BlockSpec((tm, d), lambda i, j: (i, 0))] + [tile_out] * (len(outs) - 1),
        out_shape=outs,
        compiler_params=_compute_params("parallel", "arbitrary"),
    )(x, gain, w)


def _matmul_residual(a, w, res, *, name):
    s, k = a.shape
    _, n = w.shape
    tm = _tile(s, 512, SUBLANES_BF16)
    tn = _tile(n, 1024, LANES)
    tk = _tile(k, 2048, LANES)
    nk = k // tk

    def body(a_ref, w_ref, r_ref, o_ref, acc_ref):
        kk = pl.program_id(2)

        @pl.when(kk == 0)
        def _():
            acc_ref[...] = r_ref[...]

        acc_ref[...] += jnp.dot(a_ref[...], w_ref[...], preferred_element_type=F32)

        @pl.when(kk == nk - 1)
        def _():
            o_ref[...] = acc_ref[...]

    return pl.pallas_call(
        body,
        name=name,
        grid=(s // tm, n // tn, nk),
        in_specs=[
            pl.BlockSpec((tm, tk), lambda i, j, kk: (i, kk)),
            pl.BlockSpec((tk, tn), lambda i, j, kk: (kk, j)),
            pl.BlockSpec((tm, tn), lambda i, j, kk: (i, j)),
        ],
        out_specs=pl.BlockSpec((tm, tn), lambda i, j, kk: (i, j)),
        out_shape=_shape((s, n), F32),
        scratch_shapes=[pltpu.VMEM((tm, tn), F32)],
        compiler_params=_compute_params("parallel", "parallel", "arbitrary"),
    )(a, w, res)


def _matmul_tn(a, g, *, n_shards, shard_cols, name):
    m, kd = a.shape
    _, n = g.shape
    r, c = (kd, n // n_shards) if shard_cols else (kd // n_shards, n)
    tr = _tile(r, 1024, LANES)
    tn = _tile(c, 1024, LANES)
    tm = _tile(m, 1024, SUBLANES_BF16)
    nm = m // tm
    rows_per_shard = r // tr
    cols_per_shard = c // tn

    def body(a_ref, g_ref, o_ref, acc_ref):
        mm = pl.program_id(2)

        @pl.when(mm == 0)
        def _():
            acc_ref[...] = jnp.zeros_like(acc_ref)

        acc_ref[...] += lax.dot_general(
            a_ref[...], g_ref[...], (((0,), (0,)), ((), ())), preferred_element_type=F32
        )

        @pl.when(mm == nm - 1)
        def _():
            o_ref[...] = acc_ref[...].astype(BF16)

    if shard_cols:
        out_map = lambda i, j, mm: (j // cols_per_shard, i, j % cols_per_shard)
    else:
        out_map = lambda i, j, mm: (i // rows_per_shard, i % rows_per_shard, j)
    return pl.pallas_call(
        body,
        name=name,
        grid=(kd // tr, n // tn, nm),
        in_specs=[
            pl.BlockSpec((tm, tr), lambda i, j, mm: (mm, i)),
            pl.BlockSpec((tm, tn), lambda i, j, mm: (mm, j)),
        ],
        out_specs=pl.BlockSpec((None, tr, tn), out_map),
        out_shape=_shape((n_shards, r, c), BF16),
        scratch_shapes=[pltpu.VMEM((tr, tn), F32)],
        compiler_params=_compute_params("parallel", "parallel", "arbitrary"),
    )(a, g)


def _matmul_nt(g, w, *, u=None, out_dtype=F32, name):
    m, n = g.shape
    kd, _ = w.shape
    tm = _tile(m, 512, SUBLANES_BF16)
    tj = _tile(kd, 1024, LANES)
    tk = _tile(n, 2048, LANES)
    nk = n // tk

    def body(*refs):
        if u is None:
            g_ref, w_ref, o_ref, acc_ref = refs
        else:
            g_ref, w_ref, u_ref, o_ref, acc_ref = refs
        kk = pl.program_id(2)

        @pl.when(kk == 0)
        def _():
            acc_ref[...] = jnp.zeros_like(acc_ref)

        acc_ref[...] += lax.dot_general(
            g_ref[...], w_ref[...], (((1,), (1,)), ((), ())), preferred_element_type=F32
        )

        @pl.when(kk == nk - 1)
        def _():
            if u is None:
                o_ref[...] = acc_ref[...].astype(o_ref.dtype)
            else:
                o_ref[...] = (acc_ref[...] * (2.0 * u_ref[...].astype(F32))).astype(o_ref.dtype)

    tile_out = pl.BlockSpec((tm, tj), lambda i, j, kk: (i, j))
    in_specs = [
        pl.BlockSpec((tm, tk), lambda i, j, kk: (i, kk)),
        pl.BlockSpec((tj, tk), lambda i, j, kk: (j, kk)),
    ]
    args = [g, w]
    if u is not None:
        in_specs.append(tile_out)
        args.append(u)
    return pl.pallas_call(
        body,
        name=name,
        grid=(m // tm, kd // tj, nk),
        in_specs=in_specs,
        out_specs=tile_out,
        out_shape=_shape((m, kd), BF16 if u is not None else out_dtype),
        scratch_shapes=[pltpu.VMEM((tm, tj), F32)],
        compiler_params=_compute_params("parallel", "parallel", "arbitrary"),
    )(*args)


NORM_BWD_ROWS = 64


def _matmul_nt_norm_bwd(g, w, x, gain, dres, *, name):
    s, n = g.shape
    nsh, d, c = w.shape
    tm = _tile(s, 512, SUBLANES_BF16)
    tk = _tile(c, 1024, LANES)
    per_shard = c // tk
    nk = n // tk
    rows = min(NORM_BWD_ROWS, tm)

    def body(g_ref, w_ref, x_ref, gain_ref, dres_ref, dx_ref, dxb_ref, dg_ref, acc_ref):
        i = pl.program_id(0)
        kk = pl.program_id(1)

        @pl.when(kk == 0)
        def _():
            acc_ref[...] = jnp.zeros_like(acc_ref)

        @pl.when((kk == 0) & (i == 0))
        def _():
            dg_ref[...] = jnp.zeros_like(dg_ref)

        acc_ref[...] += lax.dot_general(
            g_ref[...], w_ref[...], (((1,), (1,)), ((), ())), preferred_element_type=F32
        )

        @pl.when(kk == nk - 1)
        def _():
            def step(t, carry):
                rs = pl.ds(pl.multiple_of(t * rows, rows), rows)
                dx, dgain = _rmsnorm_bwd(acc_ref[rs, :], x_ref[rs, :], gain_ref[...], dres_ref[rs, :])
                dx_ref[rs, :] = dx
                dxb_ref[rs, :] = dx.astype(BF16)
                dg_ref[...] += dgain
                return carry

            lax.fori_loop(0, tm // rows, step, 0)

    row_tile = pl.BlockSpec((tm, d), lambda i, kk: (i, 0))
    vec = pl.BlockSpec((1, d), lambda i, kk: (0, 0))
    return pl.pallas_call(
        body,
        name=name,
        grid=(s // tm, nk),
        in_specs=[
            pl.BlockSpec((tm, tk), lambda i, kk: (i, kk)),
            pl.BlockSpec((None, d, tk), lambda i, kk: (kk // per_shard, 0, kk % per_shard)),
            row_tile,
            vec,
            row_tile,
        ],
        out_specs=[row_tile, row_tile, vec],
        out_shape=[_shape((s, d), F32), _shape((s, d), BF16), _shape((1, d), F32)],
        scratch_shapes=[pltpu.VMEM((tm, d), F32)],
        compiler_params=_compute_params("arbitrary", "arbitrary"),
    )(g, w, x, gain, dres)


def _loss_and_grad(x, gain, target, *, name):
    s, d = x.shape
    tm = _tile(s, 256, SUBLANES_BF16)

    def body(x_ref, gain_ref, t_ref, loss_ref, dx_ref, dxb_ref, dg_ref):
        @pl.when(pl.program_id(0) == 0)
        def _():
            loss_ref[...] = jnp.zeros_like(loss_ref)
            dg_ref[...] = jnp.zeros_like(dg_ref)

        xf = x_ref[...]
        r = lax.rsqrt(jnp.mean(xf * xf, axis=-1, keepdims=True) + EPS)
        err = (xf * r) * gain_ref[...] - t_ref[...]
        loss_ref[...] += 0.5 * jnp.sum(jnp.mean(err * err, axis=-1, keepdims=True))
        dx, dgain = _rmsnorm_bwd(err * (1.0 / d), xf, gain_ref[...], None)
        dx_ref[...] = dx
        dxb_ref[...] = dx.astype(BF16)
        dg_ref[...] += dgain

    row_tile = pl.BlockSpec((tm, d), lambda i: (i, 0))
    vec = pl.BlockSpec((1, d), lambda i: (0, 0))
    return pl.pallas_call(
        body,
        name=name,
        grid=(s // tm,),
        in_specs=[row_tile, vec, row_tile],
        out_specs=[pl.BlockSpec((1, LANES), lambda i: (0, 0)), row_tile, row_tile, vec],
        out_shape=[_shape((1, LANES), F32), _shape((s, d), F32), _shape((s, d), BF16), _shape((1, d), F32)],
        compiler_params=_compute_params("arbitrary"),
    )(x, gain, target)


def _trailing_sums(v_ext, window):
    acc, span = v_ext, 1
    while span < window:
        acc = acc + pltpu.roll(acc, span, 0)
        span *= 2
    return acc


def _leading_sums(q_ext, window):
    n = q_ext.shape[0]
    acc, span = q_ext, 1
    while span < window:
        acc = acc + pltpu.roll(acc, n - span, 0)
        span *= 2
    return acc


def _inverse_counts(first_token, rows, window):
    t = first_token + lax.broadcasted_iota(jnp.int32, (rows, 1), 0)
    return 1.0 / jnp.minimum(t + 1, window).astype(F32)


def _mixer_fwd(proj, conv_w, pool_w, pool_scale, *, name):
    s, _ = proj.shape
    dc = conv_w.shape[1]
    n_groups, cg, _ = pool_w.shape
    dp = n_groups * cg
    assert dc == dp and all(w & (w - 1) == 0 and w <= HALO for w in POOL_WINDOWS)
    ts = _tile(s, 256, HALO)
    halo_blocks = ts // HALO

    def body(b_ref, c_ref, xt_ref, v_ref, ch_ref, xth_ref, vh_ref, cw_ref, pw_ref, ps_ref, y_ref):
        i = pl.program_id(0)
        has_past = i > 0
        u_ext = jnp.concatenate(
            [jnp.where(has_past, ch_ref[...] * xth_ref[...], 0.0), c_ref[...] * xt_ref[...]], axis=0
        )
        conv = (
            cw_ref[2:3, :] * u_ext[HALO:]
            + cw_ref[1:2, :] * pltpu.roll(u_ext, 1, 0)[HALO:]
            + cw_ref[0:1, :] * pltpu.roll(u_ext, 2, 0)[HALO:]
        )
        y_ref[:, 0:dc] = (b_ref[...] * conv).astype(BF16)
        for gi, window in enumerate(POOL_WINDOWS):
            cols = slice(gi * cg, (gi + 1) * cg)
            v_ext = jnp.concatenate([jnp.where(has_past, vh_ref[:, cols], 0.0), v_ref[:, cols]], axis=0)
            mean = _trailing_sums(v_ext, window)[HALO:] * _inverse_counts(i * ts, ts, window)
            diff = (mean - v_ext[HALO:]).astype(BF16)
            z = jnp.dot(diff, pw_ref[gi], preferred_element_type=F32)
            y_ref[:, dc + gi * cg : dc + (gi + 1) * cg] = (z * ps_ref[:, cols]).astype(BF16)

    def col(jc):
        return pl.BlockSpec((ts, dc), lambda i: (i, jc))

    def past(jc):
        return pl.BlockSpec((HALO, dc), lambda i: (jnp.maximum(i * halo_blocks - 1, 0), jc))

    return pl.pallas_call(
        body,
        name=name,
        grid=(s // ts,),
        in_specs=[
            col(0), col(1), col(2), col(3), past(1), past(2), past(3),
            pl.BlockSpec((8, dc), lambda i: (0, 0)),
            pl.BlockSpec((n_groups, cg, cg), lambda i: (0, 0, 0)),
            pl.BlockSpec((1, dp), lambda i: (0, 0)),
        ],
        out_specs=pl.BlockSpec((ts, dc + dp), lambda i: (i, 0)),
        out_shape=_shape((s, dc + dp), BF16),
        compiler_params=_compute_params("parallel"),
    )(proj, proj, proj, proj, proj, proj, proj, conv_w, pool_w, pool_scale)


def _mixer_bwd(dy, proj, conv_w, pool_w, pool_scale, *, name):
    s, e = proj.shape
    dc = conv_w.shape[1]
    n_groups, cg, _ = pool_w.shape
    dp = n_groups * cg
    ts = _tile(s, 256, HALO)
    halo_blocks = ts // HALO
    n_tiles = s // ts
    n_halo_blocks = s // HALO
    n_ext = ts + HALO

    def body(dyc_ref, dyp_ref, dycn_ref, dypn_ref, b_ref, c_ref, xt_ref, v_ref, bn_ref, ch_ref, xth_ref, vh_ref,
             cw_ref, pw_ref, ps_ref, dproj_ref, dcw_ref, dpw_ref, dps_ref):
        i = pl.program_id(0)
        has_past = i > 0
        has_next = i < n_tiles - 1

        @pl.when(i == 0)
        def _():
            dcw_ref[...] = jnp.zeros_like(dcw_ref)
            dpw_ref[...] = jnp.zeros_like(dpw_ref)
            dps_ref[...] = jnp.zeros_like(dps_ref)

        c_now, xt_now, b_now = c_ref[...], xt_ref[...], b_ref[...]
        u_ext = jnp.concatenate([jnp.where(has_past, ch_ref[...] * xth_ref[...], 0.0), c_now * xt_now], axis=0)
        u0 = u_ext[HALO:]
        u1 = pltpu.roll(u_ext, 1, 0)[HALO:]
        u2 = pltpu.roll(u_ext, 2, 0)[HALO:]
        dyc = dyc_ref[...]
        conv = cw_ref[2:3, :] * u0 + cw_ref[1:2, :] * u1 + cw_ref[0:1, :] * u2
        dproj_ref[:, 0:dc] = (dyc * conv).astype(BF16)
        dconv = dyc * b_now
        dconv_ext = jnp.concatenate([dconv, jnp.where(has_next, dycn_ref[...] * bn_ref[...], 0.0)], axis=0)
        du = (
            cw_ref[2:3, :] * dconv
            + cw_ref[1:2, :] * pltpu.roll(dconv_ext, n_ext - 1, 0)[:ts]
            + cw_ref[0:1, :] * pltpu.roll(dconv_ext, n_ext - 2, 0)[:ts]
        )
        dproj_ref[:, dc : 2 * dc] = (du * xt_now).astype(BF16)
        dproj_ref[:, 2 * dc : 3 * dc] = (du * c_now).astype(BF16)
        dcw_ref[0:1, :] += jnp.sum(dconv * u2, axis=0, keepdims=True)
        dcw_ref[1:2, :] += jnp.sum(dconv * u1, axis=0, keepdims=True)
        dcw_ref[2:3, :] += jnp.sum(dconv * u0, axis=0, keepdims=True)

        for gi, window in enumerate(POOL_WINDOWS):
            cols = slice(gi * cg, (gi + 1) * cg)
            v_ext = jnp.concatenate([jnp.where(has_past, vh_ref[:, cols], 0.0), v_ref[:, cols]], axis=0)
            mean = _trailing_sums(v_ext, window)[HALO:] * _inverse_counts(i * ts, ts, window)
            diff = (mean - v_ext[HALO:]).astype(BF16)
            z = jnp.dot(diff, pw_ref[gi], preferred_element_type=F32)
            dyp = dyp_ref[:, cols]
            dps_ref[:, cols] += jnp.sum(dyp * z, axis=0, keepdims=True)
            scale = ps_ref[:, cols]
            dz_ext = jnp.concatenate([dyp * scale, jnp.where(has_next, dypn_ref[:, cols] * scale, 0.0)], axis=0)
            dz_ext = dz_ext.astype(BF16)
            dpw_ref[gi] += lax.dot_general(
                diff, dz_ext[:ts], (((0,), (0,)), ((), ())), preferred_element_type=F32
            )
            ddiff_ext = lax.dot_general(
                dz_ext, pw_ref[gi], (((1,), (1,)), ((), ())), preferred_element_type=F32
            )
            q_ext = ddiff_ext * _inverse_counts(i * ts, n_ext, window)
            dv = _leading_sums(q_ext, window)[:ts] - ddiff_ext[:ts]
            dproj_ref[:, 3 * dc + gi * cg : 3 * dc + (gi + 1) * cg] = dv.astype(BF16)

    def col(jc):
        return pl.BlockSpec((ts, dc), lambda i: (i, jc))

    def past(jc):
        return pl.BlockSpec((HALO, dc), lambda i: (jnp.maximum(i * halo_blocks - 1, 0), jc))

    def following(jc):
        return pl.BlockSpec((HALO, dc), lambda i: (jnp.minimum((i + 1) * halo_blocks, n_halo_blocks - 1), jc))

    return pl.pallas_call(
        body,
        name=name,
        grid=(n_tiles,),
        in_specs=[
            col(0), col(1), following(0), following(1),
            col(0), col(1), col(2), col(3), following(0), past(1), past(2), past(3),
            pl.BlockSpec((8, dc), lambda i: (0, 0)),
            pl.BlockSpec((n_groups, cg, cg), lambda i: (0, 0, 0)),
            pl.BlockSpec((1, dp), lambda i: (0, 0)),
        ],
        out_specs=[
            pl.BlockSpec((ts, e), lambda i: (i, 0)),
            pl.BlockSpec((8, dc), lambda i: (0, 0)),
            pl.BlockSpec((n_groups, cg, cg), lambda i: (0, 0, 0)),
            pl.BlockSpec((1, dp), lambda i: (0, 0)),
        ],
        out_shape=[_shape((s, e), BF16), _shape((8, dc), F32), _shape((n_groups, cg, cg), F32), _shape((1, dp), F32)],
        compiler_params=_compute_params("arbitrary"),
    )(dy, dy, dy, dy, proj, proj, proj, proj, proj, proj, proj, proj, conv_w, pool_w, pool_scale)


def _cast_bf16(a, *, name):
    rows, cols = a.shape
    tr = _tile(rows, 512, SUBLANES_BF16)

    def body(a_ref, o_ref):
        o_ref[...] = a_ref[...].astype(BF16)

    spec = pl.BlockSpec((tr, cols), lambda i: (i, 0))
    return pl.pallas_call(
        body, name=name, grid=(rows // tr,), in_specs=[spec], out_specs=spec, out_shape=_shape((rows, cols), BF16),
        compiler_params=_compute_params("parallel"),
    )(a)


def _core_index():
    return lax.axis_index("c").astype(jnp.int32).reshape((1,))


def _add_sibling_half(grads, received, *, name):
    nsh, r, c = grads.shape
    hr = r // 2
    tr = _tile(hr, 512, SUBLANES_BF16)
    tiles = hr // tr

    def body(core_ref, g_ref, r_ref, o_ref):
        del core_ref
        o_ref[...] = (g_ref[...].astype(F32) + r_ref[...].astype(F32)).astype(BF16)

    half = pl.BlockSpec((None, tr, c), lambda sh, t, core: (sh, t, 0))
    return pl.pallas_call(
        body,
        name=name,
        grid_spec=pltpu.PrefetchScalarGridSpec(
            num_scalar_prefetch=1,
            grid=(nsh, tiles),
            in_specs=[pl.BlockSpec((None, tr, c), lambda sh, t, core: (sh, core[0] * tiles + t, 0)), half],
            out_specs=half,
        ),
        out_shape=_shape((nsh, hr, c), BF16),
        compiler_params=_compute_params("parallel", "parallel"),
    )(_core_index(), grads, received)


def _sum_chip_partials(partials, *, name):
    nsh, hr, c = partials.shape
    tr = _tile(hr, 256, SUBLANES_BF16)
    tiles = hr // tr

    def body(core_ref, p_ref, o_ref):
        del core_ref
        total = p_ref[0].astype(F32)
        for sh in range(1, nsh):
            total = total + p_ref[sh].astype(F32)
        o_ref[...] = total

    return pl.pallas_call(
        body,
        name=name,
        grid_spec=pltpu.PrefetchScalarGridSpec(
            num_scalar_prefetch=1,
            grid=(tiles,),
            in_specs=[pl.BlockSpec((nsh, tr, c), lambda t, core: (0, t, 0))],
            out_specs=pl.BlockSpec((tr, c), lambda t, core: (core[0] * tiles + t, 0)),
        ),
        out_shape=_shape((2 * hr, c), F32),
        compiler_params=_compute_params("parallel"),
    )(_core_index(), partials)


def _adamw(grad, w, m, v, layer, carried, *, name):
    n_layers, r, c = w.shape
    tr = _tile(r, 256, 8)
    bias1 = 1.0 - ADAM_B1 ** ADAM_STEP
    bias2 = 1.0 - ADAM_B2 ** ADAM_STEP

    def body(g_ref, w_ref, m_ref, v_ref, *rest):
        go_ref, d_ref, mo_ref, vo_ref = rest[-4:]
        g = g_ref[...]
        m_new = ADAM_B1 * m_ref[...] + (1.0 - ADAM_B1) * g
        v_new = ADAM_B2 * v_ref[...] + (1.0 - ADAM_B2) * (g * g)
        go_ref[...] = g
        mo_ref[...] = m_new
        vo_ref[...] = v_new
        d_ref[...] = -ADAM_LR * ((m_new / bias1) / (jnp.sqrt(v_new / bias2) + ADAM_EPS) + ADAM_WD * w_ref[...])

    layer_tile = pl.BlockSpec((None, tr, c), lambda t: (layer, t, 0))
    in_specs = [pl.BlockSpec((tr, c), lambda t: (t, 0)), layer_tile, layer_tile, layer_tile]
    args = [grad, w, m, v]
    aliases = {}
    if carried is not None:
        in_specs += [HBM_SPEC] * 4
        aliases = {4 + n: n for n in range(4)}
        args += list(carried)
    return pl.pallas_call(
        body,
        name=name,
        grid=(r // tr,),
        in_specs=in_specs,
        out_specs=[layer_tile] * 4,
        out_shape=[_shape((n_layers, r, c), F32)] * 4,
        input_output_aliases=aliases,
        compiler_params=_compute_params("parallel"),
    )(*args)


def _place():
    x, y, c = (lax.axis_index(a) for a in MESH_AXES)
    other_chips = [(1 - x, y), (x, 1 - y), (1 - x, 1 - y)]
    return x, y, c, other_chips


def _chip_index(x, y):
    return 2 * x + y


def _gather_small(conv_rows, pool_rows, *, name):
    def body(cv_ref, pw_ref, cv_out, pw_out, send_sems, recv_sems):
        x, y, c, other_chips = _place()
        mine = _chip_index(x, y)
        cv_out[mine] = cv_ref[...]
        pw_out[mine] = pw_ref[...]
        sends = []
        for a, (src, dst) in enumerate(((cv_ref, cv_out), (pw_ref, pw_out))):
            for j, (px, py) in enumerate(other_chips):
                cp = pltpu.make_async_remote_copy(
                    src_ref=src, dst_ref=dst.at[mine], send_sem=send_sems.at[a, j], recv_sem=recv_sems.at[a, j],
                    device_id=(px, py, c), device_id_type=MESH_ID,
                )
                cp.start()
                sends.append(cp)
        for a, (src, dst) in enumerate(((cv_ref, cv_out), (pw_ref, pw_out))):
            for j, (px, py) in enumerate(other_chips):
                pltpu.make_async_remote_copy(
                    src_ref=src, dst_ref=dst.at[_chip_index(px, py)], send_sem=send_sems.at[a, j],
                    recv_sem=recv_sems.at[a, j], device_id=(px, py, c), device_id_type=MESH_ID,
                ).wait_recv()
        for cp in sends:
            cp.wait_send()

    return pl.pallas_call(
        body,
        name=name,
        in_specs=[VMEM_SPEC, VMEM_SPEC],
        out_specs=[VMEM_SPEC, VMEM_SPEC],
        out_shape=[_shape((N_CHIPS,) + conv_rows.shape, F32), _shape((N_CHIPS,) + pool_rows.shape, F32)],
        scratch_shapes=[pltpu.SemaphoreType.DMA((2, 3)), pltpu.SemaphoreType.DMA((2, 3))],
    )(conv_rows, pool_rows)


def _allreduce_small(vec, *, name):
    rows, n = vec.shape
    n_dev = 8

    def body(v_ref, o_ref, slots, send_sems, recv_sems):
        x, y, c, _ = _place()
        me = 4 * x + 2 * y + c
        slots[me] = v_ref[...]
        sends = []
        for mask in range(1, n_dev):
            fx, fy, fc = (mask >> 2) & 1, (mask >> 1) & 1, mask & 1
            peer = (x ^ fx, y ^ fy, c ^ fc)
            cp = pltpu.make_async_remote_copy(
                src_ref=v_ref, dst_ref=slots.at[me], send_sem=send_sems.at[mask - 1], recv_sem=recv_sems.at[mask - 1],
                device_id=peer, device_id_type=MESH_ID,
            )
            cp.start()
            sends.append(cp)
        for mask in range(1, n_dev):
            fx, fy, fc = (mask >> 2) & 1, (mask >> 1) & 1, mask & 1
            peer = (x ^ fx, y ^ fy, c ^ fc)
            pltpu.make_async_remote_copy(
                src_ref=v_ref, dst_ref=slots.at[4 * peer[0] + 2 * peer[1] + peer[2]], send_sem=send_sems.at[mask - 1],
                recv_sem=recv_sems.at[mask - 1], device_id=peer, device_id_type=MESH_ID,
            ).wait_recv()
        total = slots[0]
        for dev in range(1, n_dev):
            total = total + slots[dev]
        o_ref[...] = total
        for cp in sends:
            cp.wait_send()

    return pl.pallas_call(
        body,
        name=name,
        in_specs=[VMEM_SPEC],
        out_specs=VMEM_SPEC,
        out_shape=_shape((rows, n), F32),
        scratch_shapes=[
            pltpu.VMEM((n_dev, rows, n), F32),
            pltpu.SemaphoreType.DMA((n_dev - 1,)),
            pltpu.SemaphoreType.DMA((n_dev - 1,)),
        ],
    )(vec)


def _gather_weights(stacked, layer, *, name):
    n = len(stacked)

    def body(*refs):
        ins, outs = refs[:n], refs[n : 2 * n]
        local_sems, send_sems, recv_sems, pass_send_sems, pass_recv_sems = refs[2 * n :]
        x, y, c, other_chips = _place()
        mine = _chip_index(x, y)
        sibling = (x, y, 1 - c)
        pending = []
        for a in range(n):
            hr = ins[a].shape[1] // 2
            my_half = pl.ds(c * hr, hr)
            cp = pltpu.make_async_copy(ins[a].at[layer], outs[a].at[mine], local_sems.at[a])
            cp.start()
            pending.append(cp.wait)
            for j, (px, py) in enumerate(other_chips):
                cp = pltpu.make_async_remote_copy(
                    src_ref=ins[a].at[layer, my_half], dst_ref=outs[a].at[mine, my_half], send_sem=send_sems.at[a, j],
                    recv_sem=recv_sems.at[a, j], device_id=(px, py, c), device_id_type=MESH_ID,
                )
                cp.start()
                pending.append(cp.wait_send)
        for a in range(n):
            hr = ins[a].shape[1] // 2
            my_half = pl.ds(c * hr, hr)
            for j, (px, py) in enumerate(other_chips):
                landed = outs[a].at[_chip_index(px, py), my_half]
                pltpu.make_async_remote_copy(
                    src_ref=landed, dst_ref=landed, send_sem=send_sems.at[a, j], recv_sem=recv_sems.at[a, j],
                    device_id=(px, py, c), device_id_type=MESH_ID,
                ).wait_recv()
                cp = pltpu.make_async_remote_copy(
                    src_ref=landed, dst_ref=landed, send_sem=pass_send_sems.at[a, j], recv_sem=pass_recv_sems.at[a, j],
                    device_id=sibling, device_id_type=MESH_ID,
                )
                cp.start()
                pending.append(cp.wait_send)
        for a in range(n):
            hr = ins[a].shape[1] // 2
            sibling_half = pl.ds((1 - c) * hr, hr)
            for j, (px, py) in enumerate(other_chips):
                passed = outs[a].at[_chip_index(px, py), sibling_half]
                pltpu.make_async_remote_copy(
                    src_ref=passed, dst_ref=passed, send_sem=pass_send_sems.at[a, j], recv_sem=pass_recv_sems.at[a, j],
                    device_id=sibling, device_id_type=MESH_ID,
                ).wait_recv()
        for wait in pending:
            wait()

    return pl.pallas_call(
        body,
        name=name,
        in_specs=[HBM_SPEC] * n,
        out_specs=[HBM_SPEC] * n,
        out_shape=[_shape((N_CHIPS,) + a.shape[1:], BF16) for a in stacked],
        scratch_shapes=[pltpu.SemaphoreType.DMA((n,))] + [pltpu.SemaphoreType.DMA((n, 3))] * 4,
    )(*stacked)


def _exchange_halves(grads, *, name):
    n = len(grads)

    def body(*refs):
        ins, outs = refs[:n], refs[n : 2 * n]
        send_sems, recv_sems = refs[2 * n :]
        x, y, c, _ = _place()
        copies = []
        for a in range(n):
            hr = ins[a].shape[1] // 2
            cp = pltpu.make_async_remote_copy(
                src_ref=ins[a].at[:, pl.ds((1 - c) * hr, hr), :], dst_ref=outs[a], send_sem=send_sems.at[a],
                recv_sem=recv_sems.at[a], device_id=(x, y, 1 - c), device_id_type=MESH_ID,
            )
            cp.start()
            copies.append(cp)
        for cp in copies:
            cp.wait()

    return pl.pallas_call(
        body,
        name=name,
        in_specs=[HBM_SPEC] * n,
        out_specs=[HBM_SPEC] * n,
        out_shape=[_shape((g.shape[0], g.shape[1] // 2, g.shape[2]), BF16) for g in grads],
        scratch_shapes=[pltpu.SemaphoreType.DMA((n,)), pltpu.SemaphoreType.DMA((n,))],
    )(*grads)


def _scatter_partials(partials, *, name):
    n = len(partials)

    def body(*refs):
        ins, outs = refs[:n], refs[n : 2 * n]
        local_sems, send_sems, recv_sems = refs[2 * n :]
        x, y, c, other_chips = _place()
        mine = _chip_index(x, y)
        pending = []
        for a in range(n):
            cp = pltpu.make_async_copy(ins[a].at[mine], outs[a].at[mine], local_sems.at[a])
            cp.start()
            pending.append(cp.wait)
            for j, (px, py) in enumerate(other_chips):
                cp = pltpu.make_async_remote_copy(
                    src_ref=ins[a].at[_chip_index(px, py)], dst_ref=outs[a].at[mine], send_sem=send_sems.at[a, j],
                    recv_sem=recv_sems.at[a, j], device_id=(px, py, c), device_id_type=MESH_ID,
                )
                cp.start()
                pending.append(cp.wait_send)
        for a in range(n):
            for j, (px, py) in enumerate(other_chips):
                landed = outs[a].at[_chip_index(px, py)]
                pltpu.make_async_remote_copy(
                    src_ref=landed, dst_ref=landed, send_sem=send_sems.at[a, j], recv_sem=recv_sems.at[a, j],
                    device_id=(px, py, c), device_id_type=MESH_ID,
                ).wait_recv()
        for wait in pending:
            wait()

    return pl.pallas_call(
        body,
        name=name,
        in_specs=[HBM_SPEC] * n,
        out_specs=[HBM_SPEC] * n,
        out_shape=[_shape(p.shape, BF16) for p in partials],
        scratch_shapes=[pltpu.SemaphoreType.DMA((n,)), pltpu.SemaphoreType.DMA((n, 3)), pltpu.SemaphoreType.DMA((n, 3))],
    )(*partials)


def _share_halves(totals, *, name):
    n = len(totals)

    def body(*refs):
        outs = refs[n : 2 * n]
        send_sems, recv_sems = refs[2 * n :]
        x, y, c, _ = _place()
        copies = []
        for a in range(n):
            hr = outs[a].shape[0] // 2
            my_half = outs[a].at[pl.ds(c * hr, hr), :]
            cp = pltpu.make_async_remote_copy(
                src_ref=my_half, dst_ref=my_half, send_sem=send_sems.at[a], recv_sem=recv_sems.at[a],
                device_id=(x, y, 1 - c), device_id_type=MESH_ID,
            )
            cp.start()
            copies.append(cp)
        for a in range(n):
            hr = outs[a].shape[0] // 2
            sibling_half = outs[a].at[pl.ds((1 - c) * hr, hr), :]
            pltpu.make_async_remote_copy(
                src_ref=sibling_half, dst_ref=sibling_half, send_sem=send_sems.at[a], recv_sem=recv_sems.at[a],
                device_id=(x, y, 1 - c), device_id_type=MESH_ID,
            ).wait_recv()
        for cp in copies:
            cp.wait_send()

    return pl.pallas_call(
        body,
        name=name,
        in_specs=[HBM_SPEC] * n,
        out_specs=[HBM_SPEC] * n,
        out_shape=[_shape(t.shape, F32) for t in totals],
        input_output_aliases={a: a for a in range(n)},
        scratch_shapes=[pltpu.SemaphoreType.DMA((n,)), pltpu.SemaphoreType.DMA((n,))],
    )(*totals)


def _reduce_to_owner(grads, tag):
    received = _exchange_halves(grads, name=f"exchange_halves_{tag}")
    partials = [
        _add_sibling_half(g, r, name=f"add_sibling_{tag}_{a}") for a, (g, r) in enumerate(zip(grads, received))
    ]
    slots = _scatter_partials(partials, name=f"scatter_partials_{tag}")
    totals = [_sum_chip_partials(s, name=f"sum_partials_{tag}_{a}") for a, s in enumerate(slots)]
    return _share_halves(totals, name=f"share_halves_{tag}")


def kernel(x, w_in, conv_w, pool_w, pool_scale, w_out, norm_mix, norm_mlp, w_up, w_down, norm_final, loss_target, m_w_in, m_conv_w, m_pool_w, m_pool_scale, m_w_out, m_norm_mix, m_norm_mlp, m_w_up, m_w_down, m_norm_final, v_w_in, v_conv_w, v_pool_w, v_pool_scale, v_w_out, v_norm_mix, v_norm_mlp, v_w_up, v_w_down, v_norm_final):
    n_layers, d, _ = w_in.shape
    s = x.shape[1]
    dc = conv_w.shape[2] * N_CHIPS
    n_groups, cg_rows, cg = pool_w.shape[1:]
    dp = n_groups * cg
    x0 = x.reshape(s, d)
    target = loss_target.reshape(s, d)

    big = [w_in, w_out, w_up, w_down]
    big_bf16 = [
        _cast_bf16(w.reshape(-1, w.shape[2]), name=f"cast_{t}").reshape(w.shape)
        for t, w in zip(("w_in", "w_out", "w_up", "w_down"), big)
    ]

    conv_rows = jnp.pad(conv_w, ((0, 0), (0, 8 - CONV_TAPS), (0, 0))).reshape(n_layers * 8, -1)
    pool_rows = pool_w.reshape(n_layers * n_groups * cg_rows, cg)
    conv_all, pool_all = _gather_small(conv_rows, pool_rows, name="gather_small")
    conv_full = conv_all.reshape(N_CHIPS, n_layers, 8, -1).transpose(1, 2, 0, 3).reshape(n_layers, 8, dc)
    pool_full = (
        pool_all.reshape(N_CHIPS, n_layers, n_groups, cg_rows, cg).transpose(1, 2, 0, 3, 4)
        .reshape(n_layers, n_groups, cg, cg).astype(BF16)
    )

    saved = []
    xl = x0
    for l in range(n_layers):
        win_g, wout_g, wup_g, wdown_g = _gather_weights(big_bf16, l, name=f"gather_weights_l{l}")
        gain_mix = norm_mix[l].reshape(1, d)
        gain_mlp = norm_mlp[l].reshape(1, d)
        scale = pool_scale[l].reshape(1, dp)
        h1, proj = _norm_matmul(xl, gain_mix, win_g, relu2=False, name=f"in_proj_l{l}")
        y = _mixer_fwd(proj, conv_full[l], pool_full[l], scale, name=f"mixer_fwd_l{l}")
        x_mid = _matmul_residual(y, wout_g.reshape(-1, d), xl, name=f"out_proj_l{l}")
        h2, u, u2 = _norm_matmul(x_mid, gain_mlp, wup_g, relu2=True, name=f"up_proj_l{l}")
        x_next = _matmul_residual(u2, wdown_g.reshape(-1, d), x_mid, name=f"down_proj_l{l}")
        saved.append((xl, h1, proj, y, x_mid, h2, u, u2, win_g, wout_g, wup_g, wdown_g, gain_mix, gain_mlp, scale))
        xl = x_next

    loss_part, dx, dx_bf16, d_norm_final = _loss_and_grad(xl, norm_final.reshape(1, d), target, name="loss_head")
    loss = lax.psum(loss_part[0, 0], MESH_AXES)

    small_grads = [None] * n_layers
    carried = [None] * 5
    params = [(w_in, m_w_in, v_w_in), (w_out, m_w_out, v_w_out), (w_up, m_w_up, v_w_up), (w_down, m_w_down, v_w_down)]
    pool_params = tuple(p.reshape(n_layers, n_groups * cg_rows, cg) for p in (pool_w, m_pool_w, v_pool_w))
    for l in reversed(range(n_layers)):
        xl, h1, proj, y, x_mid, h2, u, u2, win_g, wout_g, wup_g, wdown_g, gain_mix, gain_mlp, scale = saved[l]
        g_down = _matmul_tn(u2, dx_bf16, n_shards=N_CHIPS, shard_cols=False, name=f"grad_w_down_l{l}")
        da = _matmul_nt(dx_bf16, wdown_g.reshape(-1, d), u=u, name=f"grad_act_l{l}")
        g_up = _matmul_tn(h2, da, n_shards=N_CHIPS, shard_cols=True, name=f"grad_w_up_l{l}")
        dx_mid, dx_mid_bf16, d_gain_mlp = _matmul_nt_norm_bwd(da, wup_g, x_mid, gain_mlp, dx, name=f"grad_mid_l{l}")
        g_out = _matmul_tn(y, dx_mid_bf16, n_shards=N_CHIPS, shard_cols=False, name=f"grad_w_out_l{l}")
        dy = _matmul_nt(dx_mid_bf16, wout_g.reshape(-1, d), name=f"grad_mixed_l{l}")
        dproj, d_conv, d_pool, d_scale = _mixer_bwd(dy, proj, conv_full[l], pool_full[l], scale, name=f"mixer_bwd_l{l}")
        g_in = _matmul_tn(h1, dproj, n_shards=N_CHIPS, shard_cols=True, name=f"grad_w_in_l{l}")
        dx, dx_bf16, d_gain_mix = _matmul_nt_norm_bwd(dproj, win_g, xl, gain_mix, dx_mid, name=f"grad_x_l{l}")

        g_pool = (
            d_pool.reshape(n_groups, N_CHIPS, cg_rows, cg).transpose(1, 0, 2, 3)
            .reshape(N_CHIPS, n_groups * cg_rows, cg).astype(BF16)
        )
        totals = _reduce_to_owner([g_in, g_out, g_up, g_down, g_pool], f"l{l}")
        for a, (w, m, v) in enumerate(params + [pool_params]):
            carried[a] = _adamw(totals[a], w, m, v, l, carried[a], name=f"adamw_{a}_l{l}")
        small_grads[l] = jnp.concatenate(
            [d_conv[:CONV_TAPS].reshape(-1), d_scale.reshape(-1), d_gain_mix.reshape(-1), d_gain_mlp.reshape(-1)]
        )

    vec = jnp.concatenate(small_grads + [d_norm_final.reshape(-1)])
    vec = _allreduce_small(vec.reshape(8, -1), name="allreduce_small").reshape(-1)
    per_layer = vec[: n_layers * (CONV_TAPS * dc + dp + 2 * d)].reshape(n_layers, -1)
    chip = _chip_index(lax.axis_index("x"), lax.axis_index("y"))
    dcs = dc // N_CHIPS
    g_conv = lax.dynamic_slice_in_dim(per_layer[:, : CONV_TAPS * dc].reshape(n_layers, CONV_TAPS, dc), chip * dcs, dcs, axis=2)
    g_scale = per_layer[:, CONV_TAPS * dc : CONV_TAPS * dc + dp]
    g_mix = per_layer[:, CONV_TAPS * dc + dp : CONV_TAPS * dc + dp + d]
    g_mlp = per_layer[:, CONV_TAPS * dc + dp + d :]
    g_final = vec[n_layers * (CONV_TAPS * dc + dp + 2 * d) :]

    def small_adamw(g, w, m, v, tag):
        flat = lambda t: t.reshape(1, -1, t.shape[-1])
        out = _adamw(flat(g)[0], flat(w), flat(m), flat(v), 0, None, name=f"adamw_{tag}")
        return [o.reshape(w.shape) for o in out]

    o_conv = small_adamw(g_conv, conv_w, m_conv_w, v_conv_w, "conv_w")
    o_scale = small_adamw(g_scale, pool_scale, m_pool_scale, v_pool_scale, "pool_scale")
    o_mix = small_adamw(g_mix, norm_mix, m_norm_mix, v_norm_mix, "norm_mix")
    o_mlp = small_adamw(g_mlp, norm_mlp, m_norm_mlp, v_norm_mlp, "norm_mlp")
    o_final = small_adamw(g_final, norm_final, m_norm_final, v_norm_final, "norm_final")
    o_in, o_out, o_up, o_down, o_pool = carried
    o_pool = [o.reshape(pool_w.shape) for o in o_pool]

    ordered = [o_in, o_conv, o_pool, o_scale, o_out, o_mix, o_mlp, o_up, o_down, o_final]
    return (loss, dx.reshape(x.shape), *[o[0] for o in ordered], *[o[1] for o in ordered], *[o[2] for o in ordered],
            *[o[3] for o in ordered])
```

```python
import functools

import jax
import jax.numpy as jnp
from jax import lax
from jax.experimental import pallas as pl
from jax.experimental.pallas import tpu as pltpu
from jax.experimental.pallas import tpu_sc as plsc

F32 = jnp.float32
BF16 = jnp.bfloat16

EPS = 1e-6
POOL_WINDOWS = (2, 4, 8, 16)
CONV_TAPS = 3
HALO = 16

ADAM_LR = 0.001
ADAM_B1 = 0.9
ADAM_B2 = 0.999
ADAM_EPS = 1e-08
ADAM_WD = 0.01
ADAM_STEP = 10

N_CHIPS = 4
MESH_AXES = ("x", "y", "c")
V7X_VMEM_LIMIT_BYTES = 56 * 1024 * 1024
SUBLANES_BF16 = 16
LANES = 128

HBM_SPEC = pl.BlockSpec(memory_space=pltpu.HBM)
VMEM_SPEC = pl.BlockSpec(memory_space=pltpu.VMEM)
MESH_ID = pl.DeviceIdType.MESH


def _tile(dim, target, align):
    if dim <= target:
        return dim
    t = (target // align) * align
    while dim % t:
        t -= align
    assert t > 0, (dim, target, align)
    return t


def _compute_params(*semantics):
    return pltpu.CompilerParams(dimension_semantics=semantics, vmem_limit_bytes=V7X_VMEM_LIMIT_BYTES)


def _shape(shape, dtype):
    return jax.ShapeDtypeStruct(shape, dtype)


def _rmsnorm_bwd(dh, x, gain, dres):
    r = lax.rsqrt(jnp.mean(x * x, axis=-1, keepdims=True) + EPS)
    xn = x * r
    dgain = jnp.sum(dh * xn, axis=0, keepdims=True)
    dxn = dh * gain
    dx = r * (dxn - xn * jnp.mean(dxn * xn, axis=-1, keepdims=True))
    if dres is not None:
        dx = dx + dres
    return dx, dgain


def _norm_matmul(x, gain, w, *, relu2, name):
    s, d = x.shape
    nsh, _, c = w.shape
    tm = _tile(s, 512, SUBLANES_BF16)
    tn = _tile(c, 1024, LANES)
    per_shard = c // tn

    def body(x_ref, g_ref, w_ref, h_ref, *out_refs):
        @pl.when(pl.program_id(1) == 0)
        def _():
            xf = x_ref[...]
            r = lax.rsqrt(jnp.mean(xf * xf, axis=-1, keepdims=True) + EPS)
            h_ref[...] = ((xf * r) * g_ref[...]).astype(BF16)

        acc = jnp.dot(h_ref[...], w_ref[...], preferred_element_type=F32)
        if relu2:
            u = jnp.maximum(acc, 0.0)
            out_refs[0][...] = u.astype(BF16)
            out_refs[1][...] = (u * u).astype(BF16)
        else:
            out_refs[0][...] = acc

    tile_out = pl.BlockSpec((tm, tn), lambda i, j: (i, j))
    n = nsh * c
    outs = [_shape((s, d), BF16)] + ([_shape((s, n), BF16)] * 2 if relu2 else [_shape((s, n), F32)])
    return pl.pallas_call(
        body,
        name=name,
        grid=(s // tm, n // tn),
        in_specs=[
            pl.BlockSpec((tm, d), lambda i, j: (i, 0)),
            pl.BlockSpec((1, d), lambda i, j: (0, 0)),
            pl.BlockSpec((None, d, tn), lambda i, j: (j // per_shard, 0, j % per_shard)),
        ],
        out_specs=[pl.BlockSpec((tm, d), lambda i, j: (i, 0))] + [tile_out] * (len(outs) - 1),
        out_shape=outs,
        compiler_params=_compute_params("parallel", "arbitrary"),
    )(x, gain, w)


def _matmul_residual(a, w, res, *, name):
    s, k = a.shape
    _, n = w.shape
    tm = _tile(s, 512, SUBLANES_BF16)
    tn = _tile(n, 1024, LANES)
    tk = _tile(k, 2048, LANES)
    nk = k // tk

    def body(a_ref, w_ref, r_ref, o_ref, acc_ref):
        kk = pl.program_id(2)

        @pl.when(kk == 0)
        def _():
            acc_ref[...] = r_ref[...]

        acc_ref[...] += jnp.dot(a_ref[...], w_ref[...], preferred_element_type=F32)

        @pl.when(kk == nk - 1)
        def _():
            o_ref[...] = acc_ref[...]

    return pl.pallas_call(
        body,
        name=name,
        grid=(s // tm, n // tn, nk),
        in_specs=[
            pl.BlockSpec((tm, tk), lambda i, j, kk: (i, kk)),
            pl.BlockSpec((tk, tn), lambda i, j, kk: (kk, j)),
            pl.BlockSpec((tm, tn), lambda i, j, kk: (i, j)),
        ],
        out_specs=pl.BlockSpec((tm, tn), lambda i, j, kk: (i, j)),
        out_shape=_shape((s, n), F32),
        scratch_shapes=[pltpu.VMEM((tm, tn), F32)],
        compiler_params=_compute_params("parallel", "parallel", "arbitrary"),
    )(a, w, res)


def _matmul_tn(a, g, *, n_shards, shard_cols, name):
    m, kd = a.shape
    _, n = g.shape
    r, c = (kd, n // n_shards) if shard_cols else (kd // n_shards, n)
    tr = _tile(r, 1024, LANES)
    tn = _tile(c, 1024, LANES)
    tm = _tile(m, 1024, SUBLANES_BF16)
    nm = m // tm
    rows_per_shard = r // tr
    cols_per_shard = c // tn

    def body(a_ref, g_ref, o_ref, acc_ref):
        mm = pl.program_id(2)

        @pl.when(mm == 0)
        def _():
            acc_ref[...] = jnp.zeros_like(acc_ref)

        acc_ref[...] += lax.dot_general(
            a_ref[...], g_ref[...], (((0,), (0,)), ((), ())), preferred_element_type=F32
        )

        @pl.when(mm == nm - 1)
        def _():
            o_ref[...] = acc_ref[...].astype(BF16)

    if shard_cols:
        out_map = lambda i, j, mm: (j // cols_per_shard, i, j % cols_per_shard)
    else:
        out_map = lambda i, j, mm: (i // rows_per_shard, i % rows_per_shard, j)
    return pl.pallas_call(
        body,
        name=name,
        grid=(kd // tr, n // tn, nm),
        in_specs=[
            pl.BlockSpec((tm, tr), lambda i, j, mm: (mm, i)),
            pl.BlockSpec((tm, tn), lambda i, j, mm: (mm, j)),
        ],
        out_specs=pl.BlockSpec((None, tr, tn), out_map),
        out_shape=_shape((n_shards, r, c), BF16),
        scratch_shapes=[pltpu.VMEM((tr, tn), F32)],
        compiler_params=_compute_params("parallel", "parallel", "arbitrary"),
    )(a, g)


def _matmul_nt(g, w, *, u=None, out_dtype=F32, name):
    m, n = g.shape
    kd, _ = w.shape
    tm = _tile(m, 512, SUBLANES_BF16)
    tj = _tile(kd, 1024, LANES)
    tk = _tile(n, 2048, LANES)
    nk = n // tk

    def body(*refs):
        if u is None:
            g_ref, w_ref, o_ref, acc_ref = refs
        else:
            g_ref, w_ref, u_ref, o_ref, acc_ref = refs
        kk = pl.program_id(2)

        @pl.when(kk == 0)
        def _():
            acc_ref[...] = jnp.zeros_like(acc_ref)

        acc_ref[...] += lax.dot_general(
            g_ref[...], w_ref[...], (((1,), (1,)), ((), ())), preferred_element_type=F32
        )

        @pl.when(kk == nk - 1)
        def _():
            if u is None:
                o_ref[...] = acc_ref[...].astype(o_ref.dtype)
            else:
                o_ref[...] = (acc_ref[...] * (2.0 * u_ref[...].astype(F32))).astype(o_ref.dtype)

    tile_out = pl.BlockSpec((tm, tj), lambda i, j, kk: (i, j))
    in_specs = [
        pl.BlockSpec((tm, tk), lambda i, j, kk: (i, kk)),
        pl.BlockSpec((tj, tk), lambda i, j, kk: (j, kk)),
    ]
    args = [g, w]
    if u is not None:
        in_specs.append(tile_out)
        args.append(u)
    return pl.pallas_call(
        body,
        name=name,
        grid=(m // tm, kd // tj, nk),
        in_specs=in_specs,
        out_specs=tile_out,
        out_shape=_shape((m, kd), BF16 if u is not None else out_dtype),
        scratch_shapes=[pltpu.VMEM((tm, tj), F32)],
        compiler_params=_compute_params("parallel", "parallel", "arbitrary"),
    )(*args)


NORM_BWD_ROWS = 64


def _matmul_nt_norm_bwd(g, w, x, gain, dres, *, name):
    s, n = g.shape
    nsh, d, c = w.shape
    tm = _tile(s, 512, SUBLANES_BF16)
    tk = _tile(c, 1024, LANES)
    per_shard = c // tk
    nk = n // tk
    rows = min(NORM_BWD_ROWS, tm)

    def body(g_ref, w_ref, x_ref, gain_ref, dres_ref, dx_ref, dxb_ref, dg_ref, acc_ref):
        i = pl.program_id(0)
        kk = pl.program_id(1)

        @pl.when(kk == 0)
        def _():
            acc_ref[...] = jnp.zeros_like(acc_ref)

        @pl.when((kk == 0) & (i == 0))
        def _():
            dg_ref[...] = jnp.zeros_like(dg_ref)

        acc_ref[...] += lax.dot_general(
            g_ref[...], w_ref[...], (((1,), (1,)), ((), ())), preferred_element_type=F32
        )

        @pl.when(kk == nk - 1)
        def _():
            def step(t, carry):
                rs = pl.ds(pl.multiple_of(t * rows, rows), rows)
                dx, dgain = _rmsnorm_bwd(acc_ref[rs, :], x_ref[rs, :], gain_ref[...], dres_ref[rs, :])
                dx_ref[rs, :] = dx
                dxb_ref[rs, :] = dx.astype(BF16)
                dg_ref[...] += dgain
                return carry

            lax.fori_loop(0, tm // rows, step, 0)

    row_tile = pl.BlockSpec((tm, d), lambda i, kk: (i, 0))
    vec = pl.BlockSpec((1, d), lambda i, kk: (0, 0))
    return pl.pallas_call(
        body,
        name=name,
        grid=(s // tm, nk),
        in_specs=[
            pl.BlockSpec((tm, tk), lambda i, kk: (i, kk)),
            pl.BlockSpec((None, d, tk), lambda i, kk: (kk // per_shard, 0, kk % per_shard)),
            row_tile,
            vec,
            row_tile,
        ],
        out_specs=[row_tile, row_tile, vec],
        out_shape=[_shape((s, d), F32), _shape((s, d), BF16), _shape((1, d), F32)],
        scratch_shapes=[pltpu.VMEM((tm, d), F32)],
        compiler_params=_compute_params("arbitrary", "arbitrary"),
    )(g, w, x, gain, dres)


def _loss_and_grad(x, gain, target, *, name):
    s, d = x.shape
    tm = _tile(s, 256, SUBLANES_BF16)

    def body(x_ref, gain_ref, t_ref, loss_ref, dx_ref, dxb_ref, dg_ref):
        @pl.when(pl.program_id(0) == 0)
        def _():
            loss_ref[...] = jnp.zeros_like(loss_ref)
            dg_ref[...] = jnp.zeros_like(dg_ref)

        xf = x_ref[...]
        r = lax.rsqrt(jnp.mean(xf * xf, axis=-1, keepdims=True) + EPS)
        err = (xf * r) * gain_ref[...] - t_ref[...]
        loss_ref[...] += 0.5 * jnp.sum(jnp.mean(err * err, axis=-1, keepdims=True))
        dx, dgain = _rmsnorm_bwd(err * (1.0 / d), xf, gain_ref[...], None)
        dx_ref[...] = dx
        dxb_ref[...] = dx.astype(BF16)
        dg_ref[...] += dgain

    row_tile = pl.BlockSpec((tm, d), lambda i: (i, 0))
    vec = pl.BlockSpec((1, d), lambda i: (0, 0))
    return pl.pallas_call(
        body,
        name=name,
        grid=(s // tm,),
        in_specs=[row_tile, vec, row_tile],
        out_specs=[pl.BlockSpec((1, LANES), lambda i: (0, 0)), row_tile, row_tile, vec],
        out_shape=[_shape((1, LANES), F32), _shape((s, d), F32), _shape((s, d), BF16), _shape((1, d), F32)],
        compiler_params=_compute_params("arbitrary"),
    )(x, gain, target)


def _trailing_sums(v_ext, window):
    acc, span = v_ext, 1
    while span < window:
        acc = acc + pltpu.roll(acc, span, 0)
        span *= 2
    return acc


def _leading_sums(q_ext, window):
    n = q_ext.shape[0]
    acc, span = q_ext, 1
    while span < window:
        acc = acc + pltpu.roll(acc, n - span, 0)
        span *= 2
    return acc


def _inverse_counts(first_token, rows, window):
    t = first_token + lax.broadcasted_iota(jnp.int32, (rows, 1), 0)
    return 1.0 / jnp.minimum(t + 1, window).astype(F32)


def _mixer_fwd(proj, conv_w, pool_w, pool_scale, *, name):
    s, _ = proj.shape
    dc = conv_w.shape[1]
    n_groups, cg, _ = pool_w.shape
    dp = n_groups * cg
    assert dc == dp and all(w & (w - 1) == 0 and w <= HALO for w in POOL_WINDOWS)
    ts = _tile(s, 256, HALO)
    halo_blocks = ts // HALO

    def body(b_ref, c_ref, xt_ref, v_ref, ch_ref, xth_ref, vh_ref, cw_ref, pw_ref, ps_ref, y_ref):
        i = pl.program_id(0)
        has_past = i > 0
        u_ext = jnp.concatenate(
            [jnp.where(has_past, ch_ref[...] * xth_ref[...], 0.0), c_ref[...] * xt_ref[...]], axis=0
        )
        conv = (
            cw_ref[2:3, :] * u_ext[HALO:]
            + cw_ref[1:2, :] * pltpu.roll(u_ext, 1, 0)[HALO:]
            + cw_ref[0:1, :] * pltpu.roll(u_ext, 2, 0)[HALO:]
        )
        y_ref[:, 0:dc] = (b_ref[...] * conv).astype(BF16)
        for gi, window in enumerate(POOL_WINDOWS):
            cols = slice(gi * cg, (gi + 1) * cg)
            v_ext = jnp.concatenate([jnp.where(has_past, vh_ref[:, cols], 0.0), v_ref[:, cols]], axis=0)
            mean = _trailing_sums(v_ext, window)[HALO:] * _inverse_counts(i * ts, ts, window)
            diff = (mean - v_ext[HALO:]).astype(BF16)
            z = jnp.dot(diff, pw_ref[gi], preferred_element_type=F32)
            y_ref[:, dc + gi * cg : dc + (gi + 1) * cg] = (z * ps_ref[:, cols]).astype(BF16)

    def col(jc):
        return pl.BlockSpec((ts, dc), lambda i: (i, jc))

    def past(jc):
        return pl.BlockSpec((HALO, dc), lambda i: (jnp.maximum(i * halo_blocks - 1, 0), jc))

    return pl.pallas_call(
        body,
        name=name,
        grid=(s // ts,),
        in_specs=[
            col(0), col(1), col(2), col(3), past(1), past(2), past(3),
            pl.BlockSpec((8, dc), lambda i: (0, 0)),
            pl.BlockSpec((n_groups, cg, cg), lambda i: (0, 0, 0)),
            pl.BlockSpec((1, dp), lambda i: (0, 0)),
        ],
        out_specs=pl.BlockSpec((ts, dc + dp), lambda i: (i, 0)),
        out_shape=_shape((s, dc + dp), BF16),
        compiler_params=_compute_params("parallel"),
    )(proj, proj, proj, proj, proj, proj, proj, conv_w, pool_w, pool_scale)


def _mixer_bwd(dy, proj, conv_w, pool_w, pool_scale, *, name):
    s, e = proj.shape
    dc = conv_w.shape[1]
    n_groups, cg, _ = pool_w.shape
    dp = n_groups * cg
    ts = _tile(s, 256, HALO)
    halo_blocks = ts // HALO
    n_tiles = s // ts
    n_halo_blocks = s // HALO
    n_ext = ts + HALO

    def body(dyc_ref, dyp_ref, dycn_ref, dypn_ref, b_ref, c_ref, xt_ref, v_ref, bn_ref, ch_ref, xth_ref, vh_ref,
             cw_ref, pw_ref, ps_ref, dproj_ref, dcw_ref, dpw_ref, dps_ref):
        i = pl.program_id(0)
        has_past = i > 0
        has_next = i < n_tiles - 1

        @pl.when(i == 0)
        def _():
            dcw_ref[...] = jnp.zeros_like(dcw_ref)
            dpw_ref[...] = jnp.zeros_like(dpw_ref)
            dps_ref[...] = jnp.zeros_like(dps_ref)

        c_now, xt_now, b_now = c_ref[...], xt_ref[...], b_ref[...]
        u_ext = jnp.concatenate([jnp.where(has_past, ch_ref[...] * xth_ref[...], 0.0), c_now * xt_now], axis=0)
        u0 = u_ext[HALO:]
        u1 = pltpu.roll(u_ext, 1, 0)[HALO:]
        u2 = pltpu.roll(u_ext, 2, 0)[HALO:]
        dyc = dyc_ref[...]
        conv = cw_ref[2:3, :] * u0 + cw_ref[1:2, :] * u1 + cw_ref[0:1, :] * u2
        dproj_ref[:, 0:dc] = (dyc * conv).astype(BF16)
        dconv = dyc * b_now
        dconv_ext = jnp.concatenate([dconv, jnp.where(has_next, dycn_ref[...] * bn_ref[...], 0.0)], axis=0)
        du = (
            cw_ref[2:3, :] * dconv
            + cw_ref[1:2, :] * pltpu.roll(dconv_ext, n_ext - 1, 0)[:ts]
            + cw_ref[0:1, :] * pltpu.roll(dconv_ext, n_ext - 2, 0)[:ts]
        )
        dproj_ref[:, dc : 2 * dc] = (du * xt_now).astype(BF16)
        dproj_ref[:, 2 * dc : 3 * dc] = (du * c_now).astype(BF16)
        dcw_ref[0:1, :] += jnp.sum(dconv * u2, axis=0, keepdims=True)
        dcw_ref[1:2, :] += jnp.sum(dconv * u1, axis=0, keepdims=True)
        dcw_ref[2:3, :] += jnp.sum(dconv * u0, axis=0, keepdims=True)

        for gi, window in enumerate(POOL_WINDOWS):
            cols = slice(gi * cg, (gi + 1) * cg)
            v_ext = jnp.concatenate([jnp.where(has_past, vh_ref[:, cols], 0.0), v_ref[:, cols]], axis=0)
            mean = _trailing_sums(v_ext, window)[HALO:] * _inverse_counts(i * ts, ts, window)
            diff = (mean - v_ext[HALO:]).astype(BF16)
            z = jnp.dot(diff, pw_ref[gi], preferred_element_type=F32)
            dyp = dyp_ref[:, cols]
            dps_ref[:, cols] += jnp.sum(dyp * z, axis=0, keepdims=True)
            scale = ps_ref[:, cols]
            dz_ext = jnp.concatenate([dyp * scale, jnp.where(has_next, dypn_ref[:, cols] * scale, 0.0)], axis=0)
            dz_ext = dz_ext.astype(BF16)
            dpw_ref[gi] += lax.dot_general(
                diff, dz_ext[:ts], (((0,), (0,)), ((), ())), preferred_element_type=F32
            )
            ddiff_ext = lax.dot_general(
                dz_ext, pw_ref[gi], (((1,), (1,)), ((), ())), preferred_element_type=F32
            )
            q_ext = ddiff_ext * _inverse_counts(i * ts, n_ext, window)
            dv = _leading_sums(q_ext, window)[:ts] - ddiff_ext[:ts]
            dproj_ref[:, 3 * dc + gi * cg : 3 * dc + (gi + 1) * cg] = dv.astype(BF16)

    def col(jc):
        return pl.BlockSpec((ts, dc), lambda i: (i, jc))

    def past(jc):
        return pl.BlockSpec((HALO, dc), lambda i: (jnp.maximum(i * halo_blocks - 1, 0), jc))

    def following(jc):
        return pl.BlockSpec((HALO, dc), lambda i: (jnp.minimum((i + 1) * halo_blocks, n_halo_blocks - 1), jc))

    return pl.pallas_call(
        body,
        name=name,
        grid=(n_tiles,),
        in_specs=[
            col(0), col(1), following(0), following(1),
            col(0), col(1), col(2), col(3), following(0), past(1), past(2), past(3),
            pl.BlockSpec((8, dc), lambda i: (0, 0)),
            pl.BlockSpec((n_groups, cg, cg), lambda i: (0, 0, 0)),
            pl.BlockSpec((1, dp), lambda i: (0, 0)),
        ],
        out_specs=[
            pl.BlockSpec((ts, e), lambda i: (i, 0)),
            pl.BlockSpec((8, dc), lambda i: (0, 0)),
            pl.BlockSpec((n_groups, cg, cg), lambda i: (0, 0, 0)),
            pl.BlockSpec((1, dp), lambda i: (0, 0)),
        ],
        out_shape=[_shape((s, e), BF16), _shape((8, dc), F32), _shape((n_groups, cg, cg), F32), _shape((1, dp), F32)],
        compiler_params=_compute_params("arbitrary"),
    )(dy, dy, dy, dy, proj, proj, proj, proj, proj, proj, proj, proj, conv_w, pool_w, pool_scale)


def _cast_bf16(a, *, name):
    rows, cols = a.shape
    tr = _tile(rows, 512, SUBLANES_BF16)

    def body(a_ref, o_ref):
        o_ref[...] = a_ref[...].astype(BF16)

    spec = pl.BlockSpec((tr, cols), lambda i: (i, 0))
    return pl.pallas_call(
        body, name=name, grid=(rows // tr,), in_specs=[spec], out_specs=spec, out_shape=_shape((rows, cols), BF16),
        compiler_params=_compute_params("parallel"),
    )(a)


def _core_index():
    return lax.axis_index("c").astype(jnp.int32).reshape((1,))


def _add_sibling_half(grads, received, *, name):
    nsh, r, c = grads.shape
    hr = r // 2
    tr = _tile(hr, 512, SUBLANES_BF16)
    tiles = hr // tr

    def body(core_ref, g_ref, r_ref, o_ref):
        del core_ref
        o_ref[...] = (g_ref[...].astype(F32) + r_ref[...].astype(F32)).astype(BF16)

    half = pl.BlockSpec((None, tr, c), lambda sh, t, core: (sh, t, 0))
    return pl.pallas_call(
        body,
        name=name,
        grid_spec=pltpu.PrefetchScalarGridSpec(
            num_scalar_prefetch=1,
            grid=(nsh, tiles),
            in_specs=[pl.BlockSpec((None, tr, c), lambda sh, t, core: (sh, core[0] * tiles + t, 0)), half],
            out_specs=half,
        ),
        out_shape=_shape((nsh, hr, c), BF16),
        compiler_params=_compute_params("parallel", "parallel"),
    )(_core_index(), grads, received)


def _sum_chip_partials(partials, *, name):
    nsh, hr, c = partials.shape
    tr = _tile(hr, 256, SUBLANES_BF16)
    tiles = hr // tr

    def body(core_ref, p_ref, o_ref):
        del core_ref
        total = p_ref[0].astype(F32)
        for sh in range(1, nsh):
            total = total + p_ref[sh].astype(F32)
        o_ref[...] = total

    return pl.pallas_call(
        body,
        name=name,
        grid_spec=pltpu.PrefetchScalarGridSpec(
            num_scalar_prefetch=1,
            grid=(tiles,),
            in_specs=[pl.BlockSpec((nsh, tr, c), lambda t, core: (0, t, 0))],
            out_specs=pl.BlockSpec((tr, c), lambda t, core: (core[0] * tiles + t, 0)),
        ),
        out_shape=_shape((2 * hr, c), F32),
        compiler_params=_compute_params("parallel"),
    )(_core_index(), partials)


def _adamw(grad, w, m, v, layer, carried, *, name):
    n_layers, r, c = w.shape
    tr = _tile(r, 256, 8)
    bias1 = 1.0 - ADAM_B1 ** ADAM_STEP
    bias2 = 1.0 - ADAM_B2 ** ADAM_STEP

    def body(g_ref, w_ref, m_ref, v_ref, *rest):
        go_ref, d_ref, mo_ref, vo_ref = rest[-4:]
        g = g_ref[...]
        m_new = ADAM_B1 * m_ref[...] + (1.0 - ADAM_B1) * g
        v_new = ADAM_B2 * v_ref[...] + (1.0 - ADAM_B2) * (g * g)
        go_ref[...] = g
        mo_ref[...] = m_new
        vo_ref[...] = v_new
        d_ref[...] = -ADAM_LR * ((m_new / bias1) / (jnp.sqrt(v_new / bias2) + ADAM_EPS) + ADAM_WD * w_ref[...])

    layer_tile = pl.BlockSpec((None, tr, c), lambda t: (layer, t, 0))
    in_specs = [pl.BlockSpec((tr, c), lambda t: (t, 0)), layer_tile, layer_tile, layer_tile]
    args = [grad, w, m, v]
    aliases = {}
    if carried is not None:
        in_specs += [HBM_SPEC] * 4
        aliases = {4 + n: n for n in range(4)}
        args += list(carried)
    return pl.pallas_call(
        body,
        name=name,
        grid=(r // tr,),
        in_specs=in_specs,
        out_specs=[layer_tile] * 4,
        out_shape=[_shape((n_layers, r, c), F32)] * 4,
        input_output_aliases=aliases,
        compiler_params=_compute_params("parallel"),
    )(*args)


def _place():
    x, y, c = (lax.axis_index(a) for a in MESH_AXES)
    other_chips = [(1 - x, y), (x, 1 - y), (1 - x, 1 - y)]
    return x, y, c, other_chips


def _chip_index(x, y):
    return 2 * x + y


def _gather_small(conv_rows, pool_rows, *, name):
    def body(cv_ref, pw_ref, cv_out, pw_out, send_sems, recv_sems):
        x, y, c, other_chips = _place()
        mine = _chip_index(x, y)
        cv_out[mine] = cv_ref[...]
        pw_out[mine] = pw_ref[...]
        sends = []
        for a, (src, dst) in enumerate(((cv_ref, cv_out), (pw_ref, pw_out))):
            for j, (px, py) in enumerate(other_chips):
                cp = pltpu.make_async_remote_copy(
                    src_ref=src, dst_ref=dst.at[mine], send_sem=send_sems.at[a, j], recv_sem=recv_sems.at[a, j],
                    device_id=(px, py, c), device_id_type=MESH_ID,
                )
                cp.start()
                sends.append(cp)
        for a, (src, dst) in enumerate(((cv_ref, cv_out), (pw_ref, pw_out))):
            for j, (px, py) in enumerate(other_chips):
                pltpu.make_async_remote_copy(
                    src_ref=src, dst_ref=dst.at[_chip_index(px, py)], send_sem=send_sems.at[a, j],
                    recv_sem=recv_sems.at[a, j], device_id=(px, py, c), device_id_type=MESH_ID,
                ).wait_recv()
        for cp in sends:
            cp.wait_send()

    return pl.pallas_call(
        body,
        name=name,
        in_specs=[VMEM_SPEC, VMEM_SPEC],
        out_specs=[VMEM_SPEC, VMEM_SPEC],
        out_shape=[_shape((N_CHIPS,) + conv_rows.shape, F32), _shape((N_CHIPS,) + pool_rows.shape, F32)],
        scratch_shapes=[pltpu.SemaphoreType.DMA((2, 3)), pltpu.SemaphoreType.DMA((2, 3))],
    )(conv_rows, pool_rows)


def _allreduce_small(vec, *, name):
    rows, n = vec.shape
    n_dev = 8

    def body(v_ref, o_ref, slots, send_sems, recv_sems):
        x, y, c, _ = _place()
        me = 4 * x + 2 * y + c
        slots[me] = v_ref[...]
        sends = []
        for mask in range(1, n_dev):
            fx, fy, fc = (mask >> 2) & 1, (mask >> 1) & 1, mask & 1
            peer = (x ^ fx, y ^ fy, c ^ fc)
            cp = pltpu.make_async_remote_copy(
                src_ref=v_ref, dst_ref=slots.at[me], send_sem=send_sems.at[mask - 1], recv_sem=recv_sems.at[mask - 1],
                device_id=peer, device_id_type=MESH_ID,
            )
            cp.start()
            sends.append(cp)
        for mask in range(1, n_dev):
            fx, fy, fc = (mask >> 2) & 1, (mask >> 1) & 1, mask & 1
            peer = (x ^ fx, y ^ fy, c ^ fc)
            pltpu.make_async_remote_copy(
                src_ref=v_ref, dst_ref=slots.at[4 * peer[0] + 2 * peer[1] + peer[2]], send_sem=send_sems.at[mask - 1],
                recv_sem=recv_sems.at[mask - 1], device_id=peer, device_id_type=MESH_ID,
            ).wait_recv()
        total = slots[0]
        for dev in range(1, n_dev):
            total = total + slots[dev]
        o_ref[...] = total
        for cp in sends:
            cp.wait_send()

    return pl.pallas_call(
        body,
        name=name,
        in_specs=[VMEM_SPEC],
        out_specs=VMEM_SPEC,
        out_shape=_shape((rows, n), F32),
        scratch_shapes=[
            pltpu.VMEM((n_dev, rows, n), F32),
            pltpu.SemaphoreType.DMA((n_dev - 1,)),
            pltpu.SemaphoreType.DMA((n_dev - 1,)),
        ],
    )(vec)


def _handshake(peers):
    barrier = pltpu.get_barrier_semaphore()
    for peer in peers:
        pl.semaphore_signal(barrier, inc=1, device_id=peer, device_id_type=MESH_ID)
    pl.semaphore_wait(barrier, len(peers))


GATHER_COLLECTIVE_ID = 1


def _gather_weights(stacked, layer, *, name):
    n = len(stacked)

    def body(*refs):
        ins, outs = refs[:n], refs[n : 2 * n]
        local_sems, send_sems, recv_sems, pass_send_sems, pass_recv_sems = refs[2 * n :]
        x, y, c, other_chips = _place()
        mine = _chip_index(x, y)
        sibling = (x, y, 1 - c)
        _handshake([sibling] + [(px, py, c) for px, py in other_chips])
        pending = []
        for a in range(n):
            hr = ins[a].shape[1] // 2
            my_half = pl.ds(c * hr, hr)
            cp = pltpu.make_async_copy(ins[a].at[layer], outs[a].at[mine], local_sems.at[a])
            cp.start()
            pending.append(cp.wait)
            for j, (px, py) in enumerate(other_chips):
                cp = pltpu.make_async_remote_copy(
                    src_ref=ins[a].at[layer, my_half], dst_ref=outs[a].at[mine, my_half], send_sem=send_sems.at[a, j],
                    recv_sem=recv_sems.at[a, j], device_id=(px, py, c), device_id_type=MESH_ID,
                )
                cp.start()
                pending.append(cp.wait_send)
        for a in range(n):
            hr = ins[a].shape[1] // 2
            my_half = pl.ds(c * hr, hr)
            for j, (px, py) in enumerate(other_chips):
                landed = outs[a].at[_chip_index(px, py), my_half]
                pltpu.make_async_remote_copy(
                    src_ref=landed, dst_ref=landed, send_sem=send_sems.at[a, j], recv_sem=recv_sems.at[a, j],
                    device_id=(px, py, c), device_id_type=MESH_ID,
                ).wait_recv()
                cp = pltpu.make_async_remote_copy(
                    src_ref=landed, dst_ref=landed, send_sem=pass_send_sems.at[a, j], recv_sem=pass_recv_sems.at[a, j],
                    device_id=sibling, device_id_type=MESH_ID,
                )
                cp.start()
                pending.append(cp.wait_send)
        for a in range(n):
            hr = ins[a].shape[1] // 2
            sibling_half = pl.ds((1 - c) * hr, hr)
            for j, (px, py) in enumerate(other_chips):
                passed = outs[a].at[_chip_index(px, py), sibling_half]
                pltpu.make_async_remote_copy(
                    src_ref=passed, dst_ref=passed, send_sem=pass_send_sems.at[a, j], recv_sem=pass_recv_sems.at[a, j],
                    device_id=sibling, device_id_type=MESH_ID,
                ).wait_recv()
        for wait in pending:
            wait()

    return pl.kernel(
        body,
        name=name,
        out_type=[_shape((N_CHIPS,) + a.shape[1:], BF16) for a in stacked],
        mesh=plsc.ScalarSubcoreMesh(axis_name="sequencer", num_cores=1),
        scratch_types=[pltpu.SemaphoreType.DMA((n,))] + [pltpu.SemaphoreType.DMA((n, 3))] * 4,
        compiler_params=pltpu.CompilerParams(collective_id=GATHER_COLLECTIVE_ID),
    )(*stacked)


def _exchange_halves(grads, *, name):
    n = len(grads)

    def body(*refs):
        ins, outs = refs[:n], refs[n : 2 * n]
        send_sems, recv_sems = refs[2 * n :]
        x, y, c, _ = _place()
        copies = []
        for a in range(n):
            hr = ins[a].shape[1] // 2
            cp = pltpu.make_async_remote_copy(
                src_ref=ins[a].at[:, pl.ds((1 - c) * hr, hr), :], dst_ref=outs[a], send_sem=send_sems.at[a],
                recv_sem=recv_sems.at[a], device_id=(x, y, 1 - c), device_id_type=MESH_ID,
            )
            cp.start()
            copies.append(cp)
        for cp in copies:
            cp.wait()

    return pl.pallas_call(
        body,
        name=name,
        in_specs=[HBM_SPEC] * n,
        out_specs=[HBM_SPEC] * n,
        out_shape=[_shape((g.shape[0], g.shape[1] // 2, g.shape[2]), BF16) for g in grads],
        scratch_shapes=[pltpu.SemaphoreType.DMA((n,)), pltpu.SemaphoreType.DMA((n,))],
    )(*grads)


def _scatter_partials(partials, *, name):
    n = len(partials)

    def body(*refs):
        ins, outs = refs[:n], refs[n : 2 * n]
        local_sems, send_sems, recv_sems = refs[2 * n :]
        x, y, c, other_chips = _place()
        mine = _chip_index(x, y)
        pending = []
        for a in range(n):
            cp = pltpu.make_async_copy(ins[a].at[mine], outs[a].at[mine], local_sems.at[a])
            cp.start()
            pending.append(cp.wait)
            for j, (px, py) in enumerate(other_chips):
                cp = pltpu.make_async_remote_copy(
                    src_ref=ins[a].at[_chip_index(px, py)], dst_ref=outs[a].at[mine], send_sem=send_sems.at[a, j],
                    recv_sem=recv_sems.at[a, j], device_id=(px, py, c), device_id_type=MESH_ID,
                )
                cp.start()
                pending.append(cp.wait_send)
        for a in range(n):
            for j, (px, py) in enumerate(other_chips):
                landed = outs[a].at[_chip_index(px, py)]
                pltpu.make_async_remote_copy(
                    src_ref=landed, dst_ref=landed, send_sem=send_sems.at[a, j], recv_sem=recv_sems.at[a, j],
                    device_id=(px, py, c), device_id_type=MESH_ID,
                ).wait_recv()
        for wait in pending:
            wait()

    return pl.pallas_call(
        body,
        name=name,
        in_specs=[HBM_SPEC] * n,
        out_specs=[HBM_SPEC] * n,
        out_shape=[_shape(p.shape, BF16) for p in partials],
        scratch_shapes=[pltpu.SemaphoreType.DMA((n,)), pltpu.SemaphoreType.DMA((n, 3)), pltpu.SemaphoreType.DMA((n, 3))],
    )(*partials)


def _share_halves(totals, *, name):
    n = len(totals)

    def body(*refs):
        outs = refs[n : 2 * n]
        send_sems, recv_sems = refs[2 * n :]
        x, y, c, _ = _place()
        copies = []
        for a in range(n):
            hr = outs[a].shape[0] // 2
            my_half = outs[a].at[pl.ds(c * hr, hr), :]
            cp = pltpu.make_async_remote_copy(
                src_ref=my_half, dst_ref=my_half, send_sem=send_sems.at[a], recv_sem=recv_sems.at[a],
                device_id=(x, y, 1 - c), device_id_type=MESH_ID,
            )
            cp.start()
            copies.append(cp)
        for a in range(n):
            hr = outs[a].shape[0] // 2
            sibling_half = outs[a].at[pl.ds((1 - c) * hr, hr), :]
            pltpu.make_async_remote_copy(
                src_ref=sibling_half, dst_ref=sibling_half, send_sem=send_sems.at[a], recv_sem=recv_sems.at[a],
                device_id=(x, y, 1 - c), device_id_type=MESH_ID,
            ).wait_recv()
        for cp in copies:
            cp.wait_send()

    return pl.pallas_call(
        body,
        name=name,
        in_specs=[HBM_SPEC] * n,
        out_specs=[HBM_SPEC] * n,
        out_shape=[_shape(t.shape, F32) for t in totals],
        input_output_aliases={a: a for a in range(n)},
        scratch_shapes=[pltpu.SemaphoreType.DMA((n,)), pltpu.SemaphoreType.DMA((n,))],
    )(*totals)


def _reduce_to_owner(grads, tag):
    received = _exchange_halves(grads, name=f"exchange_halves_{tag}")
    partials = [
        _add_sibling_half(g, r, name=f"add_sibling_{tag}_{a}") for a, (g, r) in enumerate(zip(grads, received))
    ]
    slots = _scatter_partials(partials, name=f"scatter_partials_{tag}")
    totals = [_sum_chip_partials(s, name=f"sum_partials_{tag}_{a}") for a, s in enumerate(slots)]
    return _share_halves(totals, name=f"share_halves_{tag}")


def kernel(x, w_in, conv_w, pool_w, pool_scale, w_out, norm_mix, norm_mlp, w_up, w_down, norm_final, loss_target, m_w_in, m_conv_w, m_pool_w, m_pool_scale, m_w_out, m_norm_mix, m_norm_mlp, m_w_up, m_w_down, m_norm_final, v_w_in, v_conv_w, v_pool_w, v_pool_scale, v_w_out, v_norm_mix, v_norm_mlp, v_w_up, v_w_down, v_norm_final):
    n_layers, d, _ = w_in.shape
    s = x.shape[1]
    dc = conv_w.shape[2] * N_CHIPS
    n_groups, cg_rows, cg = pool_w.shape[1:]
    dp = n_groups * cg
    x0 = x.reshape(s, d)
    target = loss_target.reshape(s, d)

    big = [w_in, w_out, w_up, w_down]
    big_bf16 = [
        _cast_bf16(w.reshape(-1, w.shape[2]), name=f"cast_{t}").reshape(w.shape)
        for t, w in zip(("w_in", "w_out", "w_up", "w_down"), big)
    ]

    conv_rows = jnp.pad(conv_w, ((0, 0), (0, 8 - CONV_TAPS), (0, 0))).reshape(n_layers * 8, -1)
    pool_rows = pool_w.reshape(n_layers * n_groups * cg_rows, cg)
    conv_all, pool_all = _gather_small(conv_rows, pool_rows, name="gather_small")
    conv_full = conv_all.reshape(N_CHIPS, n_layers, 8, -1).transpose(1, 2, 0, 3).reshape(n_layers, 8, dc)
    pool_full = (
        pool_all.reshape(N_CHIPS, n_layers, n_groups, cg_rows, cg).transpose(1, 2, 0, 3, 4)
        .reshape(n_layers, n_groups, cg, cg).astype(BF16)
    )

    saved = []
    xl = x0
    for l in range(n_layers):
        win_g, wout_g, wup_g, wdown_g = _gather_weights(big_bf16, l, name=f"gather_weights_l{l}")
        gain_mix = norm_mix[l].reshape(1, d)
        gain_mlp = norm_mlp[l].reshape(1, d)
        scale = pool_scale[l].reshape(1, dp)
        h1, proj = _norm_matmul(xl, gain_mix, win_g, relu2=False, name=f"in_proj_l{l}")
        y = _mixer_fwd(proj, conv_full[l], pool_full[l], scale, name=f"mixer_fwd_l{l}")
        x_mid = _matmul_residual(y, wout_g.reshape(-1, d), xl, name=f"out_proj_l{l}")
        h2, u, u2 = _norm_matmul(x_mid, gain_mlp, wup_g, relu2=True, name=f"up_proj_l{l}")
        x_next = _matmul_residual(u2, wdown_g.reshape(-1, d), x_mid, name=f"down_proj_l{l}")
        saved.append((xl, h1, proj, y, x_mid, h2, u, u2, win_g, wout_g, wup_g, wdown_g, gain_mix, gain_mlp, scale))
        xl = x_next

    loss_part, dx, dx_bf16, d_norm_final = _loss_and_grad(xl, norm_final.reshape(1, d), target, name="loss_head")
    loss = lax.psum(loss_part[0, 0], MESH_AXES)

    small_grads = [None] * n_layers
    carried = [None] * 5
    params = [(w_in, m_w_in, v_w_in), (w_out, m_w_out, v_w_out), (w_up, m_w_up, v_w_up), (w_down, m_w_down, v_w_down)]
    pool_params = tuple(p.reshape(n_layers, n_groups * cg_rows, cg) for p in (pool_w, m_pool_w, v_pool_w))
    for l in reversed(range(n_layers)):
        xl, h1, proj, y, x_mid, h2, u, u2, win_g, wout_g, wup_g, wdown_g, gain_mix, gain_mlp, scale = saved[l]
        g_down = _matmul_tn(u2, dx_bf16, n_shards=N_CHIPS, shard_cols=False, name=f"grad_w_down_l{l}")
        da = _matmul_nt(dx_bf16, wdown_g.reshape(-1, d), u=u, name=f"grad_act_l{l}")
        g_up = _matmul_tn(h2, da, n_shards=N_CHIPS, shard_cols=True, name=f"grad_w_up_l{l}")
        dx_mid, dx_mid_bf16, d_gain_mlp = _matmul_nt_norm_bwd(da, wup_g, x_mid, gain_mlp, dx, name=f"grad_mid_l{l}")
        g_out = _matmul_tn(y, dx_mid_bf16, n_shards=N_CHIPS, shard_cols=False, name=f"grad_w_out_l{l}")
        dy = _matmul_nt(dx_mid_bf16, wout_g.reshape(-1, d), name=f"grad_mixed_l{l}")
        dproj, d_conv, d_pool, d_scale = _mixer_bwd(dy, proj, conv_full[l], pool_full[l], scale, name=f"mixer_bwd_l{l}")
        g_in = _matmul_tn(h1, dproj, n_shards=N_CHIPS, shard_cols=True, name=f"grad_w_in_l{l}")
        dx, dx_bf16, d_gain_mix = _matmul_nt_norm_bwd(dproj, win_g, xl, gain_mix, dx_mid, name=f"grad_x_l{l}")

        g_pool = (
            d_pool.reshape(n_groups, N_CHIPS, cg_rows, cg).transpose(1, 0, 2, 3)
            .reshape(N_CHIPS, n_groups * cg_rows, cg).astype(BF16)
        )
        totals = _reduce_to_owner([g_in, g_out, g_up, g_down, g_pool], f"l{l}")
        for a, (w, m, v) in enumerate(params + [pool_params]):
            carried[a] = _adamw(totals[a], w, m, v, l, carried[a], name=f"adamw_{a}_l{l}")
        small_grads[l] = jnp.concatenate(
            [d_conv[:CONV_TAPS].reshape(-1), d_scale.reshape(-1), d_gain_mix.reshape(-1), d_gain_mlp.reshape(-1)]
        )

    vec = jnp.concatenate(small_grads + [d_norm_final.reshape(-1)])
    vec = _allreduce_small(vec.reshape(8, -1), name="allreduce_small").reshape(-1)
    per_layer = vec[: n_layers * (CONV_TAPS * dc + dp + 2 * d)].reshape(n_layers, -1)
    chip = _chip_index(lax.axis_index("x"), lax.axis_index("y"))
    dcs = dc // N_CHIPS
    g_conv = lax.dynamic_slice_in_dim(per_layer[:, : CONV_TAPS * dc].reshape(n_layers, CONV_TAPS, dc), chip * dcs, dcs, axis=2)
    g_scale = per_layer[:, CONV_TAPS * dc : CONV_TAPS * dc + dp]
    g_mix = per_layer[:, CONV_TAPS * dc + dp : CONV_TAPS * dc + dp + d]
    g_mlp = per_layer[:, CONV_TAPS * dc + dp + d :]
    g_final = vec[n_layers * (CONV_TAPS * dc + dp + 2 * d) :]

    def small_adamw(g, w, m, v, tag):
        flat = lambda t: t.reshape(1, -1, t.shape[-1])
        out = _adamw(flat(g)[0], flat(w), flat(m), flat(v), 0, None, name=f"adamw_{tag}")
        return [o.reshape(w.shape) for o in out]

    o_conv = small_adamw(g_conv, conv_w, m_conv_w, v_conv_w, "conv_w")
    o_scale = small_adamw(g_scale, pool_scale, m_pool_scale, v_pool_scale, "pool_scale")
    o_mix = small_adamw(g_mix, norm_mix, m_norm_mix, v_norm_mix, "norm_mix")
    o_mlp = small_adamw(g_mlp, norm_mlp, m_norm_mlp, v_norm_mlp, "norm_mlp")
    o_final = small_adamw(g_final, norm_final, m_norm_final, v_norm_final, "norm_final")
    o_in, o_out, o_up, o_down, o_pool = carried
    o_pool = [o.reshape(pool_w.shape) for o in o_pool]

    ordered = [o_in, o_conv, o_pool, o_scale, o_out, o_mix, o_mlp, o_up, o_down, o_final]
    return (loss, dx.reshape(x.shape), *[o[0] for o in ordered], *[o[1] for o in ordered], *[o[2] for o in ordered],
            *[o[3] for o in ordered])
```

```python
import functools

import jax
import jax.numpy as jnp
from jax import lax
from jax.experimental import pallas as pl
from jax.experimental.pallas import tpu as pltpu
from jax.experimental.pallas import tpu_sc as plsc

F32 = jnp.float32
BF16 = jnp.bfloat16

EPS = 1e-6
POOL_WINDOWS = (2, 4, 8, 16)
CONV_TAPS = 3
HALO = 16

ADAM_LR = 0.001
ADAM_B1 = 0.9
ADAM_B2 = 0.999
ADAM_EPS = 1e-08
ADAM_WD = 0.01
ADAM_STEP = 10

N_CHIPS = 4
MESH_AXES = ("x", "y", "c")
V7X_VMEM_LIMIT_BYTES = 56 * 1024 * 1024
SUBLANES_BF16 = 16
LANES = 128

HBM_SPEC = pl.BlockSpec(memory_space=pltpu.HBM)
VMEM_SPEC = pl.BlockSpec(memory_space=pltpu.VMEM)
MESH_ID = pl.DeviceIdType.MESH


def _tile(dim, target, align):
    if dim <= target:
        return dim
    t = (target // align) * align
    while dim % t:
        t -= align
    assert t > 0, (dim, target, align)
    return t


def _compute_params(*semantics):
    return pltpu.CompilerParams(dimension_semantics=semantics, vmem_limit_bytes=V7X_VMEM_LIMIT_BYTES)


def _shape(shape, dtype):
    return jax.ShapeDtypeStruct(shape, dtype)


ANY_SPEC = pl.BlockSpec(memory_space=pl.ANY)


def _behind(body, deps):
    return lambda *refs: body(*refs[len(deps):])


def _rmsnorm_bwd(dh, x, gain, dres):
    r = lax.rsqrt(jnp.mean(x * x, axis=-1, keepdims=True) + EPS)
    xn = x * r
    dgain = jnp.sum(dh * xn, axis=0, keepdims=True)
    dxn = dh * gain
    dx = r * (dxn - xn * jnp.mean(dxn * xn, axis=-1, keepdims=True))
    if dres is not None:
        dx = dx + dres
    return dx, dgain


MXU_ROWS = 512


def _rmsnorm(x, gain, *, name):
    s, d = x.shape
    tm = _tile(s, 256, SUBLANES_BF16)

    def body(x_ref, g_ref, h_ref):
        xf = x_ref[...]
        r = lax.rsqrt(jnp.mean(xf * xf, axis=-1, keepdims=True) + EPS)
        h_ref[...] = ((xf * r) * g_ref[...]).astype(BF16)

    row_tile = pl.BlockSpec((tm, d), lambda i: (i, 0))
    return pl.pallas_call(
        body, name=name, grid=(s // tm,), in_specs=[row_tile, pl.BlockSpec((1, d), lambda i: (0, 0))],
        out_specs=row_tile, out_shape=_shape((s, d), BF16), compiler_params=_compute_params("parallel"),
    )(x, gain)


def _matmul_cols(h, w, *, relu2, name):
    s, k = h.shape
    nsh, _, c = w.shape
    tm = _tile(s, 2048, MXU_ROWS)
    tn = _tile(c, 1024, LANES)
    rows = min(MXU_ROWS, tm)
    per_shard = c // tn

    def body(h_ref, w_ref, *out_refs):
        def step(t, carry):
            rs = pl.ds(pl.multiple_of(t * rows, rows), rows)
            acc = jnp.dot(h_ref[rs, :], w_ref[...], preferred_element_type=F32)
            if relu2:
                u = jnp.maximum(acc, 0.0)
                out_refs[0][rs, :] = u.astype(BF16)
                out_refs[1][rs, :] = (u * u).astype(BF16)
            else:
                out_refs[0][rs, :] = acc.astype(BF16)
            return carry

        lax.fori_loop(0, tm // rows, step, 0)

    tile_out = pl.BlockSpec((tm, tn), lambda i, j: (i, j))
    n = nsh * c
    outs = [_shape((s, n), BF16)] * (2 if relu2 else 1)
    result = pl.pallas_call(
        body,
        name=name,
        grid=(s // tm, n // tn),
        in_specs=[
            pl.BlockSpec((tm, k), lambda i, j: (i, 0)),
            pl.BlockSpec((None, k, tn), lambda i, j: (j // per_shard, 0, j % per_shard)),
        ],
        out_specs=[tile_out] * len(outs),
        out_shape=outs,
        compiler_params=_compute_params("parallel", "arbitrary"),
    )(h, w)
    return result if relu2 else result[0]


def _matmul_residual(a, w, res, *, name):
    s, k = a.shape
    _, n = w.shape
    tm = _tile(s, 1024, MXU_ROWS)
    tn = _tile(n, 2048, LANES)
    tk = _tile(k, 1024, LANES)
    rows = min(MXU_ROWS, tm)

    def body(a_ref, w_ref, r_ref, o_ref):
        kk = pl.program_id(2)

        def step(t, carry):
            rs = pl.ds(pl.multiple_of(t * rows, rows), rows)
            prod = jnp.dot(a_ref[rs, :], w_ref[...], preferred_element_type=F32)

            @pl.when(kk == 0)
            def _():
                o_ref[rs, :] = r_ref[rs, :] + prod

            @pl.when(kk > 0)
            def _():
                o_ref[rs, :] += prod

            return carry

        lax.fori_loop(0, tm // rows, step, 0)

    return pl.pallas_call(
        body,
        name=name,
        grid=(s // tm, n // tn, k // tk),
        in_specs=[
            pl.BlockSpec((tm, tk), lambda i, j, kk: (i, kk)),
            pl.BlockSpec((tk, tn), lambda i, j, kk: (kk, j)),
            pl.BlockSpec((tm, tn), lambda i, j, kk: (i, j)),
        ],
        out_specs=pl.BlockSpec((tm, tn), lambda i, j, kk: (i, j)),
        out_shape=_shape((s, n), F32),
        compiler_params=_compute_params("parallel", "parallel", "arbitrary"),
    )(a, w, res)


def _matmul_tn(a, g, *, n_shards, shard_cols, name, deps=()):
    m, kd = a.shape
    _, n = g.shape
    r, c = (kd, n // n_shards) if shard_cols else (kd // n_shards, n)
    tr = _tile(kd, 2048, LANES)
    tn = _tile(c, 1024, LANES)
    tm = _tile(m, 1024, SUBLANES_BF16)
    nm = m // tm
    cols_per_shard = c // tn

    def body(a_ref, g_ref, o_ref, acc_ref):
        mm = pl.program_id(2)
        prod = lax.dot_general(a_ref[...], g_ref[...], (((0,), (0,)), ((), ())), preferred_element_type=F32)
        if nm == 1:
            o_ref[...] = prod.astype(BF16)
            return

        @pl.when(mm == 0)
        def _():
            acc_ref[...] = prod

        @pl.when((mm > 0) & (mm < nm - 1))
        def _():
            acc_ref[...] += prod

        @pl.when(mm == nm - 1)
        def _():
            o_ref[...] = (acc_ref[...] + prod).astype(BF16)

    if shard_cols:
        out_spec = pl.BlockSpec((None, tr, tn), lambda i, j, mm: (j // cols_per_shard, i, j % cols_per_shard))
        out_shape = _shape((n_shards, kd, c), BF16)
    else:
        out_spec = pl.BlockSpec((tr, tn), lambda i, j, mm: (i, j))
        out_shape = _shape((kd, n), BF16)
    out = pl.pallas_call(
        _behind(body, deps),
        name=name,
        grid=(kd // tr, n // tn, nm),
        in_specs=[ANY_SPEC] * len(deps) + [
            pl.BlockSpec((tm, tr), lambda i, j, mm: (mm, i)),
            pl.BlockSpec((tm, tn), lambda i, j, mm: (mm, j)),
        ],
        out_specs=out_spec,
        out_shape=out_shape,
        scratch_shapes=[pltpu.VMEM((tr, tn), F32)],
        compiler_params=_compute_params("parallel", "parallel", "arbitrary"),
    )(*deps, a, g)
    return out.reshape(n_shards, r, c)


def _matmul_nt(g, w, *, u=None, name, deps=()):
    m, n = g.shape
    kd, _ = w.shape
    tm = _tile(m, 2048, MXU_ROWS)
    tj = _tile(kd, 1024, LANES)
    rows = min(MXU_ROWS, tm)

    def body(*refs):
        if u is None:
            g_ref, w_ref, o_ref = refs
        else:
            g_ref, w_ref, u_ref, o_ref = refs

        def step(t, carry):
            rs = pl.ds(pl.multiple_of(t * rows, rows), rows)
            prod = lax.dot_general(g_ref[rs, :], w_ref[...], (((1,), (1,)), ((), ())), preferred_element_type=F32)
            if u is None:
                o_ref[rs, :] = prod
            else:
                o_ref[rs, :] = (prod * (2.0 * u_ref[rs, :].astype(F32))).astype(BF16)
            return carry

        lax.fori_loop(0, tm // rows, step, 0)

    tile_out = pl.BlockSpec((tm, tj), lambda i, j: (i, j))
    in_specs = [pl.BlockSpec((tm, n), lambda i, j: (i, 0)), pl.BlockSpec((tj, n), lambda i, j: (j, 0))]
    args = [g, w]
    if u is not None:
        in_specs.append(tile_out)
        args.append(u)
    return pl.pallas_call(
        _behind(body, deps),
        name=name,
        grid=(m // tm, kd // tj),
        in_specs=[ANY_SPEC] * len(deps) + in_specs,
        out_specs=tile_out,
        out_shape=_shape((m, kd), F32 if u is None else BF16),
        compiler_params=_compute_params("parallel", "arbitrary"),
    )(*deps, *args)


NORM_BWD_ROWS = 64


def _matmul_nt_norm_bwd(g, w, x, gain, dres, *, name, deps=()):
    s, n = g.shape
    nsh, d, c = w.shape
    tm = _tile(s, 512, SUBLANES_BF16)
    tk = _tile(c, 1024, LANES)
    per_shard = c // tk
    nk = n // tk
    rows = min(NORM_BWD_ROWS, tm)

    def body(g_ref, w_ref, x_ref, gain_ref, dres_ref, dx_ref, dxb_ref, dg_ref, acc_ref):
        i = pl.program_id(0)
        kk = pl.program_id(1)

        @pl.when(kk == 0)
        def _():
            acc_ref[...] = jnp.zeros_like(acc_ref)

        @pl.when((kk == 0) & (i == 0))
        def _():
            dg_ref[...] = jnp.zeros_like(dg_ref)

        acc_ref[...] += lax.dot_general(
            g_ref[...], w_ref[...], (((1,), (1,)), ((), ())), preferred_element_type=F32
        )

        @pl.when(kk == nk - 1)
        def _():
            def step(t, carry):
                rs = pl.ds(pl.multiple_of(t * rows, rows), rows)
                dx, dgain = _rmsnorm_bwd(acc_ref[rs, :], x_ref[rs, :], gain_ref[...], dres_ref[rs, :])
                dx_ref[rs, :] = dx
                dxb_ref[rs, :] = dx.astype(BF16)
                dg_ref[...] += dgain
                return carry

            lax.fori_loop(0, tm // rows, step, 0)

    row_tile = pl.BlockSpec((tm, d), lambda i, kk: (i, 0))
    vec = pl.BlockSpec((1, d), lambda i, kk: (0, 0))
    return pl.pallas_call(
        _behind(body, deps),
        name=name,
        grid=(s // tm, nk),
        in_specs=[ANY_SPEC] * len(deps) + [
            pl.BlockSpec((tm, tk), lambda i, kk: (i, kk)),
            pl.BlockSpec((None, d, tk), lambda i, kk: (kk // per_shard, 0, kk % per_shard)),
            row_tile,
            vec,
            row_tile,
        ],
        out_specs=[row_tile, row_tile, vec],
        out_shape=[_shape((s, d), F32), _shape((s, d), BF16), _shape((1, d), F32)],
        scratch_shapes=[pltpu.VMEM((tm, d), F32)],
        compiler_params=_compute_params("arbitrary", "arbitrary"),
    )(*deps, g, w, x, gain, dres)


def _loss_and_grad(x, gain, target, *, name):
    s, d = x.shape
    tm = _tile(s, 256, SUBLANES_BF16)

    def body(x_ref, gain_ref, t_ref, loss_ref, dx_ref, dxb_ref, dg_ref):
        @pl.when(pl.program_id(0) == 0)
        def _():
            loss_ref[...] = jnp.zeros_like(loss_ref)
            dg_ref[...] = jnp.zeros_like(dg_ref)

        xf = x_ref[...]
        r = lax.rsqrt(jnp.mean(xf * xf, axis=-1, keepdims=True) + EPS)
        err = (xf * r) * gain_ref[...] - t_ref[...]
        loss_ref[...] += 0.5 * jnp.sum(jnp.mean(err * err, axis=-1, keepdims=True))
        dx, dgain = _rmsnorm_bwd(err * (1.0 / d), xf, gain_ref[...], None)
        dx_ref[...] = dx
        dxb_ref[...] = dx.astype(BF16)
        dg_ref[...] += dgain

    row_tile = pl.BlockSpec((tm, d), lambda i: (i, 0))
    vec = pl.BlockSpec((1, d), lambda i: (0, 0))
    return pl.pallas_call(
        body,
        name=name,
        grid=(s // tm,),
        in_specs=[row_tile, vec, row_tile],
        out_specs=[pl.BlockSpec((1, LANES), lambda i: (0, 0)), row_tile, row_tile, vec],
        out_shape=[_shape((1, LANES), F32), _shape((s, d), F32), _shape((s, d), BF16), _shape((1, d), F32)],
        compiler_params=_compute_params("arbitrary"),
    )(x, gain, target)


def _trailing_sums(v_ext, window):
    acc, span = v_ext, 1
    while span < window:
        acc = acc + pltpu.roll(acc, span, 0)
        span *= 2
    return acc


def _leading_sums(q_ext, window):
    n = q_ext.shape[0]
    acc, span = q_ext, 1
    while span < window:
        acc = acc + pltpu.roll(acc, n - span, 0)
        span *= 2
    return acc


def _inverse_counts(first_token, rows, window):
    t = first_token + lax.broadcasted_iota(jnp.int32, (rows, 1), 0)
    return 1.0 / jnp.minimum(t + 1, window).astype(F32)


def _mixer_fwd(proj, conv_w, pool_w, pool_scale, *, name):
    s, _ = proj.shape
    dc = conv_w.shape[1]
    n_groups, cg, _ = pool_w.shape
    dp = n_groups * cg
    assert dc == dp and all(w & (w - 1) == 0 and w <= HALO for w in POOL_WINDOWS)
    ts = _tile(s, 256, HALO)
    halo_blocks = ts // HALO

    def body(b_ref, c_ref, xt_ref, v_ref, ch_ref, xth_ref, vh_ref, cw_ref, pw_ref, ps_ref, y_ref):
        i = pl.program_id(0)
        has_past = i > 0
        u_ext = jnp.concatenate(
            [
                jnp.where(has_past, ch_ref[...].astype(F32) * xth_ref[...].astype(F32), 0.0),
                c_ref[...].astype(F32) * xt_ref[...].astype(F32),
            ],
            axis=0,
        )
        conv = (
            cw_ref[2:3, :] * u_ext[HALO:]
            + cw_ref[1:2, :] * pltpu.roll(u_ext, 1, 0)[HALO:]
            + cw_ref[0:1, :] * pltpu.roll(u_ext, 2, 0)[HALO:]
        )
        y_ref[:, 0:dc] = (b_ref[...].astype(F32) * conv).astype(BF16)
        for gi, window in enumerate(POOL_WINDOWS):
            cols = slice(gi * cg, (gi + 1) * cg)
            v_ext = jnp.concatenate(
                [jnp.where(has_past, vh_ref[:, cols].astype(F32), 0.0), v_ref[:, cols].astype(F32)], axis=0
            )
            mean = _trailing_sums(v_ext, window)[HALO:] * _inverse_counts(i * ts, ts, window)
            diff = (mean - v_ext[HALO:]).astype(BF16)
            z = jnp.dot(diff, pw_ref[gi], preferred_element_type=F32)
            y_ref[:, dc + gi * cg : dc + (gi + 1) * cg] = (z * ps_ref[:, cols]).astype(BF16)

    def col(jc):
        return pl.BlockSpec((ts, dc), lambda i: (i, jc))

    def past(jc):
        return pl.BlockSpec((HALO, dc), lambda i: (jnp.maximum(i * halo_blocks - 1, 0), jc))

    return pl.pallas_call(
        body,
        name=name,
        grid=(s // ts,),
        in_specs=[
            col(0), col(1), col(2), col(3), past(1), past(2), past(3),
            pl.BlockSpec((8, dc), lambda i: (0, 0)),
            pl.BlockSpec((n_groups, cg, cg), lambda i: (0, 0, 0)),
            pl.BlockSpec((1, dp), lambda i: (0, 0)),
        ],
        out_specs=pl.BlockSpec((ts, dc + dp), lambda i: (i, 0)),
        out_shape=_shape((s, dc + dp), BF16),
        compiler_params=_compute_params("parallel"),
    )(proj, proj, proj, proj, proj, proj, proj, conv_w, pool_w, pool_scale)


def _mixer_bwd(dy, proj, conv_w, pool_w, pool_scale, *, name):
    s, e = proj.shape
    dc = conv_w.shape[1]
    n_groups, cg, _ = pool_w.shape
    dp = n_groups * cg
    ts = _tile(s, 256, HALO)
    halo_blocks = ts // HALO
    n_tiles = s // ts
    n_halo_blocks = s // HALO
    n_ext = ts + HALO

    def body(dyc_ref, dyp_ref, dycn_ref, dypn_ref, b_ref, c_ref, xt_ref, v_ref, bn_ref, ch_ref, xth_ref, vh_ref,
             cw_ref, pw_ref, ps_ref, dproj_ref, dcw_ref, dpw_ref, dps_ref):
        i = pl.program_id(0)
        has_past = i > 0
        has_next = i < n_tiles - 1

        @pl.when(i == 0)
        def _():
            dcw_ref[...] = jnp.zeros_like(dcw_ref)
            dpw_ref[...] = jnp.zeros_like(dpw_ref)
            dps_ref[...] = jnp.zeros_like(dps_ref)

        c_now, xt_now, b_now = c_ref[...].astype(F32), xt_ref[...].astype(F32), b_ref[...].astype(F32)
        u_ext = jnp.concatenate(
            [jnp.where(has_past, ch_ref[...].astype(F32) * xth_ref[...].astype(F32), 0.0), c_now * xt_now], axis=0
        )
        u0 = u_ext[HALO:]
        u1 = pltpu.roll(u_ext, 1, 0)[HALO:]
        u2 = pltpu.roll(u_ext, 2, 0)[HALO:]
        dyc = dyc_ref[...]
        conv = cw_ref[2:3, :] * u0 + cw_ref[1:2, :] * u1 + cw_ref[0:1, :] * u2
        dproj_ref[:, 0:dc] = (dyc * conv).astype(BF16)
        dconv = dyc * b_now
        dconv_ext = jnp.concatenate(
            [dconv, jnp.where(has_next, dycn_ref[...] * bn_ref[...].astype(F32), 0.0)], axis=0
        )
        du = (
            cw_ref[2:3, :] * dconv
            + cw_ref[1:2, :] * pltpu.roll(dconv_ext, n_ext - 1, 0)[:ts]
            + cw_ref[0:1, :] * pltpu.roll(dconv_ext, n_ext - 2, 0)[:ts]
        )
        dproj_ref[:, dc : 2 * dc] = (du * xt_now).astype(BF16)
        dproj_ref[:, 2 * dc : 3 * dc] = (du * c_now).astype(BF16)
        dcw_ref[0:1, :] += jnp.sum(dconv * u2, axis=0, keepdims=True)
        dcw_ref[1:2, :] += jnp.sum(dconv * u1, axis=0, keepdims=True)
        dcw_ref[2:3, :] += jnp.sum(dconv * u0, axis=0, keepdims=True)

        for gi, window in enumerate(POOL_WINDOWS):
            cols = slice(gi * cg, (gi + 1) * cg)
            v_ext = jnp.concatenate(
                [jnp.where(has_past, vh_ref[:, cols].astype(F32), 0.0), v_ref[:, cols].astype(F32)], axis=0
            )
            mean = _trailing_sums(v_ext, window)[HALO:] * _inverse_counts(i * ts, ts, window)
            diff = (mean - v_ext[HALO:]).astype(BF16)
            z = jnp.dot(diff, pw_ref[gi], preferred_element_type=F32)
            dyp = dyp_ref[:, cols]
            dps_ref[:, cols] += jnp.sum(dyp * z, axis=0, keepdims=True)
            scale = ps_ref[:, cols]
            dz_ext = jnp.concatenate([dyp * scale, jnp.where(has_next, dypn_ref[:, cols] * scale, 0.0)], axis=0)
            dz_ext = dz_ext.astype(BF16)
            dpw_ref[gi] += lax.dot_general(
                diff, dz_ext[:ts], (((0,), (0,)), ((), ())), preferred_element_type=F32
            )
            ddiff_ext = lax.dot_general(
                dz_ext, pw_ref[gi], (((1,), (1,)), ((), ())), preferred_element_type=F32
            )
            q_ext = ddiff_ext * _inverse_counts(i * ts, n_ext, window)
            dv = _leading_sums(q_ext, window)[:ts] - ddiff_ext[:ts]
            dproj_ref[:, 3 * dc + gi * cg : 3 * dc + (gi + 1) * cg] = dv.astype(BF16)

    def col(jc):
        return pl.BlockSpec((ts, dc), lambda i: (i, jc))

    def past(jc):
        return pl.BlockSpec((HALO, dc), lambda i: (jnp.maximum(i * halo_blocks - 1, 0), jc))

    def following(jc):
        return pl.BlockSpec((HALO, dc), lambda i: (jnp.minimum((i + 1) * halo_blocks, n_halo_blocks - 1), jc))

    return pl.pallas_call(
        body,
        name=name,
        grid=(n_tiles,),
        in_specs=[
            col(0), col(1), following(0), following(1),
            col(0), col(1), col(2), col(3), following(0), past(1), past(2), past(3),
            pl.BlockSpec((8, dc), lambda i: (0, 0)),
            pl.BlockSpec((n_groups, cg, cg), lambda i: (0, 0, 0)),
            pl.BlockSpec((1, dp), lambda i: (0, 0)),
        ],
        out_specs=[
            pl.BlockSpec((ts, e), lambda i: (i, 0)),
            pl.BlockSpec((8, dc), lambda i: (0, 0)),
            pl.BlockSpec((n_groups, cg, cg), lambda i: (0, 0, 0)),
            pl.BlockSpec((1, dp), lambda i: (0, 0)),
        ],
        out_shape=[_shape((s, e), BF16), _shape((8, dc), F32), _shape((n_groups, cg, cg), F32), _shape((1, dp), F32)],
        compiler_params=_compute_params("arbitrary"),
    )(dy, dy, dy, dy, proj, proj, proj, proj, proj, proj, proj, proj, conv_w, pool_w, pool_scale)


def _cast_bf16(a, *, name):
    rows, cols = a.shape
    tr = _tile(rows, 512, SUBLANES_BF16)

    def body(a_ref, o_ref):
        o_ref[...] = a_ref[...].astype(BF16)

    spec = pl.BlockSpec((tr, cols), lambda i: (i, 0))
    return pl.pallas_call(
        body, name=name, grid=(rows // tr,), in_specs=[spec], out_specs=spec, out_shape=_shape((rows, cols), BF16),
        compiler_params=_compute_params("parallel"),
    )(a)


def _core_index():
    return lax.axis_index("c").astype(jnp.int32).reshape((1,))


def _add_sibling_half(grads, received, *, name, deps=()):
    nsh, r, c = grads.shape
    hr = r // 2
    tr = _tile(hr, 512, SUBLANES_BF16)
    tiles = hr // tr

    def body(core_ref, *refs):
        g_ref, r_ref, o_ref = refs[len(deps):]
        o_ref[...] = (g_ref[...].astype(F32) + r_ref[...].astype(F32)).astype(BF16)

    half = pl.BlockSpec((None, tr, c), lambda sh, t, core: (sh, t, 0))
    return pl.pallas_call(
        body,
        name=name,
        grid_spec=pltpu.PrefetchScalarGridSpec(
            num_scalar_prefetch=1,
            grid=(nsh, tiles),
            in_specs=[ANY_SPEC] * len(deps)
            + [pl.BlockSpec((None, tr, c), lambda sh, t, core: (sh, core[0] * tiles + t, 0)), half],
            out_specs=half,
        ),
        out_shape=_shape((nsh, hr, c), BF16),
        compiler_params=_compute_params("parallel", "parallel"),
    )(_core_index(), *deps, grads, received)


def _sum_chip_partials(partials, *, name, deps=()):
    nsh, hr, c = partials.shape
    tr = _tile(hr, 256, SUBLANES_BF16)

    def body(p_ref, o_ref):
        total = p_ref[0].astype(F32)
        for sh in range(1, nsh):
            total = total + p_ref[sh].astype(F32)
        o_ref[...] = total

    return pl.pallas_call(
        _behind(body, deps),
        name=name,
        grid=(hr // tr,),
        in_specs=[ANY_SPEC] * len(deps) + [pl.BlockSpec((nsh, tr, c), lambda t: (0, t, 0))],
        out_specs=pl.BlockSpec((tr, c), lambda t: (t, 0)),
        out_shape=_shape((hr, c), F32),
        compiler_params=_compute_params("parallel"),
    )(*deps, partials)


def _adamw(grad, w, m, v, layer, carried, *, name, deps=()):
    n_layers, r, c = w.shape
    tr = _tile(r, 256, 8)
    bias1 = 1.0 - ADAM_B1 ** ADAM_STEP
    bias2 = 1.0 - ADAM_B2 ** ADAM_STEP

    def body(g_ref, w_ref, m_ref, v_ref, *rest):
        go_ref, d_ref, mo_ref, vo_ref, done_ref = rest[-5:]
        done_ref[...] = jnp.zeros_like(done_ref)
        g = g_ref[...]
        m_new = ADAM_B1 * m_ref[...] + (1.0 - ADAM_B1) * g
        v_new = ADAM_B2 * v_ref[...] + (1.0 - ADAM_B2) * (g * g)
        go_ref[...] = g
        mo_ref[...] = m_new
        vo_ref[...] = v_new
        d_ref[...] = -ADAM_LR * ((m_new / bias1) / (jnp.sqrt(v_new / bias2) + ADAM_EPS) + ADAM_WD * w_ref[...])

    layer_tile = pl.BlockSpec((None, tr, c), lambda t: (layer, t, 0))
    in_specs = [pl.BlockSpec((tr, c), lambda t: (t, 0)), layer_tile, layer_tile, layer_tile]
    args = [grad, w, m, v]
    aliases = {}
    if carried is not None:
        in_specs += [HBM_SPEC] * 4
        aliases = {4 + n: n for n in range(4)}
        args += list(carried)
    *outs, done = pl.pallas_call(
        _behind(body, deps),
        name=name,
        grid=(r // tr,),
        in_specs=[ANY_SPEC] * len(deps) + in_specs,
        out_specs=[layer_tile] * 4 + [pl.BlockSpec((8, LANES), lambda t: (0, 0))],
        out_shape=[_shape((n_layers, r, c), F32)] * 4 + [_shape((8, LANES), F32)],
        input_output_aliases={len(deps) + i: o for i, o in aliases.items()},
        compiler_params=_compute_params("arbitrary"),
    )(*deps, *args)
    return outs, done


def _place():
    x, y, c = (lax.axis_index(a) for a in MESH_AXES)
    other_chips = [(1 - x, y), (x, 1 - y), (1 - x, 1 - y)]
    return x, y, c, other_chips


def _chip_index(x, y):
    return 2 * x + y


def _gather_small(conv_rows, pool_rows, *, name):
    def body(cv_ref, pw_ref, cv_out, pw_out, send_sems, recv_sems):
        x, y, c, other_chips = _place()
        mine = _chip_index(x, y)
        cv_out[mine] = cv_ref[...]
        pw_out[mine] = pw_ref[...]
        sends = []
        for a, (src, dst) in enumerate(((cv_ref, cv_out), (pw_ref, pw_out))):
            for j, (px, py) in enumerate(other_chips):
                cp = pltpu.make_async_remote_copy(
                    src_ref=src, dst_ref=dst.at[mine], send_sem=send_sems.at[a, j], recv_sem=recv_sems.at[a, j],
                    device_id=(px, py, c), device_id_type=MESH_ID,
                )
                cp.start()
                sends.append(cp)
        for a, (src, dst) in enumerate(((cv_ref, cv_out), (pw_ref, pw_out))):
            for j, (px, py) in enumerate(other_chips):
                pltpu.make_async_remote_copy(
                    src_ref=src, dst_ref=dst.at[_chip_index(px, py)], send_sem=send_sems.at[a, j],
                    recv_sem=recv_sems.at[a, j], device_id=(px, py, c), device_id_type=MESH_ID,
                ).wait_recv()
        for cp in sends:
            cp.wait_send()

    return pl.pallas_call(
        body,
        name=name,
        in_specs=[VMEM_SPEC, VMEM_SPEC],
        out_specs=[VMEM_SPEC, VMEM_SPEC],
        out_shape=[_shape((N_CHIPS,) + conv_rows.shape, F32), _shape((N_CHIPS,) + pool_rows.shape, F32)],
        scratch_shapes=[pltpu.SemaphoreType.DMA((2, 3)), pltpu.SemaphoreType.DMA((2, 3))],
    )(conv_rows, pool_rows)


def _allreduce_small(vec, *, name):
    rows, n = vec.shape
    n_dev = 8

    def body(v_ref, o_ref, slots, send_sems, recv_sems):
        x, y, c, _ = _place()
        me = 4 * x + 2 * y + c
        slots[me] = v_ref[...]
        sends = []
        for mask in range(1, n_dev):
            fx, fy, fc = (mask >> 2) & 1, (mask >> 1) & 1, mask & 1
            peer = (x ^ fx, y ^ fy, c ^ fc)
            cp = pltpu.make_async_remote_copy(
                src_ref=v_ref, dst_ref=slots.at[me], send_sem=send_sems.at[mask - 1], recv_sem=recv_sems.at[mask - 1],
                device_id=peer, device_id_type=MESH_ID,
            )
            cp.start()
            sends.append(cp)
        for mask in range(1, n_dev):
            fx, fy, fc = (mask >> 2) & 1, (mask >> 1) & 1, mask & 1
            peer = (x ^ fx, y ^ fy, c ^ fc)
            pltpu.make_async_remote_copy(
                src_ref=v_ref, dst_ref=slots.at[4 * peer[0] + 2 * peer[1] + peer[2]], send_sem=send_sems.at[mask - 1],
                recv_sem=recv_sems.at[mask - 1], device_id=peer, device_id_type=MESH_ID,
            ).wait_recv()
        total = slots[0]
        for dev in range(1, n_dev):
            total = total + slots[dev]
        o_ref[...] = total
        for cp in sends:
            cp.wait_send()

    return pl.pallas_call(
        body,
        name=name,
        in_specs=[VMEM_SPEC],
        out_specs=VMEM_SPEC,
        out_shape=_shape((rows, n), F32),
        scratch_shapes=[
            pltpu.VMEM((n_dev, rows, n), F32),
            pltpu.SemaphoreType.DMA((n_dev - 1,)),
            pltpu.SemaphoreType.DMA((n_dev - 1,)),
        ],
    )(vec)


def _handshake(peers):
    barrier = pltpu.get_barrier_semaphore()
    for peer in peers:
        pl.semaphore_signal(barrier, inc=1, device_id=peer, device_id_type=MESH_ID)
    pl.semaphore_wait(barrier, len(peers))


def _sequencer_call(body, out_type, scratch_types, collective_id, name):
    return pl.kernel(
        body,
        name=name,
        out_type=out_type,
        mesh=plsc.ScalarSubcoreMesh(axis_name="sequencer", num_cores=1),
        scratch_types=scratch_types,
        compiler_params=pltpu.CompilerParams(collective_id=collective_id),
    )


GATHER_COLLECTIVE_ID = 1
EXCHANGE_COLLECTIVE_ID = 2
SCATTER_COLLECTIVE_ID = 3
SHARE_COLLECTIVE_ID = 4


def _gather_weights(stacked, layer, *, name):
    n = len(stacked)

    def body(*refs):
        ins, outs = refs[:n], refs[n : 2 * n]
        local_sems, send_sems, recv_sems, pass_send_sems, pass_recv_sems = refs[2 * n :]
        x, y, c, other_chips = _place()
        mine = _chip_index(x, y)
        sibling = (x, y, 1 - c)
        _handshake([sibling] + [(px, py, c) for px, py in other_chips])
        pending = []

        def send_my_half(a):
            hr = ins[a].shape[1] // 2
            my_half = pl.ds(c * hr, hr)
            cp = pltpu.make_async_copy(ins[a].at[layer], outs[a].at[mine], local_sems.at[a])
            cp.start()
            pending.append(cp.wait)
            for j, (px, py) in enumerate(other_chips):
                cp = pltpu.make_async_remote_copy(
                    src_ref=ins[a].at[layer, my_half], dst_ref=outs[a].at[mine, my_half], send_sem=send_sems.at[a, j],
                    recv_sem=recv_sems.at[a, j], device_id=(px, py, c), device_id_type=MESH_ID,
                )
                cp.start()
                pending.append(cp.wait_send)

        for a in range(min(2, n)):
            send_my_half(a)
        for a in range(n):
            hr = ins[a].shape[1] // 2
            my_half = pl.ds(c * hr, hr)
            for j, (px, py) in enumerate(other_chips):
                landed = outs[a].at[_chip_index(px, py), my_half]
                pltpu.make_async_remote_copy(
                    src_ref=landed, dst_ref=landed, send_sem=send_sems.at[a, j], recv_sem=recv_sems.at[a, j],
                    device_id=(px, py, c), device_id_type=MESH_ID,
                ).wait_recv()
                cp = pltpu.make_async_remote_copy(
                    src_ref=landed, dst_ref=landed, send_sem=pass_send_sems.at[a, j], recv_sem=pass_recv_sems.at[a, j],
                    device_id=sibling, device_id_type=MESH_ID,
                )
                cp.start()
                pending.append(cp.wait_send)
            if a + 2 < n:
                send_my_half(a + 2)
        for a in range(n):
            hr = ins[a].shape[1] // 2
            sibling_half = pl.ds((1 - c) * hr, hr)
            for j, (px, py) in enumerate(other_chips):
                passed = outs[a].at[_chip_index(px, py), sibling_half]
                pltpu.make_async_remote_copy(
                    src_ref=passed, dst_ref=passed, send_sem=pass_send_sems.at[a, j], recv_sem=pass_recv_sems.at[a, j],
                    device_id=sibling, device_id_type=MESH_ID,
                ).wait_recv()
        for wait in pending:
            wait()

    return _sequencer_call(
        body,
        [_shape((N_CHIPS,) + a.shape[1:], BF16) for a in stacked],
        [pltpu.SemaphoreType.DMA((n,))] + [pltpu.SemaphoreType.DMA((n, 3))] * 4,
        GATHER_COLLECTIVE_ID,
        name,
    )(*stacked)


def _exchange_halves(grads, *, name):
    n = len(grads)

    def body(*refs):
        ins, outs = refs[:n], refs[n : 2 * n]
        send_sems, recv_sems = refs[2 * n :]
        x, y, c, _ = _place()
        sibling = (x, y, 1 - c)
        _handshake([sibling])
        copies = []
        for a in range(n):
            hr = ins[a].shape[1] // 2
            cp = pltpu.make_async_remote_copy(
                src_ref=ins[a].at[:, pl.ds((1 - c) * hr, hr), :], dst_ref=outs[a], send_sem=send_sems.at[a],
                recv_sem=recv_sems.at[a], device_id=sibling, device_id_type=MESH_ID,
            )
            cp.start()
            copies.append(cp)
        for cp in copies:
            cp.wait()

    return _sequencer_call(
        body,
        [_shape((g.shape[0], g.shape[1] // 2, g.shape[2]), BF16) for g in grads],
        [pltpu.SemaphoreType.DMA((n,)), pltpu.SemaphoreType.DMA((n,))],
        EXCHANGE_COLLECTIVE_ID,
        name,
    )(*grads)


def _scatter_partials(partials, *, name):
    n = len(partials)

    def body(*refs):
        ins, outs = refs[:n], refs[n : 2 * n]
        local_sems, send_sems, recv_sems = refs[2 * n :]
        x, y, c, other_chips = _place()
        mine = _chip_index(x, y)
        _handshake([(px, py, c) for px, py in other_chips])
        pending = []
        for a in range(n):
            cp = pltpu.make_async_copy(ins[a].at[mine], outs[a].at[mine], local_sems.at[a])
            cp.start()
            pending.append(cp.wait)
            for j, (px, py) in enumerate(other_chips):
                cp = pltpu.make_async_remote_copy(
                    src_ref=ins[a].at[_chip_index(px, py)], dst_ref=outs[a].at[mine], send_sem=send_sems.at[a, j],
                    recv_sem=recv_sems.at[a, j], device_id=(px, py, c), device_id_type=MESH_ID,
                )
                cp.start()
                pending.append(cp.wait_send)
        for a in range(n):
            for j, (px, py) in enumerate(other_chips):
                landed = outs[a].at[_chip_index(px, py)]
                pltpu.make_async_remote_copy(
                    src_ref=landed, dst_ref=landed, send_sem=send_sems.at[a, j], recv_sem=recv_sems.at[a, j],
                    device_id=(px, py, c), device_id_type=MESH_ID,
                ).wait_recv()
        for wait in pending:
            wait()

    return _sequencer_call(
        body,
        [_shape(p.shape, BF16) for p in partials],
        [pltpu.SemaphoreType.DMA((n,)), pltpu.SemaphoreType.DMA((n, 3)), pltpu.SemaphoreType.DMA((n, 3))],
        SCATTER_COLLECTIVE_ID,
        name,
    )(*partials)


def _share_halves(halves, *, name):
    n = len(halves)

    def body(*refs):
        ins, outs = refs[:n], refs[n : 2 * n]
        local_sems, send_sems, recv_sems = refs[2 * n :]
        x, y, c, _ = _place()
        sibling = (x, y, 1 - c)
        _handshake([sibling])
        pending = []
        for a in range(n):
            hr = ins[a].shape[0]
            my_half = outs[a].at[pl.ds(c * hr, hr), :]
            cp = pltpu.make_async_copy(ins[a], my_half, local_sems.at[a])
            cp.start()
            pending.append(cp.wait)
            cp = pltpu.make_async_remote_copy(
                src_ref=ins[a], dst_ref=my_half, send_sem=send_sems.at[a], recv_sem=recv_sems.at[a],
                device_id=sibling, device_id_type=MESH_ID,
            )
            cp.start()
            pending.append(cp.wait_send)
        for a in range(n):
            hr = ins[a].shape[0]
            sibling_half = outs[a].at[pl.ds((1 - c) * hr, hr), :]
            pltpu.make_async_remote_copy(
                src_ref=ins[a], dst_ref=sibling_half, send_sem=send_sems.at[a], recv_sem=recv_sems.at[a],
                device_id=sibling, device_id_type=MESH_ID,
            ).wait_recv()
        for wait in pending:
            wait()

    return _sequencer_call(
        body,
        [_shape((2 * h.shape[0], h.shape[1]), F32) for h in halves],
        [pltpu.SemaphoreType.DMA((n,)), pltpu.SemaphoreType.DMA((n,)), pltpu.SemaphoreType.DMA((n,))],
        SHARE_COLLECTIVE_ID,
        name,
    )(*halves)


class _ReduceToOwner:
    def __init__(self, grads, tag):
        self.grads, self.tag = grads, tag
        self.received = _exchange_halves(grads, name=f"exchange_halves_{tag}")

    def add_sibling(self, after):
        partials = [
            _add_sibling_half(g, r, name=f"add_sibling_{self.tag}_{a}", deps=after)
            for a, (g, r) in enumerate(zip(self.grads, self.received))
        ]
        self.slots = _scatter_partials(partials, name=f"scatter_partials_{self.tag}")
        return partials

    def sum_chips(self, after):
        halves = [
            _sum_chip_partials(s, name=f"sum_partials_{self.tag}_{a}", deps=after) for a, s in enumerate(self.slots)
        ]
        self.shared = _share_halves(halves, name=f"share_halves_{self.tag}")
        return halves

    def totals(self):
        return self.shared


def kernel(x, w_in, conv_w, pool_w, pool_scale, w_out, norm_mix, norm_mlp, w_up, w_down, norm_final, loss_target, m_w_in, m_conv_w, m_pool_w, m_pool_scale, m_w_out, m_norm_mix, m_norm_mlp, m_w_up, m_w_down, m_norm_final, v_w_in, v_conv_w, v_pool_w, v_pool_scale, v_w_out, v_norm_mix, v_norm_mlp, v_w_up, v_w_down, v_norm_final):
    n_layers, d, _ = w_in.shape
    s = x.shape[1]
    dc = conv_w.shape[2] * N_CHIPS
    n_groups, cg_rows, cg = pool_w.shape[1:]
    dp = n_groups * cg
    x0 = x.reshape(s, d)
    target = loss_target.reshape(s, d)

    big = [w_in, w_out, w_up, w_down]
    big_bf16 = [
        _cast_bf16(w.reshape(-1, w.shape[2]), name=f"cast_{t}").reshape(w.shape)
        for t, w in zip(("w_in", "w_out", "w_up", "w_down"), big)
    ]

    conv_rows = jnp.pad(conv_w, ((0, 0), (0, 8 - CONV_TAPS), (0, 0))).reshape(n_layers * 8, -1)
    pool_rows = pool_w.reshape(n_layers * n_groups * cg_rows, cg)
    conv_all, pool_all = _gather_small(conv_rows, pool_rows, name="gather_small")
    conv_full = conv_all.reshape(N_CHIPS, n_layers, 8, -1).transpose(1, 2, 0, 3).reshape(n_layers, 8, dc)
    pool_full = (
        pool_all.reshape(N_CHIPS, n_layers, n_groups, cg_rows, cg).transpose(1, 2, 0, 3, 4)
        .reshape(n_layers, n_groups, cg, cg).astype(BF16)
    )

    saved = []
    xl = x0
    for l in range(n_layers):
        if l == 0:
            win_g, wout_g, wup_g, wdown_g = [
                _gather_weights([w], l, name=f"gather_weights_l{l}_{t}")[0] for t, w in enumerate(big_bf16)
            ]
        else:
            win_g, wout_g, wup_g, wdown_g = _gather_weights(big_bf16, l, name=f"gather_weights_l{l}")
        gain_mix = norm_mix[l].reshape(1, d)
        gain_mlp = norm_mlp[l].reshape(1, d)
        scale = pool_scale[l].reshape(1, dp)
        h1 = _rmsnorm(xl, gain_mix, name=f"norm_mix_l{l}")
        proj = _matmul_cols(h1, win_g, relu2=False, name=f"in_proj_l{l}")
        y = _mixer_fwd(proj, conv_full[l], pool_full[l], scale, name=f"mixer_fwd_l{l}")
        x_mid = _matmul_residual(y, wout_g.reshape(-1, d), xl, name=f"out_proj_l{l}")
        h2 = _rmsnorm(x_mid, gain_mlp, name=f"norm_mlp_l{l}")
        u, u2 = _matmul_cols(h2, wup_g, relu2=True, name=f"up_proj_l{l}")
        x_next = _matmul_residual(u2, wdown_g.reshape(-1, d), x_mid, name=f"down_proj_l{l}")
        saved.append((xl, h1, proj, y, x_mid, h2, u, u2, win_g, wout_g, wup_g, wdown_g, gain_mix, gain_mlp, scale))
        xl = x_next

    loss_part, dx, dx_bf16, d_norm_final = _loss_and_grad(xl, norm_final.reshape(1, d), target, name="loss_head")
    loss = lax.psum(loss_part[0, 0], MESH_AXES)

    small_grads = [None] * n_layers
    carried = [None] * 5
    pool_params = tuple(p.reshape(n_layers, n_groups * cg_rows, cg) for p in (pool_w, m_pool_w, v_pool_w))
    params = [(w_in, m_w_in, v_w_in), (w_out, m_w_out, v_w_out), (w_up, m_w_up, v_w_up), (w_down, m_w_down, v_w_down),
              pool_params]
    MLP_PARAMS, MIX_PARAMS = (2, 3), (0, 1, 4)

    def update(reduce, which, layer, after):
        dones = []
        for a, total in zip(which, reduce.totals()):
            w, m, v = params[a]
            carried[a], done = _adamw(total, w, m, v, layer, carried[a], name=f"adamw_{a}_l{layer}", deps=after)
            dones.append(done)
        return dones

    mlp_above = mix_above = None
    for l in reversed(range(n_layers)):
        xl, h1, proj, y, x_mid, h2, u, u2, win_g, wout_g, wup_g, wdown_g, gain_mix, gain_mlp, scale = saved[l]
        above = mlp_above is not None
        deps = mix_above.add_sibling([dx_bf16]) if above else []
        g_down = _matmul_tn(u2, dx_bf16, n_shards=N_CHIPS, shard_cols=False, name=f"grad_w_down_l{l}", deps=deps)
        deps = mlp_above.sum_chips([g_down]) if above else [g_down]
        da = _matmul_nt(dx_bf16, wdown_g.reshape(-1, d), u=u, name=f"grad_act_l{l}", deps=deps)
        g_up = _matmul_tn(h2, da, n_shards=N_CHIPS, shard_cols=True, name=f"grad_w_up_l{l}")
        mlp = _ReduceToOwner([g_up, g_down], f"mlp_l{l}")
        deps = mix_above.sum_chips([g_up]) if above else [g_up]
        dx_mid, dx_mid_bf16, d_gain_mlp = _matmul_nt_norm_bwd(
            da, wup_g, x_mid, gain_mlp, dx, name=f"grad_mid_l{l}", deps=deps
        )
        g_out = _matmul_tn(
            y, dx_mid_bf16, n_shards=N_CHIPS, shard_cols=False, name=f"grad_w_out_l{l}", deps=mlp.add_sibling([dx_mid_bf16])
        )
        deps = update(mlp_above, MLP_PARAMS, l + 1, [g_out]) if above else [g_out]
        dy = _matmul_nt(dx_mid_bf16, wout_g.reshape(-1, d), name=f"grad_mixed_l{l}", deps=deps)
        dproj, d_conv, d_pool, d_scale = _mixer_bwd(dy, proj, conv_full[l], pool_full[l], scale, name=f"mixer_bwd_l{l}")
        deps = update(mix_above, MIX_PARAMS, l + 1, [dproj]) if above else []
        g_in = _matmul_tn(h1, dproj, n_shards=N_CHIPS, shard_cols=True, name=f"grad_w_in_l{l}", deps=deps)
        g_pool = (
            d_pool.reshape(n_groups, N_CHIPS, cg_rows, cg).transpose(1, 0, 2, 3)
            .reshape(N_CHIPS, n_groups * cg_rows, cg).astype(BF16)
        )
        mix = _ReduceToOwner([g_in, g_out, g_pool], f"mix_l{l}")
        dx, dx_bf16, d_gain_mix = _matmul_nt_norm_bwd(
            dproj, win_g, xl, gain_mix, dx_mid, name=f"grad_x_l{l}", deps=[g_in]
        )
        small_grads[l] = jnp.concatenate(
            [d_conv[:CONV_TAPS].reshape(-1), d_scale.reshape(-1), d_gain_mix.reshape(-1), d_gain_mlp.reshape(-1)]
        )
        mlp_above, mix_above = mlp, mix
    deps = mix_above.add_sibling([dx_bf16])
    deps = mlp_above.sum_chips(deps)
    deps = mix_above.sum_chips(deps)
    deps = update(mlp_above, MLP_PARAMS, 0, deps)
    update(mix_above, MIX_PARAMS, 0, deps)

    vec = jnp.concatenate(small_grads + [d_norm_final.reshape(-1)])
    vec = _allreduce_small(vec.reshape(8, -1), name="allreduce_small").reshape(-1)
    per_layer = vec[: n_layers * (CONV_TAPS * dc + dp + 2 * d)].reshape(n_layers, -1)
    chip = _chip_index(lax.axis_index("x"), lax.axis_index("y"))
    dcs = dc // N_CHIPS
    g_conv = lax.dynamic_slice_in_dim(per_layer[:, : CONV_TAPS * dc].reshape(n_layers, CONV_TAPS, dc), chip * dcs, dcs, axis=2)
    g_scale = per_layer[:, CONV_TAPS * dc : CONV_TAPS * dc + dp]
    g_mix = per_layer[:, CONV_TAPS * dc + dp : CONV_TAPS * dc + dp + d]
    g_mlp = per_layer[:, CONV_TAPS * dc + dp + d :]
    g_final = vec[n_layers * (CONV_TAPS * dc + dp + 2 * d) :]

    def small_adamw(g, w, m, v, tag):
        flat = lambda t: t.reshape(1, -1, t.shape[-1])
        out, _ = _adamw(flat(g)[0], flat(w), flat(m), flat(v), 0, None, name=f"adamw_{tag}")
        return [o.reshape(w.shape) for o in out]

    o_conv = small_adamw(g_conv, conv_w, m_conv_w, v_conv_w, "conv_w")
    o_scale = small_adamw(g_scale, pool_scale, m_pool_scale, v_pool_scale, "pool_scale")
    o_mix = small_adamw(g_mix, norm_mix, m_norm_mix, v_norm_mix, "norm_mix")
    o_mlp = small_adamw(g_mlp, norm_mlp, m_norm_mlp, v_norm_mlp, "norm_mlp")
    o_final = small_adamw(g_final, norm_final, m_norm_final, v_norm_final, "norm_final")
    o_in, o_out, o_up, o_down, o_pool = carried
    o_pool = [o.reshape(pool_w.shape) for o in o_pool]

    ordered = [o_in, o_conv, o_pool, o_scale, o_out, o_mix, o_mlp, o_up, o_down, o_final]
    return (loss, dx.reshape(x.shape), *[o[0] for o in ordered], *[o[1] for o in ordered], *[o[2] for o in ordered],
            *[o[3] for o in ordered])
```

```python
import functools

import jax
import jax.numpy as jnp
from jax import lax
from jax.experimental import pallas as pl
from jax.experimental.pallas import tpu as pltpu
from jax.experimental.pallas import tpu_sc as plsc

F32 = jnp.float32
BF16 = jnp.bfloat16

EPS = 1e-6
POOL_WINDOWS = (2, 4, 8, 16)
CONV_TAPS = 3
HALO = 16

ADAM_LR = 0.001
ADAM_B1 = 0.9
ADAM_B2 = 0.999
ADAM_EPS = 1e-08
ADAM_WD = 0.01
ADAM_STEP = 10

N_CHIPS = 4
MESH_AXES = ("x", "y", "c")
V7X_VMEM_LIMIT_BYTES = 56 * 1024 * 1024
SUBLANES_BF16 = 16
LANES = 128

HBM_SPEC = pl.BlockSpec(memory_space=pltpu.HBM)
VMEM_SPEC = pl.BlockSpec(memory_space=pltpu.VMEM)
MESH_ID = pl.DeviceIdType.MESH


def _tile(dim, target, align):
    if dim <= target:
        return dim
    t = (target // align) * align
    while dim % t:
        t -= align
    assert t > 0, (dim, target, align)
    return t


def _compute_params(*semantics):
    return pltpu.CompilerParams(dimension_semantics=semantics, vmem_limit_bytes=V7X_VMEM_LIMIT_BYTES)


def _shape(shape, dtype):
    return jax.ShapeDtypeStruct(shape, dtype)


ANY_SPEC = pl.BlockSpec(memory_space=pl.ANY)


def _behind(body, deps):
    return lambda *refs: body(*refs[len(deps):])


def _rmsnorm_bwd(dh, x, gain, dres):
    r = lax.rsqrt(jnp.mean(x * x, axis=-1, keepdims=True) + EPS)
    xn = x * r
    dgain = jnp.sum(dh * xn, axis=0, keepdims=True)
    dxn = dh * gain
    dx = r * (dxn - xn * jnp.mean(dxn * xn, axis=-1, keepdims=True))
    if dres is not None:
        dx = dx + dres
    return dx, dgain


MXU_ROWS = 512


def _rmsnorm(x, gain, *, name):
    s, d = x.shape
    tm = _tile(s, 256, SUBLANES_BF16)

    def body(x_ref, g_ref, h_ref):
        xf = x_ref[...]
        r = lax.rsqrt(jnp.mean(xf * xf, axis=-1, keepdims=True) + EPS)
        h_ref[...] = ((xf * r) * g_ref[...]).astype(BF16)

    row_tile = pl.BlockSpec((tm, d), lambda i: (i, 0))
    return pl.pallas_call(
        body, name=name, grid=(s // tm,), in_specs=[row_tile, pl.BlockSpec((1, d), lambda i: (0, 0))],
        out_specs=row_tile, out_shape=_shape((s, d), BF16), compiler_params=_compute_params("parallel"),
    )(x, gain)


def _matmul_cols(h, w, *, relu2, name):
    s, k = h.shape
    nsh, _, c = w.shape
    tm = _tile(s, 2048, MXU_ROWS)
    tn = _tile(c, 1024, LANES)
    rows = min(MXU_ROWS, tm)
    per_shard = c // tn

    def body(h_ref, w_ref, *out_refs):
        def step(t, carry):
            rs = pl.ds(pl.multiple_of(t * rows, rows), rows)
            acc = jnp.dot(h_ref[rs, :], w_ref[...], preferred_element_type=F32)
            if relu2:
                u = jnp.maximum(acc, 0.0)
                out_refs[0][rs, :] = u.astype(BF16)
                out_refs[1][rs, :] = (u * u).astype(BF16)
            else:
                out_refs[0][rs, :] = acc.astype(BF16)
            return carry

        lax.fori_loop(0, tm // rows, step, 0)

    tile_out = pl.BlockSpec((tm, tn), lambda i, j: (i, j))
    n = nsh * c
    outs = [_shape((s, n), BF16)] * (2 if relu2 else 1)
    result = pl.pallas_call(
        body,
        name=name,
        grid=(s // tm, n // tn),
        in_specs=[
            pl.BlockSpec((tm, k), lambda i, j: (i, 0)),
            pl.BlockSpec((None, k, tn), lambda i, j: (j // per_shard, 0, j % per_shard)),
        ],
        out_specs=[tile_out] * len(outs),
        out_shape=outs,
        compiler_params=_compute_params("parallel", "arbitrary"),
    )(h, w)
    return result if relu2 else result[0]


def _matmul_residual(a, w, res, *, name):
    s, k = a.shape
    _, n = w.shape
    tm = _tile(s, 1024, MXU_ROWS)
    tn = _tile(n, 2048, LANES)
    tk = _tile(k, 1024, LANES)
    rows = min(MXU_ROWS, tm)

    def body(a_ref, w_ref, r_ref, o_ref):
        kk = pl.program_id(2)

        def step(t, carry):
            rs = pl.ds(pl.multiple_of(t * rows, rows), rows)
            prod = jnp.dot(a_ref[rs, :], w_ref[...], preferred_element_type=F32)

            @pl.when(kk == 0)
            def _():
                o_ref[rs, :] = r_ref[rs, :] + prod

            @pl.when(kk > 0)
            def _():
                o_ref[rs, :] += prod

            return carry

        lax.fori_loop(0, tm // rows, step, 0)

    return pl.pallas_call(
        body,
        name=name,
        grid=(s // tm, n // tn, k // tk),
        in_specs=[
            pl.BlockSpec((tm, tk), lambda i, j, kk: (i, kk)),
            pl.BlockSpec((tk, tn), lambda i, j, kk: (kk, j)),
            pl.BlockSpec((tm, tn), lambda i, j, kk: (i, j)),
        ],
        out_specs=pl.BlockSpec((tm, tn), lambda i, j, kk: (i, j)),
        out_shape=_shape((s, n), F32),
        compiler_params=_compute_params("parallel", "parallel", "arbitrary"),
    )(a, w, res)


def _matmul_tn(a, g, *, n_shards, shard_cols, name, deps=()):
    m, kd = a.shape
    _, n = g.shape
    r, c = (kd, n // n_shards) if shard_cols else (kd // n_shards, n)
    tr = _tile(kd, 2048, LANES)
    tn = _tile(c, 1024, LANES)
    tm = _tile(m, 1024, SUBLANES_BF16)
    nm = m // tm
    cols_per_shard = c // tn

    def body(a_ref, g_ref, o_ref, acc_ref):
        mm = pl.program_id(2)
        prod = lax.dot_general(a_ref[...], g_ref[...], (((0,), (0,)), ((), ())), preferred_element_type=F32)
        if nm == 1:
            o_ref[...] = prod.astype(BF16)
            return

        @pl.when(mm == 0)
        def _():
            acc_ref[...] = prod

        @pl.when((mm > 0) & (mm < nm - 1))
        def _():
            acc_ref[...] += prod

        @pl.when(mm == nm - 1)
        def _():
            o_ref[...] = (acc_ref[...] + prod).astype(BF16)

    if shard_cols:
        out_spec = pl.BlockSpec((None, tr, tn), lambda i, j, mm: (j // cols_per_shard, i, j % cols_per_shard))
        out_shape = _shape((n_shards, kd, c), BF16)
    else:
        out_spec = pl.BlockSpec((tr, tn), lambda i, j, mm: (i, j))
        out_shape = _shape((kd, n), BF16)
    out = pl.pallas_call(
        _behind(body, deps),
        name=name,
        grid=(kd // tr, n // tn, nm),
        in_specs=[ANY_SPEC] * len(deps) + [
            pl.BlockSpec((tm, tr), lambda i, j, mm: (mm, i)),
            pl.BlockSpec((tm, tn), lambda i, j, mm: (mm, j)),
        ],
        out_specs=out_spec,
        out_shape=out_shape,
        scratch_shapes=[pltpu.VMEM((tr, tn), F32)],
        compiler_params=_compute_params("parallel", "parallel", "arbitrary"),
    )(*deps, a, g)
    return out.reshape(n_shards, r, c)


def _matmul_nt(g, w, *, u=None, name, deps=()):
    m, n = g.shape
    kd, _ = w.shape
    tm = _tile(m, 2048, MXU_ROWS)
    tj = _tile(kd, 1024, LANES)
    rows = min(MXU_ROWS, tm)

    def body(*refs):
        if u is None:
            g_ref, w_ref, o_ref = refs
        else:
            g_ref, w_ref, u_ref, o_ref = refs

        def step(t, carry):
            rs = pl.ds(pl.multiple_of(t * rows, rows), rows)
            prod = lax.dot_general(g_ref[rs, :], w_ref[...], (((1,), (1,)), ((), ())), preferred_element_type=F32)
            if u is None:
                o_ref[rs, :] = prod
            else:
                o_ref[rs, :] = (prod * (2.0 * u_ref[rs, :].astype(F32))).astype(BF16)
            return carry

        lax.fori_loop(0, tm // rows, step, 0)

    tile_out = pl.BlockSpec((tm, tj), lambda i, j: (i, j))
    in_specs = [pl.BlockSpec((tm, n), lambda i, j: (i, 0)), pl.BlockSpec((tj, n), lambda i, j: (j, 0))]
    args = [g, w]
    if u is not None:
        in_specs.append(tile_out)
        args.append(u)
    return pl.pallas_call(
        _behind(body, deps),
        name=name,
        grid=(m // tm, kd // tj),
        in_specs=[ANY_SPEC] * len(deps) + in_specs,
        out_specs=tile_out,
        out_shape=_shape((m, kd), F32 if u is None else BF16),
        compiler_params=_compute_params("parallel", "arbitrary"),
    )(*deps, *args)


NORM_BWD_ROWS = 64


def _matmul_nt_norm_bwd(g, w, x, gain, dres, *, name, deps=()):
    s, n = g.shape
    nsh, d, c = w.shape
    tm = _tile(s, 512, SUBLANES_BF16)
    tk = _tile(c, 1024, LANES)
    per_shard = c // tk
    nk = n // tk
    rows = min(NORM_BWD_ROWS, tm)

    def body(g_ref, w_ref, x_ref, gain_ref, dres_ref, dx_ref, dxb_ref, dg_ref, acc_ref):
        i = pl.program_id(0)
        kk = pl.program_id(1)

        @pl.when(kk == 0)
        def _():
            acc_ref[...] = jnp.zeros_like(acc_ref)

        @pl.when((kk == 0) & (i == 0))
        def _():
            dg_ref[...] = jnp.zeros_like(dg_ref)

        acc_ref[...] += lax.dot_general(
            g_ref[...], w_ref[...], (((1,), (1,)), ((), ())), preferred_element_type=F32
        )

        @pl.when(kk == nk - 1)
        def _():
            def step(t, carry):
                rs = pl.ds(pl.multiple_of(t * rows, rows), rows)
                dx, dgain = _rmsnorm_bwd(acc_ref[rs, :], x_ref[rs, :], gain_ref[...], dres_ref[rs, :])
                dx_ref[rs, :] = dx
                dxb_ref[rs, :] = dx.astype(BF16)
                dg_ref[...] += dgain
                return carry

            lax.fori_loop(0, tm // rows, step, 0)

    row_tile = pl.BlockSpec((tm, d), lambda i, kk: (i, 0))
    vec = pl.BlockSpec((1, d), lambda i, kk: (0, 0))
    return pl.pallas_call(
        _behind(body, deps),
        name=name,
        grid=(s // tm, nk),
        in_specs=[ANY_SPEC] * len(deps) + [
            pl.BlockSpec((tm, tk), lambda i, kk: (i, kk)),
            pl.BlockSpec((None, d, tk), lambda i, kk: (kk // per_shard, 0, kk % per_shard)),
            row_tile,
            vec,
            row_tile,
        ],
        out_specs=[row_tile, row_tile, vec],
        out_shape=[_shape((s, d), F32), _shape((s, d), BF16), _shape((1, d), F32)],
        scratch_shapes=[pltpu.VMEM((tm, d), F32)],
        compiler_params=_compute_params("arbitrary", "arbitrary"),
    )(*deps, g, w, x, gain, dres)


def _loss_and_grad(x, gain, target, *, name):
    s, d = x.shape
    tm = _tile(s, 256, SUBLANES_BF16)

    def body(x_ref, gain_ref, t_ref, loss_ref, dx_ref, dxb_ref, dg_ref):
        @pl.when(pl.program_id(0) == 0)
        def _():
            loss_ref[...] = jnp.zeros_like(loss_ref)
            dg_ref[...] = jnp.zeros_like(dg_ref)

        xf = x_ref[...]
        r = lax.rsqrt(jnp.mean(xf * xf, axis=-1, keepdims=True) + EPS)
        err = (xf * r) * gain_ref[...] - t_ref[...]
        loss_ref[...] += 0.5 * jnp.sum(jnp.mean(err * err, axis=-1, keepdims=True))
        dx, dgain = _rmsnorm_bwd(err * (1.0 / d), xf, gain_ref[...], None)
        dx_ref[...] = dx
        dxb_ref[...] = dx.astype(BF16)
        dg_ref[...] += dgain

    row_tile = pl.BlockSpec((tm, d), lambda i: (i, 0))
    vec = pl.BlockSpec((1, d), lambda i: (0, 0))
    return pl.pallas_call(
        body,
        name=name,
        grid=(s // tm,),
        in_specs=[row_tile, vec, row_tile],
        out_specs=[pl.BlockSpec((1, LANES), lambda i: (0, 0)), row_tile, row_tile, vec],
        out_shape=[_shape((1, LANES), F32), _shape((s, d), F32), _shape((s, d), BF16), _shape((1, d), F32)],
        compiler_params=_compute_params("arbitrary"),
    )(x, gain, target)


def _trailing_sums(v_ext, window):
    acc, span = v_ext, 1
    while span < window:
        acc = acc + pltpu.roll(acc, span, 0)
        span *= 2
    return acc


def _leading_sums(q_ext, window):
    n = q_ext.shape[0]
    acc, span = q_ext, 1
    while span < window:
        acc = acc + pltpu.roll(acc, n - span, 0)
        span *= 2
    return acc


def _inverse_counts(first_token, rows, window):
    t = first_token + lax.broadcasted_iota(jnp.int32, (rows, 1), 0)
    return 1.0 / jnp.minimum(t + 1, window).astype(F32)


def _mixer_fwd(proj, conv_w, pool_w, pool_scale, *, name):
    s, _ = proj.shape
    dc = conv_w.shape[1]
    n_groups, cg, _ = pool_w.shape
    dp = n_groups * cg
    assert dc == dp and all(w & (w - 1) == 0 and w <= HALO for w in POOL_WINDOWS)
    ts = _tile(s, 256, HALO)
    halo_blocks = ts // HALO

    def body(b_ref, c_ref, xt_ref, v_ref, ch_ref, xth_ref, vh_ref, cw_ref, pw_ref, ps_ref, y_ref):
        i = pl.program_id(0)
        has_past = i > 0
        u_ext = jnp.concatenate(
            [
                jnp.where(has_past, ch_ref[...].astype(F32) * xth_ref[...].astype(F32), 0.0),
                c_ref[...].astype(F32) * xt_ref[...].astype(F32),
            ],
            axis=0,
        )
        conv = (
            cw_ref[2:3, :] * u_ext[HALO:]
            + cw_ref[1:2, :] * pltpu.roll(u_ext, 1, 0)[HALO:]
            + cw_ref[0:1, :] * pltpu.roll(u_ext, 2, 0)[HALO:]
        )
        y_ref[:, 0:dc] = (b_ref[...].astype(F32) * conv).astype(BF16)
        for gi, window in enumerate(POOL_WINDOWS):
            cols = slice(gi * cg, (gi + 1) * cg)
            v_ext = jnp.concatenate(
                [jnp.where(has_past, vh_ref[:, cols].astype(F32), 0.0), v_ref[:, cols].astype(F32)], axis=0
            )
            mean = _trailing_sums(v_ext, window)[HALO:] * _inverse_counts(i * ts, ts, window)
            diff = (mean - v_ext[HALO:]).astype(BF16)
            z = jnp.dot(diff, pw_ref[gi], preferred_element_type=F32)
            y_ref[:, dc + gi * cg : dc + (gi + 1) * cg] = (z * ps_ref[:, cols]).astype(BF16)

    def col(jc):
        return pl.BlockSpec((ts, dc), lambda i: (i, jc))

    def past(jc):
        return pl.BlockSpec((HALO, dc), lambda i: (jnp.maximum(i * halo_blocks - 1, 0), jc))

    return pl.pallas_call(
        body,
        name=name,
        grid=(s // ts,),
        in_specs=[
            col(0), col(1), col(2), col(3), past(1), past(2), past(3),
            pl.BlockSpec((8, dc), lambda i: (0, 0)),
            pl.BlockSpec((n_groups, cg, cg), lambda i: (0, 0, 0)),
            pl.BlockSpec((1, dp), lambda i: (0, 0)),
        ],
        out_specs=pl.BlockSpec((ts, dc + dp), lambda i: (i, 0)),
        out_shape=_shape((s, dc + dp), BF16),
        compiler_params=_compute_params("parallel"),
    )(proj, proj, proj, proj, proj, proj, proj, conv_w, pool_w, pool_scale)


def _mixer_bwd(dy, proj, conv_w, pool_w, pool_scale, *, name):
    s, e = proj.shape
    dc = conv_w.shape[1]
    n_groups, cg, _ = pool_w.shape
    dp = n_groups * cg
    ts = _tile(s, 256, HALO)
    halo_blocks = ts // HALO
    n_tiles = s // ts
    n_halo_blocks = s // HALO
    n_ext = ts + HALO

    def body(dyc_ref, dyp_ref, dycn_ref, dypn_ref, b_ref, c_ref, xt_ref, v_ref, bn_ref, ch_ref, xth_ref, vh_ref,
             cw_ref, pw_ref, ps_ref, dproj_ref, dcw_ref, dpw_ref, dps_ref):
        i = pl.program_id(0)
        has_past = i > 0
        has_next = i < n_tiles - 1

        @pl.when(i == 0)
        def _():
            dcw_ref[...] = jnp.zeros_like(dcw_ref)
            dpw_ref[...] = jnp.zeros_like(dpw_ref)
            dps_ref[...] = jnp.zeros_like(dps_ref)

        c_now, xt_now, b_now = c_ref[...].astype(F32), xt_ref[...].astype(F32), b_ref[...].astype(F32)
        u_ext = jnp.concatenate(
            [jnp.where(has_past, ch_ref[...].astype(F32) * xth_ref[...].astype(F32), 0.0), c_now * xt_now], axis=0
        )
        u0 = u_ext[HALO:]
        u1 = pltpu.roll(u_ext, 1, 0)[HALO:]
        u2 = pltpu.roll(u_ext, 2, 0)[HALO:]
        dyc = dyc_ref[...]
        conv = cw_ref[2:3, :] * u0 + cw_ref[1:2, :] * u1 + cw_ref[0:1, :] * u2
        dproj_ref[:, 0:dc] = (dyc * conv).astype(BF16)
        dconv = dyc * b_now
        dconv_ext = jnp.concatenate(
            [dconv, jnp.where(has_next, dycn_ref[...] * bn_ref[...].astype(F32), 0.0)], axis=0
        )
        du = (
            cw_ref[2:3, :] * dconv
            + cw_ref[1:2, :] * pltpu.roll(dconv_ext, n_ext - 1, 0)[:ts]
            + cw_ref[0:1, :] * pltpu.roll(dconv_ext, n_ext - 2, 0)[:ts]
        )
        dproj_ref[:, dc : 2 * dc] = (du * xt_now).astype(BF16)
        dproj_ref[:, 2 * dc : 3 * dc] = (du * c_now).astype(BF16)
        dcw_ref[0:1, :] += jnp.sum(dconv * u2, axis=0, keepdims=True)
        dcw_ref[1:2, :] += jnp.sum(dconv * u1, axis=0, keepdims=True)
        dcw_ref[2:3, :] += jnp.sum(dconv * u0, axis=0, keepdims=True)

        for gi, window in enumerate(POOL_WINDOWS):
            cols = slice(gi * cg, (gi + 1) * cg)
            v_ext = jnp.concatenate(
                [jnp.where(has_past, vh_ref[:, cols].astype(F32), 0.0), v_ref[:, cols].astype(F32)], axis=0
            )
            mean = _trailing_sums(v_ext, window)[HALO:] * _inverse_counts(i * ts, ts, window)
            diff = (mean - v_ext[HALO:]).astype(BF16)
            z = jnp.dot(diff, pw_ref[gi], preferred_element_type=F32)
            dyp = dyp_ref[:, cols]
            dps_ref[:, cols] += jnp.sum(dyp * z, axis=0, keepdims=True)
            scale = ps_ref[:, cols]
            dz_ext = jnp.concatenate([dyp * scale, jnp.where(has_next, dypn_ref[:, cols] * scale, 0.0)], axis=0)
            dz_ext = dz_ext.astype(BF16)
            dpw_ref[gi] += lax.dot_general(
                diff, dz_ext[:ts], (((0,), (0,)), ((), ())), preferred_element_type=F32
            )
            ddiff_ext = lax.dot_general(
                dz_ext, pw_ref[gi], (((1,), (1,)), ((), ())), preferred_element_type=F32
            )
            q_ext = ddiff_ext * _inverse_counts(i * ts, n_ext, window)
            dv = _leading_sums(q_ext, window)[:ts] - ddiff_ext[:ts]
            dproj_ref[:, 3 * dc + gi * cg : 3 * dc + (gi + 1) * cg] = dv.astype(BF16)

    def col(jc):
        return pl.BlockSpec((ts, dc), lambda i: (i, jc))

    def past(jc):
        return pl.BlockSpec((HALO, dc), lambda i: (jnp.maximum(i * halo_blocks - 1, 0), jc))

    def following(jc):
        return pl.BlockSpec((HALO, dc), lambda i: (jnp.minimum((i + 1) * halo_blocks, n_halo_blocks - 1), jc))

    return pl.pallas_call(
        body,
        name=name,
        grid=(n_tiles,),
        in_specs=[
            col(0), col(1), following(0), following(1),
            col(0), col(1), col(2), col(3), following(0), past(1), past(2), past(3),
            pl.BlockSpec((8, dc), lambda i: (0, 0)),
            pl.BlockSpec((n_groups, cg, cg), lambda i: (0, 0, 0)),
            pl.BlockSpec((1, dp), lambda i: (0, 0)),
        ],
        out_specs=[
            pl.BlockSpec((ts, e), lambda i: (i, 0)),
            pl.BlockSpec((8, dc), lambda i: (0, 0)),
            pl.BlockSpec((n_groups, cg, cg), lambda i: (0, 0, 0)),
            pl.BlockSpec((1, dp), lambda i: (0, 0)),
        ],
        out_shape=[_shape((s, e), BF16), _shape((8, dc), F32), _shape((n_groups, cg, cg), F32), _shape((1, dp), F32)],
        compiler_params=_compute_params("arbitrary"),
    )(dy, dy, dy, dy, proj, proj, proj, proj, proj, proj, proj, proj, conv_w, pool_w, pool_scale)


def _cast_bf16(a, *, name):
    rows, cols = a.shape
    tr = _tile(rows, 512, SUBLANES_BF16)

    def body(a_ref, o_ref):
        o_ref[...] = a_ref[...].astype(BF16)

    spec = pl.BlockSpec((tr, cols), lambda i: (i, 0))
    return pl.pallas_call(
        body, name=name, grid=(rows // tr,), in_specs=[spec], out_specs=spec, out_shape=_shape((rows, cols), BF16),
        compiler_params=_compute_params("parallel"),
    )(a)


def _core_index():
    return lax.axis_index("c").astype(jnp.int32).reshape((1,))


def _add_sibling_half(grads, received, *, name, deps=()):
    nsh, r, c = grads.shape
    hr = r // 2
    tr = _tile(hr, 512, SUBLANES_BF16)
    tiles = hr // tr

    def body(core_ref, *refs):
        g_ref, r_ref, o_ref = refs[len(deps):]
        o_ref[...] = (g_ref[...].astype(F32) + r_ref[...].astype(F32)).astype(BF16)

    half = pl.BlockSpec((None, tr, c), lambda sh, t, core: (sh, t, 0))
    return pl.pallas_call(
        body,
        name=name,
        grid_spec=pltpu.PrefetchScalarGridSpec(
            num_scalar_prefetch=1,
            grid=(nsh, tiles),
            in_specs=[ANY_SPEC] * len(deps)
            + [pl.BlockSpec((None, tr, c), lambda sh, t, core: (sh, core[0] * tiles + t, 0)), half],
            out_specs=half,
        ),
        out_shape=_shape((nsh, hr, c), BF16),
        compiler_params=_compute_params("parallel", "parallel"),
    )(_core_index(), *deps, grads, received)


def _chip_index_operand():
    return _chip_index(lax.axis_index("x"), lax.axis_index("y")).astype(jnp.int32).reshape((1,))


def _sum_chip_partials(own, received, *, name, deps=()):
    nsh, hr, c = received.shape
    tr = _tile(hr, 256, SUBLANES_BF16)

    def body(chip_ref, *refs):
        own_ref, p_ref, o_ref = refs[len(deps):]
        mine = chip_ref[0]
        total = None
        for sh in range(nsh):
            term = jnp.where(mine == sh, own_ref[...], p_ref[sh]).astype(F32)
            total = term if total is None else total + term
        o_ref[...] = total

    return pl.pallas_call(
        body,
        name=name,
        grid_spec=pltpu.PrefetchScalarGridSpec(
            num_scalar_prefetch=1,
            grid=(hr // tr,),
            in_specs=[ANY_SPEC] * len(deps) + [
                pl.BlockSpec((None, tr, c), lambda t, chip: (chip[0], t, 0)),
                pl.BlockSpec((nsh, tr, c), lambda t, chip: (0, t, 0)),
            ],
            out_specs=pl.BlockSpec((tr, c), lambda t, chip: (t, 0)),
        ),
        out_shape=_shape((hr, c), F32),
        compiler_params=_compute_params("parallel"),
    )(_chip_index_operand(), *deps, own, received)


def _adamw(grad, w, m, v, layer, carried, *, name, deps=()):
    n_layers, r, c = w.shape
    in_halves = isinstance(grad, (tuple, list))
    tr = _tile(r // 2 if in_halves else r, 256, 8)
    half_tiles = (r // 2) // tr if in_halves else 0
    bias1 = 1.0 - ADAM_B1 ** ADAM_STEP
    bias2 = 1.0 - ADAM_B2 ** ADAM_STEP
    n_grads = 2 if in_halves else 1

    def body(core_ref, *refs):
        refs = refs[len(deps):]
        w_ref, m_ref, v_ref = refs[n_grads : n_grads + 3]
        go_ref, d_ref, mo_ref, vo_ref, done_ref = refs[-5:]
        done_ref[...] = jnp.zeros_like(done_ref)
        if in_halves:
            in_my_half = pl.program_id(0) // half_tiles == core_ref[0]
            g = jnp.where(in_my_half, refs[0][...], refs[1][...])
        else:
            g = refs[0][...]
        m_new = ADAM_B1 * m_ref[...] + (1.0 - ADAM_B1) * g
        v_new = ADAM_B2 * v_ref[...] + (1.0 - ADAM_B2) * (g * g)
        go_ref[...] = g
        mo_ref[...] = m_new
        vo_ref[...] = v_new
        d_ref[...] = -ADAM_LR * ((m_new / bias1) / (jnp.sqrt(v_new / bias2) + ADAM_EPS) + ADAM_WD * w_ref[...])

    def half_spec(mine):
        def index(t, core):
            first = (core[0] if mine else 1 - core[0]) * half_tiles
            return (jnp.clip(t - first, 0, half_tiles - 1), 0)

        return pl.BlockSpec((tr, c), index)

    layer_tile = pl.BlockSpec((None, tr, c), lambda t, core: (layer, t, 0))
    if in_halves:
        grad_specs, grads = [half_spec(True), half_spec(False)], list(grad)
    else:
        grad_specs, grads = [pl.BlockSpec((tr, c), lambda t, core: (t, 0))], [grad]
    in_specs = [ANY_SPEC] * len(deps) + grad_specs + [layer_tile] * 3
    args = list(deps) + grads + [w, m, v]
    aliases = {}
    if carried is not None:
        aliases = {1 + len(args) + n: n for n in range(4)}
        in_specs += [HBM_SPEC] * 4
        args += list(carried)
    *outs, done = pl.pallas_call(
        body,
        name=name,
        grid_spec=pltpu.PrefetchScalarGridSpec(
            num_scalar_prefetch=1,
            grid=(r // tr,),
            in_specs=in_specs,
            out_specs=[layer_tile] * 4 + [pl.BlockSpec((8, LANES), lambda t, core: (0, 0))],
        ),
        out_shape=[_shape((n_layers, r, c), F32)] * 4 + [_shape((8, LANES), F32)],
        input_output_aliases=aliases,
        compiler_params=_compute_params("arbitrary"),
    )(_core_index(), *args)
    return outs, done


def _place():
    x, y, c = (lax.axis_index(a) for a in MESH_AXES)
    other_chips = [(1 - x, y), (x, 1 - y), (1 - x, 1 - y)]
    return x, y, c, other_chips


def _chip_index(x, y):
    return 2 * x + y


def _gather_small(conv_rows, pool_rows, *, name):
    def body(cv_ref, pw_ref, cv_out, pw_out, send_sems, recv_sems):
        x, y, c, other_chips = _place()
        mine = _chip_index(x, y)
        cv_out[mine] = cv_ref[...]
        pw_out[mine] = pw_ref[...]
        sends = []
        for a, (src, dst) in enumerate(((cv_ref, cv_out), (pw_ref, pw_out))):
            for j, (px, py) in enumerate(other_chips):
                cp = pltpu.make_async_remote_copy(
                    src_ref=src, dst_ref=dst.at[mine], send_sem=send_sems.at[a, j], recv_sem=recv_sems.at[a, j],
                    device_id=(px, py, c), device_id_type=MESH_ID,
                )
                cp.start()
                sends.append(cp)
        for a, (src, dst) in enumerate(((cv_ref, cv_out), (pw_ref, pw_out))):
            for j, (px, py) in enumerate(other_chips):
                pltpu.make_async_remote_copy(
                    src_ref=src, dst_ref=dst.at[_chip_index(px, py)], send_sem=send_sems.at[a, j],
                    recv_sem=recv_sems.at[a, j], device_id=(px, py, c), device_id_type=MESH_ID,
                ).wait_recv()
        for cp in sends:
            cp.wait_send()

    return pl.pallas_call(
        body,
        name=name,
        in_specs=[VMEM_SPEC, VMEM_SPEC],
        out_specs=[VMEM_SPEC, VMEM_SPEC],
        out_shape=[_shape((N_CHIPS,) + conv_rows.shape, F32), _shape((N_CHIPS,) + pool_rows.shape, F32)],
        scratch_shapes=[pltpu.SemaphoreType.DMA((2, 3)), pltpu.SemaphoreType.DMA((2, 3))],
    )(conv_rows, pool_rows)


def _allreduce_small(vec, *, name):
    rows, n = vec.shape
    n_dev = 8

    def body(v_ref, o_ref, slots, send_sems, recv_sems):
        x, y, c, _ = _place()
        me = 4 * x + 2 * y + c
        slots[me] = v_ref[...]
        sends = []
        for mask in range(1, n_dev):
            fx, fy, fc = (mask >> 2) & 1, (mask >> 1) & 1, mask & 1
            peer = (x ^ fx, y ^ fy, c ^ fc)
            cp = pltpu.make_async_remote_copy(
                src_ref=v_ref, dst_ref=slots.at[me], send_sem=send_sems.at[mask - 1], recv_sem=recv_sems.at[mask - 1],
                device_id=peer, device_id_type=MESH_ID,
            )
            cp.start()
            sends.append(cp)
        for mask in range(1, n_dev):
            fx, fy, fc = (mask >> 2) & 1, (mask >> 1) & 1, mask & 1
            peer = (x ^ fx, y ^ fy, c ^ fc)
            pltpu.make_async_remote_copy(
                src_ref=v_ref, dst_ref=slots.at[4 * peer[0] + 2 * peer[1] + peer[2]], send_sem=send_sems.at[mask - 1],
                recv_sem=recv_sems.at[mask - 1], device_id=peer, device_id_type=MESH_ID,
            ).wait_recv()
        total = slots[0]
        for dev in range(1, n_dev):
            total = total + slots[dev]
        o_ref[...] = total
        for cp in sends:
            cp.wait_send()

    return pl.pallas_call(
        body,
        name=name,
        in_specs=[VMEM_SPEC],
        out_specs=VMEM_SPEC,
        out_shape=_shape((rows, n), F32),
        scratch_shapes=[
            pltpu.VMEM((n_dev, rows, n), F32),
            pltpu.SemaphoreType.DMA((n_dev - 1,)),
            pltpu.SemaphoreType.DMA((n_dev - 1,)),
        ],
    )(vec)


def _handshake(peers):
    barrier = pltpu.get_barrier_semaphore()
    for peer in peers:
        pl.semaphore_signal(barrier, inc=1, device_id=peer, device_id_type=MESH_ID)
    pl.semaphore_wait(barrier, len(peers))


def _sequencer_call(body, out_type, scratch_types, collective_id, name):
    return pl.kernel(
        body,
        name=name,
        out_type=out_type,
        mesh=plsc.ScalarSubcoreMesh(axis_name="sequencer", num_cores=1),
        scratch_types=scratch_types,
        compiler_params=pltpu.CompilerParams(collective_id=collective_id),
    )


GATHER_COLLECTIVE_ID = 1
EXCHANGE_COLLECTIVE_ID = 2
SCATTER_COLLECTIVE_ID = 3
SHARE_COLLECTIVE_ID = 4


def _gather_weights(stacked, layer, *, name):
    n = len(stacked)

    def body(*refs):
        ins, outs = refs[:n], refs[n : 2 * n]
        own_sems, send_sems, recv_sems, pass_send_sems, pass_recv_sems = refs[2 * n :]
        x, y, c, other_chips = _place()
        mine = _chip_index(x, y)
        sibling = (x, y, 1 - c)
        _handshake([sibling] + [(px, py, c) for px, py in other_chips])
        pending = []

        def send_my_half(a):
            hr = ins[a].shape[1] // 2
            my_half = pl.ds(c * hr, hr)
            cp = pltpu.make_async_remote_copy(
                src_ref=ins[a].at[layer], dst_ref=outs[a].at[mine], send_sem=own_sems.at[0, a],
                recv_sem=own_sems.at[1, a], device_id=sibling, device_id_type=MESH_ID,
            )
            cp.start()
            pending.append(cp.wait)
            for j, (px, py) in enumerate(other_chips):
                cp = pltpu.make_async_remote_copy(
                    src_ref=ins[a].at[layer, my_half], dst_ref=outs[a].at[mine, my_half], send_sem=send_sems.at[a, j],
                    recv_sem=recv_sems.at[a, j], device_id=(px, py, c), device_id_type=MESH_ID,
                )
                cp.start()
                pending.append(cp.wait_send)

        for a in range(min(2, n)):
            send_my_half(a)
        for a in range(n):
            hr = ins[a].shape[1] // 2
            my_half = pl.ds(c * hr, hr)
            for j, (px, py) in enumerate(other_chips):
                landed = outs[a].at[_chip_index(px, py), my_half]
                pltpu.make_async_remote_copy(
                    src_ref=landed, dst_ref=landed, send_sem=send_sems.at[a, j], recv_sem=recv_sems.at[a, j],
                    device_id=(px, py, c), device_id_type=MESH_ID,
                ).wait_recv()
                cp = pltpu.make_async_remote_copy(
                    src_ref=landed, dst_ref=landed, send_sem=pass_send_sems.at[a, j], recv_sem=pass_recv_sems.at[a, j],
                    device_id=sibling, device_id_type=MESH_ID,
                )
                cp.start()
                pending.append(cp.wait_send)
            if a + 2 < n:
                send_my_half(a + 2)
        for a in range(n):
            hr = ins[a].shape[1] // 2
            sibling_half = pl.ds((1 - c) * hr, hr)
            for j, (px, py) in enumerate(other_chips):
                passed = outs[a].at[_chip_index(px, py), sibling_half]
                pltpu.make_async_remote_copy(
                    src_ref=passed, dst_ref=passed, send_sem=pass_send_sems.at[a, j], recv_sem=pass_recv_sems.at[a, j],
                    device_id=sibling, device_id_type=MESH_ID,
                ).wait_recv()
        for wait in pending:
            wait()

    return _sequencer_call(
        body,
        [_shape((N_CHIPS,) + a.shape[1:], BF16) for a in stacked],
        [pltpu.SemaphoreType.DMA((2, n))] + [pltpu.SemaphoreType.DMA((n, 3))] * 4,
        GATHER_COLLECTIVE_ID,
        name,
    )(*stacked)


def _exchange_halves(grads, *, name):
    n = len(grads)

    def body(*refs):
        ins, outs = refs[:n], refs[n : 2 * n]
        send_sems, recv_sems = refs[2 * n :]
        x, y, c, _ = _place()
        sibling = (x, y, 1 - c)
        _handshake([sibling])
        copies = []
        for a in range(n):
            hr = ins[a].shape[1] // 2
            cp = pltpu.make_async_remote_copy(
                src_ref=ins[a].at[:, pl.ds((1 - c) * hr, hr), :], dst_ref=outs[a], send_sem=send_sems.at[a],
                recv_sem=recv_sems.at[a], device_id=sibling, device_id_type=MESH_ID,
            )
            cp.start()
            copies.append(cp)
        for cp in copies:
            cp.wait()

    return _sequencer_call(
        body,
        [_shape((g.shape[0], g.shape[1] // 2, g.shape[2]), BF16) for g in grads],
        [pltpu.SemaphoreType.DMA((n,)), pltpu.SemaphoreType.DMA((n,))],
        EXCHANGE_COLLECTIVE_ID,
        name,
    )(*grads)


def _scatter_partials(partials, *, name):
    n = len(partials)

    def body(*refs):
        ins, outs = refs[:n], refs[n : 2 * n]
        send_sems, recv_sems = refs[2 * n :]
        x, y, c, other_chips = _place()
        mine = _chip_index(x, y)
        _handshake([(px, py, c) for px, py in other_chips])
        pending = []
        for a in range(n):
            for j, (px, py) in enumerate(other_chips):
                cp = pltpu.make_async_remote_copy(
                    src_ref=ins[a].at[_chip_index(px, py)], dst_ref=outs[a].at[mine], send_sem=send_sems.at[a, j],
                    recv_sem=recv_sems.at[a, j], device_id=(px, py, c), device_id_type=MESH_ID,
                )
                cp.start()
                pending.append(cp.wait_send)
        for a in range(n):
            for j, (px, py) in enumerate(other_chips):
                landed = outs[a].at[_chip_index(px, py)]
                pltpu.make_async_remote_copy(
                    src_ref=landed, dst_ref=landed, send_sem=send_sems.at[a, j], recv_sem=recv_sems.at[a, j],
                    device_id=(px, py, c), device_id_type=MESH_ID,
                ).wait_recv()
        for wait in pending:
            wait()

    return _sequencer_call(
        body,
        [_shape(p.shape, BF16) for p in partials],
        [pltpu.SemaphoreType.DMA((n, 3)), pltpu.SemaphoreType.DMA((n, 3))],
        SCATTER_COLLECTIVE_ID,
        name,
    )(*partials)


def _share_halves(halves, *, name):
    n = len(halves)

    def body(*refs):
        ins, outs = refs[:n], refs[n : 2 * n]
        send_sems, recv_sems = refs[2 * n :]
        x, y, c, _ = _place()
        sibling = (x, y, 1 - c)
        _handshake([sibling])
        copies = []
        for a in range(n):
            cp = pltpu.make_async_remote_copy(
                src_ref=ins[a], dst_ref=outs[a], send_sem=send_sems.at[a], recv_sem=recv_sems.at[a],
                device_id=sibling, device_id_type=MESH_ID,
            )
            cp.start()
            copies.append(cp)
        for cp in copies:
            cp.wait()

    return _sequencer_call(
        body,
        [_shape(h.shape, F32) for h in halves],
        [pltpu.SemaphoreType.DMA((n,)), pltpu.SemaphoreType.DMA((n,))],
        SHARE_COLLECTIVE_ID,
        name,
    )(*halves)


class _ReduceToOwner:
    def __init__(self, grads, tag):
        self.grads, self.tag = grads, tag
        self.received = _exchange_halves(grads, name=f"exchange_halves_{tag}")

    def add_sibling(self, after):
        partials = [
            _add_sibling_half(g, r, name=f"add_sibling_{self.tag}_{a}", deps=after)
            for a, (g, r) in enumerate(zip(self.grads, self.received))
        ]
        self.partials = partials
        self.slots = _scatter_partials(partials, name=f"scatter_partials_{self.tag}")
        return partials

    def sum_chips(self, after):
        halves = [
            _sum_chip_partials(p, s, name=f"sum_partials_{self.tag}_{a}", deps=after)
            for a, (p, s) in enumerate(zip(self.partials, self.slots))
        ]
        self.halves = halves
        self.theirs = _share_halves(halves, name=f"share_halves_{self.tag}")
        return halves

    def totals(self):
        return list(zip(self.halves, self.theirs))


def kernel(x, w_in, conv_w, pool_w, pool_scale, w_out, norm_mix, norm_mlp, w_up, w_down, norm_final, loss_target, m_w_in, m_conv_w, m_pool_w, m_pool_scale, m_w_out, m_norm_mix, m_norm_mlp, m_w_up, m_w_down, m_norm_final, v_w_in, v_conv_w, v_pool_w, v_pool_scale, v_w_out, v_norm_mix, v_norm_mlp, v_w_up, v_w_down, v_norm_final):
    n_layers, d, _ = w_in.shape
    s = x.shape[1]
    dc = conv_w.shape[2] * N_CHIPS
    n_groups, cg_rows, cg = pool_w.shape[1:]
    dp = n_groups * cg
    x0 = x.reshape(s, d)
    target = loss_target.reshape(s, d)

    big = [w_in, w_out, w_up, w_down]
    big_bf16 = [
        _cast_bf16(w.reshape(-1, w.shape[2]), name=f"cast_{t}").reshape(w.shape)
        for t, w in zip(("w_in", "w_out", "w_up", "w_down"), big)
    ]

    conv_rows = jnp.pad(conv_w, ((0, 0), (0, 8 - CONV_TAPS), (0, 0))).reshape(n_layers * 8, -1)
    pool_rows = pool_w.reshape(n_layers * n_groups * cg_rows, cg)
    conv_all, pool_all = _gather_small(conv_rows, pool_rows, name="gather_small")
    conv_full = conv_all.reshape(N_CHIPS, n_layers, 8, -1).transpose(1, 2, 0, 3).reshape(n_layers, 8, dc)
    pool_full = (
        pool_all.reshape(N_CHIPS, n_layers, n_groups, cg_rows, cg).transpose(1, 2, 0, 3, 4)
        .reshape(n_layers, n_groups, cg, cg).astype(BF16)
    )

    saved = []
    xl = x0
    for l in range(n_layers):
        if l == 0:
            win_g, wout_g, wup_g, wdown_g = [
                _gather_weights([w], l, name=f"gather_weights_l{l}_{t}")[0] for t, w in enumerate(big_bf16)
            ]
        else:
            win_g, wout_g, wup_g, wdown_g = _gather_weights(big_bf16, l, name=f"gather_weights_l{l}")
        gain_mix = norm_mix[l].reshape(1, d)
        gain_mlp = norm_mlp[l].reshape(1, d)
        scale = pool_scale[l].reshape(1, dp)
        h1 = _rmsnorm(xl, gain_mix, name=f"norm_mix_l{l}")
        proj = _matmul_cols(h1, win_g, relu2=False, name=f"in_proj_l{l}")
        y = _mixer_fwd(proj, conv_full[l], pool_full[l], scale, name=f"mixer_fwd_l{l}")
        x_mid = _matmul_residual(y, wout_g.reshape(-1, d), xl, name=f"out_proj_l{l}")
        h2 = _rmsnorm(x_mid, gain_mlp, name=f"norm_mlp_l{l}")
        u, u2 = _matmul_cols(h2, wup_g, relu2=True, name=f"up_proj_l{l}")
        x_next = _matmul_residual(u2, wdown_g.reshape(-1, d), x_mid, name=f"down_proj_l{l}")
        saved.append((xl, h1, proj, y, x_mid, h2, u, u2, win_g, wout_g, wup_g, wdown_g, gain_mix, gain_mlp, scale))
        xl = x_next

    loss_part, dx, dx_bf16, d_norm_final = _loss_and_grad(xl, norm_final.reshape(1, d), target, name="loss_head")
    loss = lax.psum(loss_part[0, 0], MESH_AXES)

    small_grads = [None] * n_layers
    carried = [None] * 5
    pool_params = tuple(p.reshape(n_layers, n_groups * cg_rows, cg) for p in (pool_w, m_pool_w, v_pool_w))
    params = [(w_in, m_w_in, v_w_in), (w_out, m_w_out, v_w_out), (w_up, m_w_up, v_w_up), (w_down, m_w_down, v_w_down),
              pool_params]
    MLP_PARAMS, MIX_PARAMS = (2, 3), (0, 1, 4)

    def update(reduce, which, layer, after):
        dones = []
        for a, total in zip(which, reduce.totals()):
            w, m, v = params[a]
            carried[a], done = _adamw(total, w, m, v, layer, carried[a], name=f"adamw_{a}_l{layer}", deps=after)
            dones.append(done)
        return dones

    mlp_above = mix_above = None
    for l in reversed(range(n_layers)):
        xl, h1, proj, y, x_mid, h2, u, u2, win_g, wout_g, wup_g, wdown_g, gain_mix, gain_mlp, scale = saved[l]
        above = mlp_above is not None
        deps = mix_above.add_sibling([dx_bf16]) if above else []
        g_down = _matmul_tn(u2, dx_bf16, n_shards=N_CHIPS, shard_cols=False, name=f"grad_w_down_l{l}", deps=deps)
        deps = mlp_above.sum_chips([g_down]) if above else [g_down]
        da = _matmul_nt(dx_bf16, wdown_g.reshape(-1, d), u=u, name=f"grad_act_l{l}", deps=deps)
        g_up = _matmul_tn(h2, da, n_shards=N_CHIPS, shard_cols=True, name=f"grad_w_up_l{l}")
        mlp = _ReduceToOwner([g_up, g_down], f"mlp_l{l}")
        deps = mix_above.sum_chips([g_up]) if above else [g_up]
        dx_mid, dx_mid_bf16, d_gain_mlp = _matmul_nt_norm_bwd(
            da, wup_g, x_mid, gain_mlp, dx, name=f"grad_mid_l{l}", deps=deps
        )
        g_out = _matmul_tn(
            y, dx_mid_bf16, n_shards=N_CHIPS, shard_cols=False, name=f"grad_w_out_l{l}", deps=mlp.add_sibling([dx_mid_bf16])
        )
        deps = update(mlp_above, MLP_PARAMS, l + 1, [g_out]) if above else [g_out]
        dy = _matmul_nt(dx_mid_bf16, wout_g.reshape(-1, d), name=f"grad_mixed_l{l}", deps=deps)
        dproj, d_conv, d_pool, d_scale = _mixer_bwd(dy, proj, conv_full[l], pool_full[l], scale, name=f"mixer_bwd_l{l}")
        deps = update(mix_above, MIX_PARAMS, l + 1, [dproj]) if above else []
        g_in = _matmul_tn(h1, dproj, n_shards=N_CHIPS, shard_cols=True, name=f"grad_w_in_l{l}", deps=deps)
        g_pool = (
            d_pool.reshape(n_groups, N_CHIPS, cg_rows, cg).transpose(1, 0, 2, 3)
            .reshape(N_CHIPS, n_groups * cg_rows, cg).astype(BF16)
        )
        mix = _ReduceToOwner([g_in, g_out, g_pool], f"mix_l{l}")
        dx, dx_bf16, d_gain_mix = _matmul_nt_norm_bwd(
            dproj, win_g, xl, gain_mix, dx_mid, name=f"grad_x_l{l}", deps=[g_in]
        )
        small_grads[l] = jnp.concatenate(
            [d_conv[:CONV_TAPS].reshape(-1), d_scale.reshape(-1), d_gain_mix.reshape(-1), d_gain_mlp.reshape(-1)]
        )
        mlp_above, mix_above = mlp, mix
    deps = mix_above.add_sibling([dx_bf16])
    deps = mlp_above.sum_chips(deps)
    deps = mix_above.sum_chips(deps)
    deps = update(mlp_above, MLP_PARAMS, 0, deps)
    update(mix_above, MIX_PARAMS, 0, deps)

    vec = jnp.concatenate(small_grads + [d_norm_final.reshape(-1)])
    vec = _allreduce_small(vec.reshape(8, -1), name="allreduce_small").reshape(-1)
    per_layer = vec[: n_layers * (CONV_TAPS * dc + dp + 2 * d)].reshape(n_layers, -1)
    chip = _chip_index(lax.axis_index("x"), lax.axis_index("y"))
    dcs = dc // N_CHIPS
    g_conv = lax.dynamic_slice_in_dim(per_layer[:, : CONV_TAPS * dc].reshape(n_layers, CONV_TAPS, dc), chip * dcs, dcs, axis=2)
    g_scale = per_layer[:, CONV_TAPS * dc : CONV_TAPS * dc + dp]
    g_mix = per_layer[:, CONV_TAPS * dc + dp : CONV_TAPS * dc + dp + d]
    g_mlp = per_layer[:, CONV_TAPS * dc + dp + d :]
    g_final = vec[n_layers * (CONV_TAPS * dc + dp + 2 * d) :]

    def small_adamw(g, w, m, v, tag):
        flat = lambda t: t.reshape(1, -1, t.shape[-1])
        out, _ = _adamw(flat(g)[0], flat(w), flat(m), flat(v), 0, None, name=f"adamw_{tag}")
        return [o.reshape(w.shape) for o in out]

    o_conv = small_adamw(g_conv, conv_w, m_conv_w, v_conv_w, "conv_w")
    o_scale = small_adamw(g_scale, pool_scale, m_pool_scale, v_pool_scale, "pool_scale")
    o_mix = small_adamw(g_mix, norm_mix, m_norm_mix, v_norm_mix, "norm_mix")
    o_mlp = small_adamw(g_mlp, norm_mlp, m_norm_mlp, v_norm_mlp, "norm_mlp")
    o_final = small_adamw(g_final, norm_final, m_norm_final, v_norm_final, "norm_final")
    o_in, o_out, o_up, o_down, o_pool = carried
    o_pool = [o.reshape(pool_w.shape) for o in o_pool]

    ordered = [o_in, o_conv, o_pool, o_scale, o_out, o_mix, o_mlp, o_up, o_down, o_final]
    return (loss, dx.reshape(x.shape), *[o[0] for o in ordered], *[o[1] for o in ordered], *[o[2] for o in ordered],
            *[o[3] for o in ordered])
```

```python
import functools

import jax
import jax.numpy as jnp
from jax import lax
from jax.experimental import pallas as pl
from jax.experimental.pallas import tpu as pltpu
from jax.experimental.pallas import tpu_sc as plsc

F32 = jnp.float32
BF16 = jnp.bfloat16

EPS = 1e-6
POOL_WINDOWS = (2, 4, 8, 16)
CONV_TAPS = 3
HALO = 16

ADAM_LR = 0.001
ADAM_B1 = 0.9
ADAM_B2 = 0.999
ADAM_EPS = 1e-08
ADAM_WD = 0.01
ADAM_STEP = 10

N_CHIPS = 4
MESH_AXES = ("x", "y", "c")
V7X_VMEM_LIMIT_BYTES = 56 * 1024 * 1024
SUBLANES_BF16 = 16
LANES = 128

HBM_SPEC = pl.BlockSpec(memory_space=pltpu.HBM)
VMEM_SPEC = pl.BlockSpec(memory_space=pltpu.VMEM)
MESH_ID = pl.DeviceIdType.MESH


def _tile(dim, target, align):
    if dim <= target:
        return dim
    t = (target // align) * align
    while dim % t:
        t -= align
    assert t > 0, (dim, target, align)
    return t


def _compute_params(*semantics):
    return pltpu.CompilerParams(dimension_semantics=semantics, vmem_limit_bytes=V7X_VMEM_LIMIT_BYTES)


def _shape(shape, dtype):
    return jax.ShapeDtypeStruct(shape, dtype)


ANY_SPEC = pl.BlockSpec(memory_space=pl.ANY)


def _behind(body, deps):
    return lambda *refs: body(*refs[len(deps):])


def _rmsnorm_bwd(dh, x, gain, dres):
    r = lax.rsqrt(jnp.mean(x * x, axis=-1, keepdims=True) + EPS)
    xn = x * r
    dgain = jnp.sum(dh * xn, axis=0, keepdims=True)
    dxn = dh * gain
    dx = r * (dxn - xn * jnp.mean(dxn * xn, axis=-1, keepdims=True))
    if dres is not None:
        dx = dx + dres
    return dx, dgain


MXU_ROWS = 512


def _rmsnorm(x, gain, *, name):
    s, d = x.shape
    tm = _tile(s, 256, SUBLANES_BF16)

    def body(x_ref, g_ref, h_ref):
        xf = x_ref[...]
        r = lax.rsqrt(jnp.mean(xf * xf, axis=-1, keepdims=True) + EPS)
        h_ref[...] = ((xf * r) * g_ref[...]).astype(BF16)

    row_tile = pl.BlockSpec((tm, d), lambda i: (i, 0))
    return pl.pallas_call(
        body, name=name, grid=(s // tm,), in_specs=[row_tile, pl.BlockSpec((1, d), lambda i: (0, 0))],
        out_specs=row_tile, out_shape=_shape((s, d), BF16), compiler_params=_compute_params("parallel"),
    )(x, gain)


def _matmul_cols(h, w, *, relu2, name):
    s, k = h.shape
    nsh, _, c = w.shape
    tm = _tile(s, 2048, MXU_ROWS)
    tn = _tile(c, 1024, LANES)
    rows = min(MXU_ROWS, tm)
    per_shard = c // tn

    def body(h_ref, w_ref, *out_refs):
        def step(t, carry):
            rs = pl.ds(pl.multiple_of(t * rows, rows), rows)
            acc = jnp.dot(h_ref[rs, :], w_ref[...], preferred_element_type=F32)
            if relu2:
                u = jnp.maximum(acc, 0.0)
                out_refs[0][rs, :] = u.astype(BF16)
                out_refs[1][rs, :] = (u * u).astype(BF16)
            else:
                out_refs[0][rs, :] = acc.astype(BF16)
            return carry

        lax.fori_loop(0, tm // rows, step, 0)

    tile_out = pl.BlockSpec((tm, tn), lambda i, j: (i, j))
    n = nsh * c
    outs = [_shape((s, n), BF16)] * (2 if relu2 else 1)
    result = pl.pallas_call(
        body,
        name=name,
        grid=(s // tm, n // tn),
        in_specs=[
            pl.BlockSpec((tm, k), lambda i, j: (i, 0)),
            pl.BlockSpec((None, k, tn), lambda i, j: (j // per_shard, 0, j % per_shard)),
        ],
        out_specs=[tile_out] * len(outs),
        out_shape=outs,
        compiler_params=_compute_params("parallel", "arbitrary"),
    )(h, w)
    return result if relu2 else result[0]


def _matmul_residual(a, w, res, *, name):
    s, k = a.shape
    _, n = w.shape
    tm = _tile(s, 1024, MXU_ROWS)
    tn = _tile(n, 2048, LANES)
    tk = _tile(k, 1024, LANES)
    rows = min(MXU_ROWS, tm)

    def body(a_ref, w_ref, r_ref, o_ref):
        kk = pl.program_id(2)

        def step(t, carry):
            rs = pl.ds(pl.multiple_of(t * rows, rows), rows)
            prod = jnp.dot(a_ref[rs, :], w_ref[...], preferred_element_type=F32)

            @pl.when(kk == 0)
            def _():
                o_ref[rs, :] = r_ref[rs, :] + prod

            @pl.when(kk > 0)
            def _():
                o_ref[rs, :] += prod

            return carry

        lax.fori_loop(0, tm // rows, step, 0)

    return pl.pallas_call(
        body,
        name=name,
        grid=(s // tm, n // tn, k // tk),
        in_specs=[
            pl.BlockSpec((tm, tk), lambda i, j, kk: (i, kk)),
            pl.BlockSpec((tk, tn), lambda i, j, kk: (kk, j)),
            pl.BlockSpec((tm, tn), lambda i, j, kk: (i, j)),
        ],
        out_specs=pl.BlockSpec((tm, tn), lambda i, j, kk: (i, j)),
        out_shape=_shape((s, n), F32),
        compiler_params=_compute_params("parallel", "parallel", "arbitrary"),
    )(a, w, res)


def _matmul_tn(a, g, *, n_shards, shard_cols, name, deps=()):
    m, kd = a.shape
    _, n = g.shape
    r, c = (kd, n // n_shards) if shard_cols else (kd // n_shards, n)
    tr = _tile(kd, 2048, LANES)
    tn = _tile(c, 1024, LANES)
    tm = _tile(m, 1024, SUBLANES_BF16)
    nm = m // tm
    cols_per_shard = c // tn

    def body(a_ref, g_ref, o_ref, acc_ref):
        mm = pl.program_id(2)
        prod = lax.dot_general(a_ref[...], g_ref[...], (((0,), (0,)), ((), ())), preferred_element_type=F32)
        if nm == 1:
            o_ref[...] = prod.astype(BF16)
            return

        @pl.when(mm == 0)
        def _():
            acc_ref[...] = prod

        @pl.when((mm > 0) & (mm < nm - 1))
        def _():
            acc_ref[...] += prod

        @pl.when(mm == nm - 1)
        def _():
            o_ref[...] = (acc_ref[...] + prod).astype(BF16)

    if shard_cols:
        out_spec = pl.BlockSpec((None, tr, tn), lambda i, j, mm: (j // cols_per_shard, i, j % cols_per_shard))
        out_shape = _shape((n_shards, kd, c), BF16)
    else:
        out_spec = pl.BlockSpec((tr, tn), lambda i, j, mm: (i, j))
        out_shape = _shape((kd, n), BF16)
    out = pl.pallas_call(
        _behind(body, deps),
        name=name,
        grid=(kd // tr, n // tn, nm),
        in_specs=[ANY_SPEC] * len(deps) + [
            pl.BlockSpec((tm, tr), lambda i, j, mm: (mm, i)),
            pl.BlockSpec((tm, tn), lambda i, j, mm: (mm, j)),
        ],
        out_specs=out_spec,
        out_shape=out_shape,
        scratch_shapes=[pltpu.VMEM((tr, tn), F32)],
        compiler_params=_compute_params("parallel", "parallel", "arbitrary"),
    )(*deps, a, g)
    return out.reshape(n_shards, r, c)


def _matmul_nt(g, w, *, u=None, name, deps=()):
    m, n = g.shape
    kd, _ = w.shape
    tm = _tile(m, 2048, MXU_ROWS)
    tj = _tile(kd, 1024, LANES)
    rows = min(MXU_ROWS, tm)

    def body(*refs):
        if u is None:
            g_ref, w_ref, o_ref = refs
        else:
            g_ref, w_ref, u_ref, o_ref = refs

        def step(t, carry):
            rs = pl.ds(pl.multiple_of(t * rows, rows), rows)
            prod = lax.dot_general(g_ref[rs, :], w_ref[...], (((1,), (1,)), ((), ())), preferred_element_type=F32)
            if u is None:
                o_ref[rs, :] = prod
            else:
                o_ref[rs, :] = (prod * (2.0 * u_ref[rs, :].astype(F32))).astype(BF16)
            return carry

        lax.fori_loop(0, tm // rows, step, 0)

    tile_out = pl.BlockSpec((tm, tj), lambda i, j: (i, j))
    in_specs = [pl.BlockSpec((tm, n), lambda i, j: (i, 0)), pl.BlockSpec((tj, n), lambda i, j: (j, 0))]
    args = [g, w]
    if u is not None:
        in_specs.append(tile_out)
        args.append(u)
    return pl.pallas_call(
        _behind(body, deps),
        name=name,
        grid=(m // tm, kd // tj),
        in_specs=[ANY_SPEC] * len(deps) + in_specs,
        out_specs=tile_out,
        out_shape=_shape((m, kd), F32 if u is None else BF16),
        compiler_params=_compute_params("parallel", "arbitrary"),
    )(*deps, *args)


EPILOGUE_ROWS = 128


def _matmul_nt_norm_bwd(g, w, x, gain, dres, *, name, deps=()):
    s, n = g.shape
    nsh, d, c = w.shape
    tm = _tile(s, 2048, MXU_ROWS)
    tk = _tile(c, 1024, LANES)
    per_shard = c // tk
    nk = n // tk
    mxu_rows = min(MXU_ROWS, tm)
    rows = min(EPILOGUE_ROWS, tm)
    n_chunks = tm // rows
    assert n_chunks >= 2

    def body(g_ref, w_ref, gain_ref, x_hbm, dres_hbm, dx_hbm, dxb_hbm, dg_ref, acc_ref, x_buf, dres_buf, dx_buf,
             dxb_buf, in_sems, out_sems):
        i = pl.program_id(0)
        kk = pl.program_id(1)

        @pl.when((kk == 0) & (i == 0))
        def _():
            dg_ref[...] = jnp.zeros_like(dg_ref)

        def accumulate(t, carry):
            rs = pl.ds(pl.multiple_of(t * mxu_rows, mxu_rows), mxu_rows)
            prod = lax.dot_general(g_ref[rs, :], w_ref[...], (((1,), (1,)), ((), ())), preferred_element_type=F32)

            @pl.when(kk == 0)
            def _():
                acc_ref[rs, :] = prod

            @pl.when(kk > 0)
            def _():
                acc_ref[rs, :] += prod

            return carry

        lax.fori_loop(0, tm // mxu_rows, accumulate, 0)

        def rows_of(j):
            return pl.ds(pl.multiple_of(i * tm + j * rows, rows), rows)

        def fetches(j, slot):
            return (
                pltpu.make_async_copy(x_hbm.at[rows_of(j), :], x_buf.at[slot], in_sems.at[0, slot]),
                pltpu.make_async_copy(dres_hbm.at[rows_of(j), :], dres_buf.at[slot], in_sems.at[1, slot]),
            )

        def stores(j, slot):
            return (
                pltpu.make_async_copy(dx_buf.at[slot], dx_hbm.at[rows_of(j), :], out_sems.at[0, slot]),
                pltpu.make_async_copy(dxb_buf.at[slot], dxb_hbm.at[rows_of(j), :], out_sems.at[1, slot]),
            )

        @pl.when(kk == nk - 1)
        def _():
            for cp in fetches(0, 0):
                cp.start()

            def step(j, carry):
                slot = j % 2
                for cp in fetches(j, slot):
                    cp.wait()

                @pl.when(j + 1 < n_chunks)
                def _():
                    for cp in fetches(j + 1, 1 - slot):
                        cp.start()

                @pl.when(j >= 2)
                def _():
                    for cp in stores(j - 2, slot):
                        cp.wait()

                rs = pl.ds(pl.multiple_of(j * rows, rows), rows)
                dx, dgain = _rmsnorm_bwd(acc_ref[rs, :], x_buf[slot], gain_ref[...], dres_buf[slot])
                dx_buf[slot] = dx
                dxb_buf[slot] = dx.astype(BF16)
                for cp in stores(j, slot):
                    cp.start()
                dg_ref[...] += dgain
                return carry

            lax.fori_loop(0, n_chunks, step, 0)
            for j in (n_chunks - 2, n_chunks - 1):
                for cp in stores(j, j % 2):
                    cp.wait()

    vec = pl.BlockSpec((1, d), lambda i, kk: (0, 0))
    return pl.pallas_call(
        _behind(body, deps),
        name=name,
        grid=(s // tm, nk),
        in_specs=[ANY_SPEC] * len(deps) + [
            pl.BlockSpec((tm, tk), lambda i, kk: (i, kk)),
            pl.BlockSpec((None, d, tk), lambda i, kk: (kk // per_shard, 0, kk % per_shard)),
            vec,
            ANY_SPEC,
            ANY_SPEC,
        ],
        out_specs=[ANY_SPEC, ANY_SPEC, vec],
        out_shape=[_shape((s, d), F32), _shape((s, d), BF16), _shape((1, d), F32)],
        scratch_shapes=[
            pltpu.VMEM((tm, d), F32),
            pltpu.VMEM((2, rows, d), F32),
            pltpu.VMEM((2, rows, d), F32),
            pltpu.VMEM((2, rows, d), F32),
            pltpu.VMEM((2, rows, d), BF16),
            pltpu.SemaphoreType.DMA((2, 2)),
            pltpu.SemaphoreType.DMA((2, 2)),
        ],
        compiler_params=_compute_params("arbitrary", "arbitrary"),
    )(*deps, g, w, gain, x, dres)


def _loss_and_grad(x, gain, target, *, name):
    s, d = x.shape
    tm = _tile(s, 256, SUBLANES_BF16)

    def body(x_ref, gain_ref, t_ref, loss_ref, dx_ref, dxb_ref, dg_ref):
        @pl.when(pl.program_id(0) == 0)
        def _():
            loss_ref[...] = jnp.zeros_like(loss_ref)
            dg_ref[...] = jnp.zeros_like(dg_ref)

        xf = x_ref[...]
        r = lax.rsqrt(jnp.mean(xf * xf, axis=-1, keepdims=True) + EPS)
        err = (xf * r) * gain_ref[...] - t_ref[...]
        loss_ref[...] += 0.5 * jnp.sum(jnp.mean(err * err, axis=-1, keepdims=True))
        dx, dgain = _rmsnorm_bwd(err * (1.0 / d), xf, gain_ref[...], None)
        dx_ref[...] = dx
        dxb_ref[...] = dx.astype(BF16)
        dg_ref[...] += dgain

    row_tile = pl.BlockSpec((tm, d), lambda i: (i, 0))
    vec = pl.BlockSpec((1, d), lambda i: (0, 0))
    return pl.pallas_call(
        body,
        name=name,
        grid=(s // tm,),
        in_specs=[row_tile, vec, row_tile],
        out_specs=[pl.BlockSpec((1, LANES), lambda i: (0, 0)), row_tile, row_tile, vec],
        out_shape=[_shape((1, LANES), F32), _shape((s, d), F32), _shape((s, d), BF16), _shape((1, d), F32)],
        compiler_params=_compute_params("arbitrary"),
    )(x, gain, target)


def _trailing_sums(v_ext, window):
    acc, span = v_ext, 1
    while span < window:
        acc = acc + pltpu.roll(acc, span, 0)
        span *= 2
    return acc


def _leading_sums(q_ext, window):
    n = q_ext.shape[0]
    acc, span = q_ext, 1
    while span < window:
        acc = acc + pltpu.roll(acc, n - span, 0)
        span *= 2
    return acc


def _inverse_counts(first_token, rows, window):
    t = first_token + lax.broadcasted_iota(jnp.int32, (rows, 1), 0)
    return 1.0 / jnp.minimum(t + 1, window).astype(F32)


def _mixer_fwd(proj, conv_w, pool_w, pool_scale, *, name):
    s, _ = proj.shape
    dc = conv_w.shape[1]
    n_groups, cg, _ = pool_w.shape
    dp = n_groups * cg
    assert dc == dp and all(w & (w - 1) == 0 and w <= HALO for w in POOL_WINDOWS)
    ts = _tile(s, 256, HALO)
    halo_blocks = ts // HALO

    def body(b_ref, c_ref, xt_ref, v_ref, ch_ref, xth_ref, vh_ref, cw_ref, pw_ref, ps_ref, y_ref):
        i = pl.program_id(0)
        has_past = i > 0
        u_ext = jnp.concatenate(
            [
                jnp.where(has_past, ch_ref[...].astype(F32) * xth_ref[...].astype(F32), 0.0),
                c_ref[...].astype(F32) * xt_ref[...].astype(F32),
            ],
            axis=0,
        )
        conv = (
            cw_ref[2:3, :] * u_ext[HALO:]
            + cw_ref[1:2, :] * pltpu.roll(u_ext, 1, 0)[HALO:]
            + cw_ref[0:1, :] * pltpu.roll(u_ext, 2, 0)[HALO:]
        )
        y_ref[:, 0:dc] = (b_ref[...].astype(F32) * conv).astype(BF16)
        for gi, window in enumerate(POOL_WINDOWS):
            cols = slice(gi * cg, (gi + 1) * cg)
            v_ext = jnp.concatenate(
                [jnp.where(has_past, vh_ref[:, cols].astype(F32), 0.0), v_ref[:, cols].astype(F32)], axis=0
            )
            mean = _trailing_sums(v_ext, window)[HALO:] * _inverse_counts(i * ts, ts, window)
            diff = (mean - v_ext[HALO:]).astype(BF16)
            z = jnp.dot(diff, pw_ref[gi], preferred_element_type=F32)
            y_ref[:, dc + gi * cg : dc + (gi + 1) * cg] = (z * ps_ref[:, cols]).astype(BF16)

    def col(jc):
        return pl.BlockSpec((ts, dc), lambda i: (i, jc))

    def past(jc):
        return pl.BlockSpec((HALO, dc), lambda i: (jnp.maximum(i * halo_blocks - 1, 0), jc))

    return pl.pallas_call(
        body,
        name=name,
        grid=(s // ts,),
        in_specs=[
            col(0), col(1), col(2), col(3), past(1), past(2), past(3),
            pl.BlockSpec((8, dc), lambda i: (0, 0)),
            pl.BlockSpec((n_groups, cg, cg), lambda i: (0, 0, 0)),
            pl.BlockSpec((1, dp), lambda i: (0, 0)),
        ],
        out_specs=pl.BlockSpec((ts, dc + dp), lambda i: (i, 0)),
        out_shape=_shape((s, dc + dp), BF16),
        compiler_params=_compute_params("parallel"),
    )(proj, proj, proj, proj, proj, proj, proj, conv_w, pool_w, pool_scale)


def _mixer_bwd(dy, proj, conv_w, pool_w, pool_scale, *, name):
    s, e = proj.shape
    dc = conv_w.shape[1]
    n_groups, cg, _ = pool_w.shape
    dp = n_groups * cg
    ts = _tile(s, 256, HALO)
    halo_blocks = ts // HALO
    n_tiles = s // ts
    n_halo_blocks = s // HALO
    n_ext = ts + HALO

    def body(dyc_ref, dyp_ref, dycn_ref, dypn_ref, b_ref, c_ref, xt_ref, v_ref, bn_ref, ch_ref, xth_ref, vh_ref,
             cw_ref, pw_ref, ps_ref, dproj_ref, dcw_ref, dpw_ref, dps_ref):
        i = pl.program_id(0)
        has_past = i > 0
        has_next = i < n_tiles - 1

        @pl.when(i == 0)
        def _():
            dcw_ref[...] = jnp.zeros_like(dcw_ref)
            dpw_ref[...] = jnp.zeros_like(dpw_ref)
            dps_ref[...] = jnp.zeros_like(dps_ref)

        c_now, xt_now, b_now = c_ref[...].astype(F32), xt_ref[...].astype(F32), b_ref[...].astype(F32)
        u_ext = jnp.concatenate(
            [jnp.where(has_past, ch_ref[...].astype(F32) * xth_ref[...].astype(F32), 0.0), c_now * xt_now], axis=0
        )
        u0 = u_ext[HALO:]
        u1 = pltpu.roll(u_ext, 1, 0)[HALO:]
        u2 = pltpu.roll(u_ext, 2, 0)[HALO:]
        dyc = dyc_ref[...]
        conv = cw_ref[2:3, :] * u0 + cw_ref[1:2, :] * u1 + cw_ref[0:1, :] * u2
        dproj_ref[:, 0:dc] = (dyc * conv).astype(BF16)
        dconv = dyc * b_now
        dconv_ext = jnp.concatenate(
            [dconv, jnp.where(has_next, dycn_ref[...] * bn_ref[...].astype(F32), 0.0)], axis=0
        )
        du = (
            cw_ref[2:3, :] * dconv
            + cw_ref[1:2, :] * pltpu.roll(dconv_ext, n_ext - 1, 0)[:ts]
            + cw_ref[0:1, :] * pltpu.roll(dconv_ext, n_ext - 2, 0)[:ts]
        )
        dproj_ref[:, dc : 2 * dc] = (du * xt_now).astype(BF16)
        dproj_ref[:, 2 * dc : 3 * dc] = (du * c_now).astype(BF16)
        dcw_ref[0:1, :] += jnp.sum(dconv * u2, axis=0, keepdims=True)
        dcw_ref[1:2, :] += jnp.sum(dconv * u1, axis=0, keepdims=True)
        dcw_ref[2:3, :] += jnp.sum(dconv * u0, axis=0, keepdims=True)

        for gi, window in enumerate(POOL_WINDOWS):
            cols = slice(gi * cg, (gi + 1) * cg)
            v_ext = jnp.concatenate(
                [jnp.where(has_past, vh_ref[:, cols].astype(F32), 0.0), v_ref[:, cols].astype(F32)], axis=0
            )
            mean = _trailing_sums(v_ext, window)[HALO:] * _inverse_counts(i * ts, ts, window)
            diff = (mean - v_ext[HALO:]).astype(BF16)
            z = jnp.dot(diff, pw_ref[gi], preferred_element_type=F32)
            dyp = dyp_ref[:, cols]
            dps_ref[:, cols] += jnp.sum(dyp * z, axis=0, keepdims=True)
            scale = ps_ref[:, cols]
            dz_ext = jnp.concatenate([dyp * scale, jnp.where(has_next, dypn_ref[:, cols] * scale, 0.0)], axis=0)
            dz_ext = dz_ext.astype(BF16)
            dpw_ref[gi] += lax.dot_general(
                diff, dz_ext[:ts], (((0,), (0,)), ((), ())), preferred_element_type=F32
            )
            ddiff_ext = lax.dot_general(
                dz_ext, pw_ref[gi], (((1,), (1,)), ((), ())), preferred_element_type=F32
            )
            q_ext = ddiff_ext * _inverse_counts(i * ts, n_ext, window)
            dv = _leading_sums(q_ext, window)[:ts] - ddiff_ext[:ts]
            dproj_ref[:, 3 * dc + gi * cg : 3 * dc + (gi + 1) * cg] = dv.astype(BF16)

    def col(jc):
        return pl.BlockSpec((ts, dc), lambda i: (i, jc))

    def past(jc):
        return pl.BlockSpec((HALO, dc), lambda i: (jnp.maximum(i * halo_blocks - 1, 0), jc))

    def following(jc):
        return pl.BlockSpec((HALO, dc), lambda i: (jnp.minimum((i + 1) * halo_blocks, n_halo_blocks - 1), jc))

    return pl.pallas_call(
        body,
        name=name,
        grid=(n_tiles,),
        in_specs=[
            col(0), col(1), following(0), following(1),
            col(0), col(1), col(2), col(3), following(0), past(1), past(2), past(3),
            pl.BlockSpec((8, dc), lambda i: (0, 0)),
            pl.BlockSpec((n_groups, cg, cg), lambda i: (0, 0, 0)),
            pl.BlockSpec((1, dp), lambda i: (0, 0)),
        ],
        out_specs=[
            pl.BlockSpec((ts, e), lambda i: (i, 0)),
            pl.BlockSpec((8, dc), lambda i: (0, 0)),
            pl.BlockSpec((n_groups, cg, cg), lambda i: (0, 0, 0)),
            pl.BlockSpec((1, dp), lambda i: (0, 0)),
        ],
        out_shape=[_shape((s, e), BF16), _shape((8, dc), F32), _shape((n_groups, cg, cg), F32), _shape((1, dp), F32)],
        compiler_params=_compute_params("arbitrary"),
    )(dy, dy, dy, dy, proj, proj, proj, proj, proj, proj, proj, proj, conv_w, pool_w, pool_scale)


def _cast_bf16(a, *, name):
    rows, cols = a.shape
    tr = _tile(rows, 512, SUBLANES_BF16)

    def body(a_ref, o_ref):
        o_ref[...] = a_ref[...].astype(BF16)

    spec = pl.BlockSpec((tr, cols), lambda i: (i, 0))
    return pl.pallas_call(
        body, name=name, grid=(rows // tr,), in_specs=[spec], out_specs=spec, out_shape=_shape((rows, cols), BF16),
        compiler_params=_compute_params("parallel"),
    )(a)


def _core_index():
    return lax.axis_index("c").astype(jnp.int32).reshape((1,))


def _add_sibling_half(grads, received, *, name, deps=()):
    nsh, r, c = grads.shape
    hr = r // 2
    tr = _tile(hr, 512, SUBLANES_BF16)
    tiles = hr // tr

    def body(core_ref, *refs):
        g_ref, r_ref, o_ref = refs[len(deps):]
        o_ref[...] = (g_ref[...].astype(F32) + r_ref[...].astype(F32)).astype(BF16)

    half = pl.BlockSpec((None, tr, c), lambda sh, t, core: (sh, t, 0))
    return pl.pallas_call(
        body,
        name=name,
        grid_spec=pltpu.PrefetchScalarGridSpec(
            num_scalar_prefetch=1,
            grid=(nsh, tiles),
            in_specs=[ANY_SPEC] * len(deps)
            + [pl.BlockSpec((None, tr, c), lambda sh, t, core: (sh, core[0] * tiles + t, 0)), half],
            out_specs=half,
        ),
        out_shape=_shape((nsh, hr, c), BF16),
        compiler_params=_compute_params("parallel", "parallel"),
    )(_core_index(), *deps, grads, received)


def _chip_index_operand():
    return _chip_index(lax.axis_index("x"), lax.axis_index("y")).astype(jnp.int32).reshape((1,))


def _sum_chip_partials(own, received, *, name, deps=()):
    nsh, hr, c = received.shape
    tr = _tile(hr, 256, SUBLANES_BF16)

    def body(chip_ref, *refs):
        own_ref, p_ref, o_ref = refs[len(deps):]
        mine = chip_ref[0]
        total = None
        for sh in range(nsh):
            term = jnp.where(mine == sh, own_ref[...], p_ref[sh]).astype(F32)
            total = term if total is None else total + term
        o_ref[...] = total

    return pl.pallas_call(
        body,
        name=name,
        grid_spec=pltpu.PrefetchScalarGridSpec(
            num_scalar_prefetch=1,
            grid=(hr // tr,),
            in_specs=[ANY_SPEC] * len(deps) + [
                pl.BlockSpec((None, tr, c), lambda t, chip: (chip[0], t, 0)),
                pl.BlockSpec((nsh, tr, c), lambda t, chip: (0, t, 0)),
            ],
            out_specs=pl.BlockSpec((tr, c), lambda t, chip: (t, 0)),
        ),
        out_shape=_shape((hr, c), F32),
        compiler_params=_compute_params("parallel"),
    )(_chip_index_operand(), *deps, own, received)


def _adamw(grad, w, m, v, layer, carried, *, name, deps=()):
    n_layers, r, c = w.shape
    in_halves = isinstance(grad, (tuple, list))
    tr = _tile(r // 2 if in_halves else r, 256, 8)
    half_tiles = (r // 2) // tr if in_halves else 0
    bias1 = 1.0 - ADAM_B1 ** ADAM_STEP
    bias2 = 1.0 - ADAM_B2 ** ADAM_STEP
    n_grads = 2 if in_halves else 1

    def body(core_ref, *refs):
        refs = refs[len(deps):]
        w_ref, m_ref, v_ref = refs[n_grads : n_grads + 3]
        go_ref, d_ref, mo_ref, vo_ref, done_ref = refs[-5:]
        done_ref[...] = jnp.zeros_like(done_ref)
        if in_halves:
            in_my_half = pl.program_id(0) // half_tiles == core_ref[0]
            g = jnp.where(in_my_half, refs[0][...], refs[1][...])
        else:
            g = refs[0][...]
        m_new = ADAM_B1 * m_ref[...] + (1.0 - ADAM_B1) * g
        v_new = ADAM_B2 * v_ref[...] + (1.0 - ADAM_B2) * (g * g)
        go_ref[...] = g
        mo_ref[...] = m_new
        vo_ref[...] = v_new
        d_ref[...] = -ADAM_LR * ((m_new / bias1) / (jnp.sqrt(v_new / bias2) + ADAM_EPS) + ADAM_WD * w_ref[...])

    def half_spec(mine):
        def index(t, core):
            first = (core[0] if mine else 1 - core[0]) * half_tiles
            return (jnp.clip(t - first, 0, half_tiles - 1), 0)

        return pl.BlockSpec((tr, c), index)

    layer_tile = pl.BlockSpec((None, tr, c), lambda t, core: (layer, t, 0))
    if in_halves:
        grad_specs, grads = [half_spec(True), half_spec(False)], list(grad)
    else:
        grad_specs, grads = [pl.BlockSpec((tr, c), lambda t, core: (t, 0))], [grad]
    in_specs = [ANY_SPEC] * len(deps) + grad_specs + [layer_tile] * 3
    args = list(deps) + grads + [w, m, v]
    aliases = {}
    if carried is not None:
        aliases = {1 + len(args) + n: n for n in range(4)}
        in_specs += [HBM_SPEC] * 4
        args += list(carried)
    *outs, done = pl.pallas_call(
        body,
        name=name,
        grid_spec=pltpu.PrefetchScalarGridSpec(
            num_scalar_prefetch=1,
            grid=(r // tr,),
            in_specs=in_specs,
            out_specs=[layer_tile] * 4 + [pl.BlockSpec((8, LANES), lambda t, core: (0, 0))],
        ),
        out_shape=[_shape((n_layers, r, c), F32)] * 4 + [_shape((8, LANES), F32)],
        input_output_aliases=aliases,
        compiler_params=_compute_params("arbitrary"),
    )(_core_index(), *args)
    return outs, done


def _place():
    x, y, c = (lax.axis_index(a) for a in MESH_AXES)
    other_chips = [(1 - x, y), (x, 1 - y), (1 - x, 1 - y)]
    return x, y, c, other_chips


def _chip_index(x, y):
    return 2 * x + y


def _gather_small(conv_rows, pool_rows, *, name):
    blocks = (conv_rows, pool_rows)
    n = len(blocks)

    def body(*refs):
        ins, outs = refs[:n], refs[n : 2 * n]
        send_sems, recv_sems = refs[2 * n :]
        x, y, c, other_chips = _place()
        mine = _chip_index(x, y)
        peers = [(x, y, 1 - c)] + [(px, py, c) for px, py in other_chips]
        _handshake(peers)
        sends = []
        for a in range(n):
            for j, peer in enumerate(peers):
                cp = pltpu.make_async_remote_copy(
                    src_ref=ins[a], dst_ref=outs[a].at[mine], send_sem=send_sems.at[a, j], recv_sem=recv_sems.at[a, j],
                    device_id=peer, device_id_type=MESH_ID,
                )
                cp.start()
                sends.append(cp)
        for a in range(n):
            for j, (px, py, _) in enumerate(peers):
                pltpu.make_async_remote_copy(
                    src_ref=ins[a], dst_ref=outs[a].at[_chip_index(px, py)], send_sem=send_sems.at[a, j],
                    recv_sem=recv_sems.at[a, j], device_id=peers[j], device_id_type=MESH_ID,
                ).wait_recv()
        for cp in sends:
            cp.wait_send()

    return _sequencer_call(
        body,
        [_shape((N_CHIPS,) + blk.shape, F32) for blk in blocks],
        [pltpu.SemaphoreType.DMA((n, 4)), pltpu.SemaphoreType.DMA((n, 4))],
        GATHER_SMALL_COLLECTIVE_ID,
        name,
    )(*blocks)


def _allreduce_small(vec, *, name):
    rows, n = vec.shape
    n_dev = 8

    def body(v_ref, o_ref, slots, send_sems, recv_sems):
        x, y, c, _ = _place()
        me = 4 * x + 2 * y + c
        slots[me] = v_ref[...]
        sends = []
        for mask in range(1, n_dev):
            fx, fy, fc = (mask >> 2) & 1, (mask >> 1) & 1, mask & 1
            peer = (x ^ fx, y ^ fy, c ^ fc)
            cp = pltpu.make_async_remote_copy(
                src_ref=v_ref, dst_ref=slots.at[me], send_sem=send_sems.at[mask - 1], recv_sem=recv_sems.at[mask - 1],
                device_id=peer, device_id_type=MESH_ID,
            )
            cp.start()
            sends.append(cp)
        for mask in range(1, n_dev):
            fx, fy, fc = (mask >> 2) & 1, (mask >> 1) & 1, mask & 1
            peer = (x ^ fx, y ^ fy, c ^ fc)
            pltpu.make_async_remote_copy(
                src_ref=v_ref, dst_ref=slots.at[4 * peer[0] + 2 * peer[1] + peer[2]], send_sem=send_sems.at[mask - 1],
                recv_sem=recv_sems.at[mask - 1], device_id=peer, device_id_type=MESH_ID,
            ).wait_recv()
        total = slots[0]
        for dev in range(1, n_dev):
            total = total + slots[dev]
        o_ref[...] = total
        for cp in sends:
            cp.wait_send()

    return pl.pallas_call(
        body,
        name=name,
        in_specs=[VMEM_SPEC],
        out_specs=VMEM_SPEC,
        out_shape=_shape((rows, n), F32),
        scratch_shapes=[
            pltpu.VMEM((n_dev, rows, n), F32),
            pltpu.SemaphoreType.DMA((n_dev - 1,)),
            pltpu.SemaphoreType.DMA((n_dev - 1,)),
        ],
    )(vec)


def _handshake(peers):
    barrier = pltpu.get_barrier_semaphore()
    for peer in peers:
        pl.semaphore_signal(barrier, inc=1, device_id=peer, device_id_type=MESH_ID)
    pl.semaphore_wait(barrier, len(peers))


def _sequencer_call(body, out_type, scratch_types, collective_id, name):
    return pl.kernel(
        body,
        name=name,
        out_type=out_type,
        mesh=plsc.ScalarSubcoreMesh(axis_name="sequencer", num_cores=1),
        scratch_types=scratch_types,
        compiler_params=pltpu.CompilerParams(collective_id=collective_id),
    )


GATHER_COLLECTIVE_ID = 1
EXCHANGE_COLLECTIVE_ID = 2
SCATTER_COLLECTIVE_ID = 3
SHARE_COLLECTIVE_ID = 4
GATHER_SMALL_COLLECTIVE_ID = 5


def _gather_weights(stacked, layer, *, name):
    n = len(stacked)

    def body(*refs):
        ins, outs = refs[:n], refs[n : 2 * n]
        own_sems, send_sems, recv_sems, pass_send_sems, pass_recv_sems = refs[2 * n :]
        x, y, c, other_chips = _place()
        mine = _chip_index(x, y)
        sibling = (x, y, 1 - c)
        _handshake([sibling] + [(px, py, c) for px, py in other_chips])
        pending = []

        def send_my_half(a):
            hr = ins[a].shape[1] // 2
            my_half = pl.ds(c * hr, hr)
            cp = pltpu.make_async_remote_copy(
                src_ref=ins[a].at[layer], dst_ref=outs[a].at[mine], send_sem=own_sems.at[0, a],
                recv_sem=own_sems.at[1, a], device_id=sibling, device_id_type=MESH_ID,
            )
            cp.start()
            pending.append(cp.wait)
            for j, (px, py) in enumerate(other_chips):
                cp = pltpu.make_async_remote_copy(
                    src_ref=ins[a].at[layer, my_half], dst_ref=outs[a].at[mine, my_half], send_sem=send_sems.at[a, j],
                    recv_sem=recv_sems.at[a, j], device_id=(px, py, c), device_id_type=MESH_ID,
                )
                cp.start()
                pending.append(cp.wait_send)

        for a in range(min(2, n)):
            send_my_half(a)
        for a in range(n):
            hr = ins[a].shape[1] // 2
            my_half = pl.ds(c * hr, hr)
            for j, (px, py) in enumerate(other_chips):
                landed = outs[a].at[_chip_index(px, py), my_half]
                pltpu.make_async_remote_copy(
                    src_ref=landed, dst_ref=landed, send_sem=send_sems.at[a, j], recv_sem=recv_sems.at[a, j],
                    device_id=(px, py, c), device_id_type=MESH_ID,
                ).wait_recv()
                cp = pltpu.make_async_remote_copy(
                    src_ref=landed, dst_ref=landed, send_sem=pass_send_sems.at[a, j], recv_sem=pass_recv_sems.at[a, j],
                    device_id=sibling, device_id_type=MESH_ID,
                )
                cp.start()
                pending.append(cp.wait_send)
            if a + 2 < n:
                send_my_half(a + 2)
        for a in range(n):
            hr = ins[a].shape[1] // 2
            sibling_half = pl.ds((1 - c) * hr, hr)
            for j, (px, py) in enumerate(other_chips):
                passed = outs[a].at[_chip_index(px, py), sibling_half]
                pltpu.make_async_remote_copy(
                    src_ref=passed, dst_ref=passed, send_sem=pass_send_sems.at[a, j], recv_sem=pass_recv_sems.at[a, j],
                    device_id=sibling, device_id_type=MESH_ID,
                ).wait_recv()
        for wait in pending:
            wait()

    return _sequencer_call(
        body,
        [_shape((N_CHIPS,) + a.shape[1:], BF16) for a in stacked],
        [pltpu.SemaphoreType.DMA((2, n))] + [pltpu.SemaphoreType.DMA((n, 3))] * 4,
        GATHER_COLLECTIVE_ID,
        name,
    )(*stacked)


def _exchange_halves(grads, *, name):
    n = len(grads)

    def body(*refs):
        ins, outs = refs[:n], refs[n : 2 * n]
        send_sems, recv_sems = refs[2 * n :]
        x, y, c, _ = _place()
        sibling = (x, y, 1 - c)
        _handshake([sibling])
        copies = []
        for a in range(n):
            hr = ins[a].shape[1] // 2
            cp = pltpu.make_async_remote_copy(
                src_ref=ins[a].at[:, pl.ds((1 - c) * hr, hr), :], dst_ref=outs[a], send_sem=send_sems.at[a],
                recv_sem=recv_sems.at[a], device_id=sibling, device_id_type=MESH_ID,
            )
            cp.start()
            copies.append(cp)
        for cp in copies:
            cp.wait()

    return _sequencer_call(
        body,
        [_shape((g.shape[0], g.shape[1] // 2, g.shape[2]), BF16) for g in grads],
        [pltpu.SemaphoreType.DMA((n,)), pltpu.SemaphoreType.DMA((n,))],
        EXCHANGE_COLLECTIVE_ID,
        name,
    )(*grads)


def _scatter_partials(partials, *, name):
    n = len(partials)

    def body(*refs):
        ins, outs = refs[:n], refs[n : 2 * n]
        send_sems, recv_sems = refs[2 * n :]
        x, y, c, other_chips = _place()
        mine = _chip_index(x, y)
        _handshake([(px, py, c) for px, py in other_chips])
        pending = []
        for a in range(n):
            for j, (px, py) in enumerate(other_chips):
                cp = pltpu.make_async_remote_copy(
                    src_ref=ins[a].at[_chip_index(px, py)], dst_ref=outs[a].at[mine], send_sem=send_sems.at[a, j],
                    recv_sem=recv_sems.at[a, j], device_id=(px, py, c), device_id_type=MESH_ID,
                )
                cp.start()
                pending.append(cp.wait_send)
        for a in range(n):
            for j, (px, py) in enumerate(other_chips):
                landed = outs[a].at[_chip_index(px, py)]
                pltpu.make_async_remote_copy(
                    src_ref=landed, dst_ref=landed, send_sem=send_sems.at[a, j], recv_sem=recv_sems.at[a, j],
                    device_id=(px, py, c), device_id_type=MESH_ID,
                ).wait_recv()
        for wait in pending:
            wait()

    return _sequencer_call(
        body,
        [_shape(p.shape, BF16) for p in partials],
        [pltpu.SemaphoreType.DMA((n, 3)), pltpu.SemaphoreType.DMA((n, 3))],
        SCATTER_COLLECTIVE_ID,
        name,
    )(*partials)


def _share_halves(halves, *, name):
    n = len(halves)

    def body(*refs):
        ins, outs = refs[:n], refs[n : 2 * n]
        send_sems, recv_sems = refs[2 * n :]
        x, y, c, _ = _place()
        sibling = (x, y, 1 - c)
        _handshake([sibling])
        copies = []
        for a in range(n):
            cp = pltpu.make_async_remote_copy(
                src_ref=ins[a], dst_ref=outs[a], send_sem=send_sems.at[a], recv_sem=recv_sems.at[a],
                device_id=sibling, device_id_type=MESH_ID,
            )
            cp.start()
            copies.append(cp)
        for cp in copies:
            cp.wait()

    return _sequencer_call(
        body,
        [_shape(h.shape, F32) for h in halves],
        [pltpu.SemaphoreType.DMA((n,)), pltpu.SemaphoreType.DMA((n,))],
        SHARE_COLLECTIVE_ID,
        name,
    )(*halves)


class _ReduceToOwner:
    def __init__(self, grads, tag):
        self.grads, self.tag = grads, tag
        self.received = _exchange_halves(grads, name=f"exchange_halves_{tag}")

    def add_sibling(self, after):
        partials = [
            _add_sibling_half(g, r, name=f"add_sibling_{self.tag}_{a}", deps=after)
            for a, (g, r) in enumerate(zip(self.grads, self.received))
        ]
        self.partials = partials
        self.slots = _scatter_partials(partials, name=f"scatter_partials_{self.tag}")
        return partials

    def sum_chips(self, after):
        halves = [
            _sum_chip_partials(p, s, name=f"sum_partials_{self.tag}_{a}", deps=after)
            for a, (p, s) in enumerate(zip(self.partials, self.slots))
        ]
        self.halves = halves
        self.theirs = _share_halves(halves, name=f"share_halves_{self.tag}")
        return halves

    def totals(self):
        return list(zip(self.halves, self.theirs))


def kernel(x, w_in, conv_w, pool_w, pool_scale, w_out, norm_mix, norm_mlp, w_up, w_down, norm_final, loss_target, m_w_in, m_conv_w, m_pool_w, m_pool_scale, m_w_out, m_norm_mix, m_norm_mlp, m_w_up, m_w_down, m_norm_final, v_w_in, v_conv_w, v_pool_w, v_pool_scale, v_w_out, v_norm_mix, v_norm_mlp, v_w_up, v_w_down, v_norm_final):
    n_layers, d, _ = w_in.shape
    s = x.shape[1]
    dc = conv_w.shape[2] * N_CHIPS
    n_groups, cg_rows, cg = pool_w.shape[1:]
    dp = n_groups * cg
    x0 = x.reshape(s, d)
    target = loss_target.reshape(s, d)

    big = [w_in, w_out, w_up, w_down]
    big_bf16 = [
        _cast_bf16(w.reshape(-1, w.shape[2]), name=f"cast_{t}").reshape(w.shape)
        for t, w in zip(("w_in", "w_out", "w_up", "w_down"), big)
    ]

    conv_rows = jnp.pad(conv_w, ((0, 0), (0, 8 - CONV_TAPS), (0, 0))).reshape(n_layers * 8, -1)
    pool_rows = pool_w.reshape(n_layers * n_groups * cg_rows, cg)
    conv_all, pool_all = _gather_small(conv_rows, pool_rows, name="gather_small")
    conv_full = conv_all.reshape(N_CHIPS, n_layers, 8, -1).transpose(1, 2, 0, 3).reshape(n_layers, 8, dc)
    pool_full = (
        pool_all.reshape(N_CHIPS, n_layers, n_groups, cg_rows, cg).transpose(1, 2, 0, 3, 4)
        .reshape(n_layers, n_groups, cg, cg).astype(BF16)
    )

    saved = []
    xl = x0
    for l in range(n_layers):
        if l == 0:
            win_g, wout_g, wup_g, wdown_g = [
                _gather_weights([w], l, name=f"gather_weights_l{l}_{t}")[0] for t, w in enumerate(big_bf16)
            ]
        else:
            win_g, wout_g, wup_g, wdown_g = _gather_weights(big_bf16, l, name=f"gather_weights_l{l}")
        gain_mix = norm_mix[l].reshape(1, d)
        gain_mlp = norm_mlp[l].reshape(1, d)
        scale = pool_scale[l].reshape(1, dp)
        h1 = _rmsnorm(xl, gain_mix, name=f"norm_mix_l{l}")
        proj = _matmul_cols(h1, win_g, relu2=False, name=f"in_proj_l{l}")
        y = _mixer_fwd(proj, conv_full[l], pool_full[l], scale, name=f"mixer_fwd_l{l}")
        x_mid = _matmul_residual(y, wout_g.reshape(-1, d), xl, name=f"out_proj_l{l}")
        h2 = _rmsnorm(x_mid, gain_mlp, name=f"norm_mlp_l{l}")
        u, u2 = _matmul_cols(h2, wup_g, relu2=True, name=f"up_proj_l{l}")
        x_next = _matmul_residual(u2, wdown_g.reshape(-1, d), x_mid, name=f"down_proj_l{l}")
        saved.append((xl, h1, proj, y, x_mid, h2, u, u2, win_g, wout_g, wup_g, wdown_g, gain_mix, gain_mlp, scale))
        xl = x_next

    loss_part, dx, dx_bf16, d_norm_final = _loss_and_grad(xl, norm_final.reshape(1, d), target, name="loss_head")
    loss = lax.psum(loss_part[0, 0], MESH_AXES)

    small_grads = [None] * n_layers
    carried = [None] * 5
    pool_params = tuple(p.reshape(n_layers, n_groups * cg_rows, cg) for p in (pool_w, m_pool_w, v_pool_w))
    params = [(w_in, m_w_in, v_w_in), (w_out, m_w_out, v_w_out), (w_up, m_w_up, v_w_up), (w_down, m_w_down, v_w_down),
              pool_params]
    DOWN, UP_OUT, IN_POOL = (3,), (2, 1), (0, 4)

    def update(reduce, which, layer, after):
        dones = []
        for a, total in zip(which, reduce.totals()):
            w, m, v = params[a]
            carried[a], done = _adamw(total, w, m, v, layer, carried[a], name=f"adamw_{a}_l{layer}", deps=after)
            dones.append(done)
        return dones

    up_out_above = in_pool_above = None
    for l in reversed(range(n_layers)):
        xl, h1, proj, y, x_mid, h2, u, u2, win_g, wout_g, wup_g, wdown_g, gain_mix, gain_mlp, scale = saved[l]
        above = up_out_above is not None
        deps = in_pool_above.add_sibling([dx_bf16]) if above else []
        g_down = _matmul_tn(u2, dx_bf16, n_shards=N_CHIPS, shard_cols=False, name=f"grad_w_down_l{l}", deps=deps)
        down = _ReduceToOwner([g_down], f"down_l{l}")
        deps = up_out_above.sum_chips([g_down]) if above else [g_down]
        da = _matmul_nt(dx_bf16, wdown_g.reshape(-1, d), u=u, name=f"grad_act_l{l}", deps=deps)
        g_up = _matmul_tn(
            h2, da, n_shards=N_CHIPS, shard_cols=True, name=f"grad_w_up_l{l}", deps=down.add_sibling([da])
        )
        deps = in_pool_above.sum_chips([g_up]) if above else [g_up]
        dx_mid, dx_mid_bf16, d_gain_mlp = _matmul_nt_norm_bwd(
            da, wup_g, x_mid, gain_mlp, dx, name=f"grad_mid_l{l}", deps=deps
        )
        g_out = _matmul_tn(
            y, dx_mid_bf16, n_shards=N_CHIPS, shard_cols=False, name=f"grad_w_out_l{l}",
            deps=down.sum_chips([dx_mid_bf16]),
        )
        up_out = _ReduceToOwner([g_up, g_out], f"up_out_l{l}")
        deps = update(up_out_above, UP_OUT, l + 1, [g_out]) if above else [g_out]
        dy = _matmul_nt(dx_mid_bf16, wout_g.reshape(-1, d), name=f"grad_mixed_l{l}", deps=deps)
        dproj, d_conv, d_pool, d_scale = _mixer_bwd(dy, proj, conv_full[l], pool_full[l], scale, name=f"mixer_bwd_l{l}")
        g_in = _matmul_tn(
            h1, dproj, n_shards=N_CHIPS, shard_cols=True, name=f"grad_w_in_l{l}", deps=up_out.add_sibling([dproj])
        )
        g_pool = (
            d_pool.reshape(n_groups, N_CHIPS, cg_rows, cg).transpose(1, 0, 2, 3)
            .reshape(N_CHIPS, n_groups * cg_rows, cg).astype(BF16)
        )
        in_pool = _ReduceToOwner([g_in, g_pool], f"in_pool_l{l}")
        deps = update(down, DOWN, l, [g_in])
        if above:
            deps = deps + update(in_pool_above, IN_POOL, l + 1, [g_in])
        dx, dx_bf16, d_gain_mix = _matmul_nt_norm_bwd(
            dproj, win_g, xl, gain_mix, dx_mid, name=f"grad_x_l{l}", deps=deps
        )
        small_grads[l] = jnp.concatenate(
            [d_conv[:CONV_TAPS].reshape(-1), d_scale.reshape(-1), d_gain_mix.reshape(-1), d_gain_mlp.reshape(-1)]
        )
        up_out_above, in_pool_above = up_out, in_pool
    deps = in_pool_above.add_sibling([dx_bf16])
    deps = up_out_above.sum_chips(deps)
    deps = in_pool_above.sum_chips(deps)
    deps = update(up_out_above, UP_OUT, 0, deps)
    update(in_pool_above, IN_POOL, 0, deps)

    vec = jnp.concatenate(small_grads + [d_norm_final.reshape(-1)])
    vec = _allreduce_small(vec.reshape(8, -1), name="allreduce_small").reshape(-1)
    per_layer = vec[: n_layers * (CONV_TAPS * dc + dp + 2 * d)].reshape(n_layers, -1)
    chip = _chip_index(lax.axis_index("x"), lax.axis_index("y"))
    dcs = dc // N_CHIPS
    g_conv = lax.dynamic_slice_in_dim(per_layer[:, : CONV_TAPS * dc].reshape(n_layers, CONV_TAPS, dc), chip * dcs, dcs, axis=2)
    g_scale = per_layer[:, CONV_TAPS * dc : CONV_TAPS * dc + dp]
    g_mix = per_layer[:, CONV_TAPS * dc + dp : CONV_TAPS * dc + dp + d]
    g_mlp = per_layer[:, CONV_TAPS * dc + dp + d :]
    g_final = vec[n_layers * (CONV_TAPS * dc + dp + 2 * d) :]

    def small_adamw(g, w, m, v, tag):
        flat = lambda t: t.reshape(1, -1, t.shape[-1])
        out, _ = _adamw(flat(g)[0], flat(w), flat(m), flat(v), 0, None, name=f"adamw_{tag}")
        return [o.reshape(w.shape) for o in out]

    o_conv = small_adamw(g_conv, conv_w, m_conv_w, v_conv_w, "conv_w")
    o_scale = small_adamw(g_scale, pool_scale, m_pool_scale, v_pool_scale, "pool_scale")
    o_mix = small_adamw(g_mix, norm_mix, m_norm_mix, v_norm_mix, "norm_mix")
    o_mlp = small_adamw(g_mlp, norm_mlp, m_norm_mlp, v_norm_mlp, "norm_mlp")
    o_final = small_adamw(g_final, norm_final, m_norm_final, v_norm_final, "norm_final")
    o_in, o_out, o_up, o_down, o_pool = carried
    o_pool = [o.reshape(pool_w.shape) for o in o_pool]

    ordered = [o_in, o_conv, o_pool, o_scale, o_out, o_mix, o_mlp, o_up, o_down, o_final]
    return (loss, dx.reshape(x.shape), *[o[0] for o in ordered], *[o[1] for o in ordered], *[o[2] for o in ordered],
            *[o[3] for o in ordered])
```

```python
import functools

import jax
import jax.numpy as jnp
from jax import lax
from jax.experimental import pallas as pl
from jax.experimental.pallas import tpu as pltpu
from jax.experimental.pallas import tpu_sc as plsc

F32 = jnp.float32
BF16 = jnp.bfloat16

EPS = 1e-6
POOL_WINDOWS = (2, 4, 8, 16)
CONV_TAPS = 3
HALO = 16

ADAM_LR = 0.001
ADAM_B1 = 0.9
ADAM_B2 = 0.999
ADAM_EPS = 1e-08
ADAM_WD = 0.01
ADAM_STEP = 10

N_CHIPS = 4
MESH_AXES = ("x", "y", "c")
V7X_VMEM_LIMIT_BYTES = 56 * 1024 * 1024
SUBLANES_BF16 = 16
LANES = 128

HBM_SPEC = pl.BlockSpec(memory_space=pltpu.HBM)
VMEM_SPEC = pl.BlockSpec(memory_space=pltpu.VMEM)
MESH_ID = pl.DeviceIdType.MESH


def _tile(dim, target, align):
    if dim <= target:
        return dim
    t = (target // align) * align
    while dim % t:
        t -= align
    assert t > 0, (dim, target, align)
    return t


def _compute_params(*semantics):
    return pltpu.CompilerParams(dimension_semantics=semantics, vmem_limit_bytes=V7X_VMEM_LIMIT_BYTES)


def _shape(shape, dtype):
    return jax.ShapeDtypeStruct(shape, dtype)


ANY_SPEC = pl.BlockSpec(memory_space=pl.ANY)


def _behind(body, deps):
    return lambda *refs: body(*refs[len(deps):])


def _rmsnorm_bwd(dh, x, gain, dres):
    r = lax.rsqrt(jnp.mean(x * x, axis=-1, keepdims=True) + EPS)
    xn = x * r
    dgain = jnp.sum(dh * xn, axis=0, keepdims=True)
    dxn = dh * gain
    dx = r * (dxn - xn * jnp.mean(dxn * xn, axis=-1, keepdims=True))
    if dres is not None:
        dx = dx + dres
    return dx, dgain


MXU_ROWS = 512


def _rmsnorm(x, gain, *, name):
    s, d = x.shape
    tm = _tile(s, 256, SUBLANES_BF16)

    def body(x_ref, g_ref, h_ref):
        xf = x_ref[...]
        r = lax.rsqrt(jnp.mean(xf * xf, axis=-1, keepdims=True) + EPS)
        h_ref[...] = ((xf * r) * g_ref[...]).astype(BF16)

    row_tile = pl.BlockSpec((tm, d), lambda i: (i, 0))
    return pl.pallas_call(
        body, name=name, grid=(s // tm,), in_specs=[row_tile, pl.BlockSpec((1, d), lambda i: (0, 0))],
        out_specs=row_tile, out_shape=_shape((s, d), BF16), compiler_params=_compute_params("parallel"),
    )(x, gain)


def _matmul_cols(h, w, *, relu2, name):
    s, k = h.shape
    nsh, _, c = w.shape
    tm = _tile(s, 2048, MXU_ROWS)
    tn = _tile(c, 1024, LANES)
    rows = min(MXU_ROWS, tm)
    per_shard = c // tn

    def body(h_ref, w_ref, *out_refs):
        def step(t, carry):
            rs = pl.ds(pl.multiple_of(t * rows, rows), rows)
            acc = jnp.dot(h_ref[rs, :], w_ref[...], preferred_element_type=F32)
            if relu2:
                u = jnp.maximum(acc, 0.0)
                out_refs[0][rs, :] = u.astype(BF16)
                out_refs[1][rs, :] = (u * u).astype(BF16)
            else:
                out_refs[0][rs, :] = acc.astype(BF16)
            return carry

        lax.fori_loop(0, tm // rows, step, 0)

    tile_out = pl.BlockSpec((tm, tn), lambda i, j: (i, j))
    n = nsh * c
    outs = [_shape((s, n), BF16)] * (2 if relu2 else 1)
    result = pl.pallas_call(
        body,
        name=name,
        grid=(s // tm, n // tn),
        in_specs=[
            pl.BlockSpec((tm, k), lambda i, j: (i, 0)),
            pl.BlockSpec((None, k, tn), lambda i, j: (j // per_shard, 0, j % per_shard)),
        ],
        out_specs=[tile_out] * len(outs),
        out_shape=outs,
        compiler_params=_compute_params("parallel", "arbitrary"),
    )(h, w)
    return result if relu2 else result[0]


def _matmul_residual(a, w, res, *, name):
    s, k = a.shape
    _, n = w.shape
    tm = _tile(s, 1024, MXU_ROWS)
    tn = _tile(n, 1024, LANES)
    tk = _tile(k, 2048, LANES)
    rows = min(MXU_ROWS, tm)

    def body(a_ref, w_ref, r_ref, o_ref):
        kk = pl.program_id(2)

        def step(t, carry):
            rs = pl.ds(pl.multiple_of(t * rows, rows), rows)
            prod = jnp.dot(a_ref[rs, :], w_ref[...], preferred_element_type=F32)

            @pl.when(kk == 0)
            def _():
                o_ref[rs, :] = r_ref[rs, :] + prod

            @pl.when(kk > 0)
            def _():
                o_ref[rs, :] += prod

            return carry

        lax.fori_loop(0, tm // rows, step, 0)

    return pl.pallas_call(
        body,
        name=name,
        grid=(s // tm, n // tn, k // tk),
        in_specs=[
            pl.BlockSpec((tm, tk), lambda i, j, kk: (i, kk)),
            pl.BlockSpec((tk, tn), lambda i, j, kk: (kk, j)),
            pl.BlockSpec((tm, tn), lambda i, j, kk: (i, j)),
        ],
        out_specs=pl.BlockSpec((tm, tn), lambda i, j, kk: (i, j)),
        out_shape=_shape((s, n), F32),
        compiler_params=_compute_params("parallel", "parallel", "arbitrary"),
    )(a, w, res)


def _matmul_tn(a, g, *, n_shards, shard_cols, name, deps=()):
    m, kd = a.shape
    _, n = g.shape
    r, c = (kd, n // n_shards) if shard_cols else (kd // n_shards, n)
    tr = _tile(kd, 2048, LANES)
    tn = _tile(c, 1024, LANES)
    tm = _tile(m, 2048, SUBLANES_BF16)
    nm = m // tm
    cols_per_shard = c // tn

    out_rows = min(MXU_ROWS, tr)

    def body(a_ref, g_ref, o_ref, acc_ref):
        mm = pl.program_id(2)
        for t in range(tr // out_rows):
            rs = slice(t * out_rows, (t + 1) * out_rows)
            prod = lax.dot_general(
                a_ref[:, rs], g_ref[...], (((0,), (0,)), ((), ())), preferred_element_type=F32
            )
            if nm == 1:
                o_ref[rs, :] = prod.astype(BF16)
                continue

            @pl.when(mm == 0)
            def _():
                acc_ref[rs, :] = prod

            @pl.when((mm > 0) & (mm < nm - 1))
            def _():
                acc_ref[rs, :] += prod

            @pl.when(mm == nm - 1)
            def _():
                o_ref[rs, :] = (acc_ref[rs, :] + prod).astype(BF16)

    if shard_cols:
        out_spec = pl.BlockSpec((None, tr, tn), lambda i, j, mm: (j // cols_per_shard, i, j % cols_per_shard))
        out_shape = _shape((n_shards, kd, c), BF16)
    else:
        out_spec = pl.BlockSpec((tr, tn), lambda i, j, mm: (i, j))
        out_shape = _shape((kd, n), BF16)
    out = pl.pallas_call(
        _behind(body, deps),
        name=name,
        grid=(kd // tr, n // tn, nm),
        in_specs=[ANY_SPEC] * len(deps) + [
            pl.BlockSpec((tm, tr), lambda i, j, mm: (mm, i)),
            pl.BlockSpec((tm, tn), lambda i, j, mm: (mm, j)),
        ],
        out_specs=out_spec,
        out_shape=out_shape,
        scratch_shapes=[pltpu.VMEM((tr, tn), F32)],
        compiler_params=_compute_params("parallel", "parallel", "arbitrary"),
    )(*deps, a, g)
    return out.reshape(n_shards, r, c)


def _matmul_nt(g, w, *, u=None, name, deps=()):
    m, n = g.shape
    kd, _ = w.shape
    tm = _tile(m, 2048, MXU_ROWS)
    tj = _tile(kd, 1024, LANES)
    rows = min(MXU_ROWS, tm)

    def body(*refs):
        if u is None:
            g_ref, w_ref, o_ref = refs
        else:
            g_ref, w_ref, u_ref, o_ref = refs

        def step(t, carry):
            rs = pl.ds(pl.multiple_of(t * rows, rows), rows)
            prod = lax.dot_general(g_ref[rs, :], w_ref[...], (((1,), (1,)), ((), ())), preferred_element_type=F32)
            if u is None:
                o_ref[rs, :] = prod
            else:
                o_ref[rs, :] = (prod * (2.0 * u_ref[rs, :].astype(F32))).astype(BF16)
            return carry

        lax.fori_loop(0, tm // rows, step, 0)

    tile_out = pl.BlockSpec((tm, tj), lambda i, j: (i, j))
    in_specs = [pl.BlockSpec((tm, n), lambda i, j: (i, 0)), pl.BlockSpec((tj, n), lambda i, j: (j, 0))]
    args = [g, w]
    if u is not None:
        in_specs.append(tile_out)
        args.append(u)
    return pl.pallas_call(
        _behind(body, deps),
        name=name,
        grid=(m // tm, kd // tj),
        in_specs=[ANY_SPEC] * len(deps) + in_specs,
        out_specs=tile_out,
        out_shape=_shape((m, kd), F32 if u is None else BF16),
        compiler_params=_compute_params("parallel", "arbitrary"),
    )(*deps, *args)


EPILOGUE_ROWS = 128


def _matmul_nt_norm_bwd(g, w, x, gain, dres, *, name, deps=()):
    s, n = g.shape
    nsh, d, c = w.shape
    tm = _tile(s, 1024, MXU_ROWS)
    tk = _tile(c, 1024, LANES)
    per_shard = c // tk
    nk = n // tk
    n_blocks = s // tm
    mxu_rows = min(MXU_ROWS, tm)
    rows = min(EPILOGUE_ROWS, tm)
    piece = tm // nk
    assert piece * nk == tm and piece % 8 == 0

    def body(g_ref, w_ref, gain_ref, x_hbm, dres_hbm, dx_hbm, dxb_hbm, dg_ref, acc_ref, x_buf, dres_buf, dx_buf,
             dxb_buf, in_sems, out_sems):
        i = pl.program_id(0)
        kk = pl.program_id(1)

        @pl.when((kk == 0) & (i == 0))
        def _():
            dg_ref[...] = jnp.zeros_like(dg_ref)

        block_rows = pl.ds(pl.multiple_of(i * tm, tm), tm)
        pieces = pl.ds(pl.multiple_of(kk * piece, piece), piece)
        piece_rows = pl.ds(pl.multiple_of(i * tm + kk * piece, piece), piece)
        pltpu.make_async_copy(x_hbm.at[piece_rows, :], x_buf.at[pieces, :], in_sems.at[0]).start()
        pltpu.make_async_copy(dres_hbm.at[piece_rows, :], dres_buf.at[pieces, :], in_sems.at[1]).start()

        def accumulate(t, carry):
            rs = pl.ds(pl.multiple_of(t * mxu_rows, mxu_rows), mxu_rows)
            prod = lax.dot_general(g_ref[rs, :], w_ref[...], (((1,), (1,)), ((), ())), preferred_element_type=F32)

            @pl.when(kk == 0)
            def _():
                acc_ref[rs, :] = prod

            @pl.when(kk > 0)
            def _():
                acc_ref[rs, :] += prod

            return carry

        lax.fori_loop(0, tm // mxu_rows, accumulate, 0)

        def stores():
            return (
                pltpu.make_async_copy(dx_buf, dx_hbm.at[block_rows, :], out_sems.at[0]),
                pltpu.make_async_copy(dxb_buf, dxb_hbm.at[block_rows, :], out_sems.at[1]),
            )

        @pl.when(kk == nk - 1)
        def _():
            pltpu.make_async_copy(x_hbm.at[block_rows, :], x_buf, in_sems.at[0]).wait()
            pltpu.make_async_copy(dres_hbm.at[block_rows, :], dres_buf, in_sems.at[1]).wait()

            @pl.when(i > 0)
            def _():
                for cp in stores():
                    cp.wait()

            def step(j, carry):
                rs = pl.ds(pl.multiple_of(j * rows, rows), rows)
                dx, dgain = _rmsnorm_bwd(acc_ref[rs, :], x_buf[rs, :], gain_ref[...], dres_buf[rs, :])
                dx_buf[rs, :] = dx
                dxb_buf[rs, :] = dx.astype(BF16)
                dg_ref[...] += dgain
                return carry

            lax.fori_loop(0, tm // rows, step, 0)
            for cp in stores():
                cp.start()

            @pl.when(i == n_blocks - 1)
            def _():
                for cp in stores():
                    cp.wait()

    vec = pl.BlockSpec((1, d), lambda i, kk: (0, 0))
    return pl.pallas_call(
        _behind(body, deps),
        name=name,
        grid=(n_blocks, nk),
        in_specs=[ANY_SPEC] * len(deps) + [
            pl.BlockSpec((tm, tk), lambda i, kk: (i, kk)),
            pl.BlockSpec((None, d, tk), lambda i, kk: (kk // per_shard, 0, kk % per_shard)),
            vec,
            ANY_SPEC,
            ANY_SPEC,
        ],
        out_specs=[ANY_SPEC, ANY_SPEC, vec],
        out_shape=[_shape((s, d), F32), _shape((s, d), BF16), _shape((1, d), F32)],
        scratch_shapes=[
            pltpu.VMEM((tm, d), F32),
            pltpu.VMEM((tm, d), F32),
            pltpu.VMEM((tm, d), F32),
            pltpu.VMEM((tm, d), F32),
            pltpu.VMEM((tm, d), BF16),
            pltpu.SemaphoreType.DMA((2,)),
            pltpu.SemaphoreType.DMA((2,)),
        ],
        compiler_params=_compute_params("arbitrary", "arbitrary"),
    )(*deps, g, w, gain, x, dres)


def _loss_and_grad(x, gain, target, *, name):
    s, d = x.shape
    tm = _tile(s, 256, SUBLANES_BF16)

    def body(x_ref, gain_ref, t_ref, loss_ref, dx_ref, dxb_ref, dg_ref):
        @pl.when(pl.program_id(0) == 0)
        def _():
            loss_ref[...] = jnp.zeros_like(loss_ref)
            dg_ref[...] = jnp.zeros_like(dg_ref)

        xf = x_ref[...]
        r = lax.rsqrt(jnp.mean(xf * xf, axis=-1, keepdims=True) + EPS)
        err = (xf * r) * gain_ref[...] - t_ref[...]
        loss_ref[...] += 0.5 * jnp.sum(jnp.mean(err * err, axis=-1, keepdims=True))
        dx, dgain = _rmsnorm_bwd(err * (1.0 / d), xf, gain_ref[...], None)
        dx_ref[...] = dx
        dxb_ref[...] = dx.astype(BF16)
        dg_ref[...] += dgain

    row_tile = pl.BlockSpec((tm, d), lambda i: (i, 0))
    vec = pl.BlockSpec((1, d), lambda i: (0, 0))
    return pl.pallas_call(
        body,
        name=name,
        grid=(s // tm,),
        in_specs=[row_tile, vec, row_tile],
        out_specs=[pl.BlockSpec((1, LANES), lambda i: (0, 0)), row_tile, row_tile, vec],
        out_shape=[_shape((1, LANES), F32), _shape((s, d), F32), _shape((s, d), BF16), _shape((1, d), F32)],
        compiler_params=_compute_params("arbitrary"),
    )(x, gain, target)


def _trailing_sums(v_ext, window):
    acc, span = v_ext, 1
    while span < window:
        acc = acc + pltpu.roll(acc, span, 0)
        span *= 2
    return acc


def _leading_sums(q_ext, window):
    n = q_ext.shape[0]
    acc, span = q_ext, 1
    while span < window:
        acc = acc + pltpu.roll(acc, n - span, 0)
        span *= 2
    return acc


def _inverse_counts(first_token, rows, window):
    t = first_token + lax.broadcasted_iota(jnp.int32, (rows, 1), 0)
    return 1.0 / jnp.minimum(t + 1, window).astype(F32)


def _mixer_fwd(proj, conv_w, pool_w, pool_scale, *, name):
    s, _ = proj.shape
    dc = conv_w.shape[1]
    n_groups, cg, _ = pool_w.shape
    dp = n_groups * cg
    assert dc == dp and all(w & (w - 1) == 0 and w <= HALO for w in POOL_WINDOWS)
    ts = _tile(s, 256, HALO)
    halo_blocks = ts // HALO

    def body(b_ref, c_ref, xt_ref, v_ref, ch_ref, xth_ref, vh_ref, cw_ref, pw_ref, ps_ref, y_ref):
        i = pl.program_id(0)
        has_past = i > 0
        u_ext = jnp.concatenate(
            [
                jnp.where(has_past, ch_ref[...].astype(F32) * xth_ref[...].astype(F32), 0.0),
                c_ref[...].astype(F32) * xt_ref[...].astype(F32),
            ],
            axis=0,
        )
        conv = (
            cw_ref[2:3, :] * u_ext[HALO:]
            + cw_ref[1:2, :] * pltpu.roll(u_ext, 1, 0)[HALO:]
            + cw_ref[0:1, :] * pltpu.roll(u_ext, 2, 0)[HALO:]
        )
        y_ref[:, 0:dc] = (b_ref[...].astype(F32) * conv).astype(BF16)
        for gi, window in enumerate(POOL_WINDOWS):
            cols = slice(gi * cg, (gi + 1) * cg)
            v_ext = jnp.concatenate(
                [jnp.where(has_past, vh_ref[:, cols].astype(F32), 0.0), v_ref[:, cols].astype(F32)], axis=0
            )
            mean = _trailing_sums(v_ext, window)[HALO:] * _inverse_counts(i * ts, ts, window)
            diff = (mean - v_ext[HALO:]).astype(BF16)
            z = jnp.dot(diff, pw_ref[gi], preferred_element_type=F32)
            y_ref[:, dc + gi * cg : dc + (gi + 1) * cg] = (z * ps_ref[:, cols]).astype(BF16)

    def col(jc):
        return pl.BlockSpec((ts, dc), lambda i: (i, jc))

    def past(jc):
        return pl.BlockSpec((HALO, dc), lambda i: (jnp.maximum(i * halo_blocks - 1, 0), jc))

    return pl.pallas_call(
        body,
        name=name,
        grid=(s // ts,),
        in_specs=[
            col(0), col(1), col(2), col(3), past(1), past(2), past(3),
            pl.BlockSpec((8, dc), lambda i: (0, 0)),
            pl.BlockSpec((n_groups, cg, cg), lambda i: (0, 0, 0)),
            pl.BlockSpec((1, dp), lambda i: (0, 0)),
        ],
        out_specs=pl.BlockSpec((ts, dc + dp), lambda i: (i, 0)),
        out_shape=_shape((s, dc + dp), BF16),
        compiler_params=_compute_params("parallel"),
    )(proj, proj, proj, proj, proj, proj, proj, conv_w, pool_w, pool_scale)


def _mixer_bwd(dy, proj, conv_w, pool_w, pool_scale, *, name):
    s, e = proj.shape
    dc = conv_w.shape[1]
    n_groups, cg, _ = pool_w.shape
    dp = n_groups * cg
    ts = _tile(s, 256, HALO)
    halo_blocks = ts // HALO
    n_tiles = s // ts
    n_halo_blocks = s // HALO
    n_ext = ts + HALO

    def body(dyc_ref, dyp_ref, dycn_ref, dypn_ref, b_ref, c_ref, xt_ref, v_ref, bn_ref, ch_ref, xth_ref, vh_ref,
             cw_ref, pw_ref, ps_ref, dproj_ref, dcw_ref, dpw_ref, dps_ref):
        i = pl.program_id(0)
        has_past = i > 0
        has_next = i < n_tiles - 1

        @pl.when(i == 0)
        def _():
            dcw_ref[...] = jnp.zeros_like(dcw_ref)
            dpw_ref[...] = jnp.zeros_like(dpw_ref)
            dps_ref[...] = jnp.zeros_like(dps_ref)

        c_now, xt_now, b_now = c_ref[...].astype(F32), xt_ref[...].astype(F32), b_ref[...].astype(F32)
        u_ext = jnp.concatenate(
            [jnp.where(has_past, ch_ref[...].astype(F32) * xth_ref[...].astype(F32), 0.0), c_now * xt_now], axis=0
        )
        u0 = u_ext[HALO:]
        u1 = pltpu.roll(u_ext, 1, 0)[HALO:]
        u2 = pltpu.roll(u_ext, 2, 0)[HALO:]
        dyc = dyc_ref[...]
        conv = cw_ref[2:3, :] * u0 + cw_ref[1:2, :] * u1 + cw_ref[0:1, :] * u2
        dproj_ref[:, 0:dc] = (dyc * conv).astype(BF16)
        dconv = dyc * b_now
        dconv_ext = jnp.concatenate(
            [dconv, jnp.where(has_next, dycn_ref[...] * bn_ref[...].astype(F32), 0.0)], axis=0
        )
        du = (
            cw_ref[2:3, :] * dconv
            + cw_ref[1:2, :] * pltpu.roll(dconv_ext, n_ext - 1, 0)[:ts]
            + cw_ref[0:1, :] * pltpu.roll(dconv_ext, n_ext - 2, 0)[:ts]
        )
        dproj_ref[:, dc : 2 * dc] = (du * xt_now).astype(BF16)
        dproj_ref[:, 2 * dc : 3 * dc] = (du * c_now).astype(BF16)
        dcw_ref[0:1, :] += jnp.sum(dconv * u2, axis=0, keepdims=True)
        dcw_ref[1:2, :] += jnp.sum(dconv * u1, axis=0, keepdims=True)
        dcw_ref[2:3, :] += jnp.sum(dconv * u0, axis=0, keepdims=True)

        for gi, window in enumerate(POOL_WINDOWS):
            cols = slice(gi * cg, (gi + 1) * cg)
            v_ext = jnp.concatenate(
                [jnp.where(has_past, vh_ref[:, cols].astype(F32), 0.0), v_ref[:, cols].astype(F32)], axis=0
            )
            mean = _trailing_sums(v_ext, window)[HALO:] * _inverse_counts(i * ts, ts, window)
            diff = (mean - v_ext[HALO:]).astype(BF16)
            z = jnp.dot(diff, pw_ref[gi], preferred_element_type=F32)
            dyp = dyp_ref[:, cols]
            dps_ref[:, cols] += jnp.sum(dyp * z, axis=0, keepdims=True)
            scale = ps_ref[:, cols]
            dz_ext = jnp.concatenate([dyp * scale, jnp.where(has_next, dypn_ref[:, cols] * scale, 0.0)], axis=0)
            dz_ext = dz_ext.astype(BF16)
            dpw_ref[gi] += lax.dot_general(
                diff, dz_ext[:ts], (((0,), (0,)), ((), ())), preferred_element_type=F32
            )
            ddiff_ext = lax.dot_general(
                dz_ext, pw_ref[gi], (((1,), (1,)), ((), ())), preferred_element_type=F32
            )
            q_ext = ddiff_ext * _inverse_counts(i * ts, n_ext, window)
            dv = _leading_sums(q_ext, window)[:ts] - ddiff_ext[:ts]
            dproj_ref[:, 3 * dc + gi * cg : 3 * dc + (gi + 1) * cg] = dv.astype(BF16)

    def col(jc):
        return pl.BlockSpec((ts, dc), lambda i: (i, jc))

    def past(jc):
        return pl.BlockSpec((HALO, dc), lambda i: (jnp.maximum(i * halo_blocks - 1, 0), jc))

    def following(jc):
        return pl.BlockSpec((HALO, dc), lambda i: (jnp.minimum((i + 1) * halo_blocks, n_halo_blocks - 1), jc))

    return pl.pallas_call(
        body,
        name=name,
        grid=(n_tiles,),
        in_specs=[
            col(0), col(1), following(0), following(1),
            col(0), col(1), col(2), col(3), following(0), past(1), past(2), past(3),
            pl.BlockSpec((8, dc), lambda i: (0, 0)),
            pl.BlockSpec((n_groups, cg, cg), lambda i: (0, 0, 0)),
            pl.BlockSpec((1, dp), lambda i: (0, 0)),
        ],
        out_specs=[
            pl.BlockSpec((ts, e), lambda i: (i, 0)),
            pl.BlockSpec((8, dc), lambda i: (0, 0)),
            pl.BlockSpec((n_groups, cg, cg), lambda i: (0, 0, 0)),
            pl.BlockSpec((1, dp), lambda i: (0, 0)),
        ],
        out_shape=[_shape((s, e), BF16), _shape((8, dc), F32), _shape((n_groups, cg, cg), F32), _shape((1, dp), F32)],
        compiler_params=_compute_params("arbitrary"),
    )(dy, dy, dy, dy, proj, proj, proj, proj, proj, proj, proj, proj, conv_w, pool_w, pool_scale)


def _cast_bf16(a, *, name):
    rows, cols = a.shape
    tr = _tile(rows, 512, SUBLANES_BF16)

    def body(a_ref, o_ref):
        o_ref[...] = a_ref[...].astype(BF16)

    spec = pl.BlockSpec((tr, cols), lambda i: (i, 0))
    return pl.pallas_call(
        body, name=name, grid=(rows // tr,), in_specs=[spec], out_specs=spec, out_shape=_shape((rows, cols), BF16),
        compiler_params=_compute_params("parallel"),
    )(a)


def _core_index():
    return lax.axis_index("c").astype(jnp.int32).reshape((1,))


def _add_sibling_half(grads, received, *, name, deps=()):
    nsh, r, c = grads.shape
    hr = r // 2
    tr = _tile(hr, 512, SUBLANES_BF16)
    tiles = hr // tr

    def body(core_ref, *refs):
        g_ref, r_ref, o_ref = refs[len(deps):]
        o_ref[...] = (g_ref[...].astype(F32) + r_ref[...].astype(F32)).astype(BF16)

    half = pl.BlockSpec((None, tr, c), lambda sh, t, core: (sh, t, 0))
    return pl.pallas_call(
        body,
        name=name,
        grid_spec=pltpu.PrefetchScalarGridSpec(
            num_scalar_prefetch=1,
            grid=(nsh, tiles),
            in_specs=[ANY_SPEC] * len(deps)
            + [pl.BlockSpec((None, tr, c), lambda sh, t, core: (sh, core[0] * tiles + t, 0)), half],
            out_specs=half,
        ),
        out_shape=_shape((nsh, hr, c), BF16),
        compiler_params=_compute_params("parallel", "parallel"),
    )(_core_index(), *deps, grads, received)


def _chip_index_operand():
    return _chip_index(lax.axis_index("x"), lax.axis_index("y")).astype(jnp.int32).reshape((1,))


def _sum_chip_partials(own, received, *, name, deps=()):
    nsh, hr, c = received.shape
    tr = _tile(hr, 256, SUBLANES_BF16)

    def body(chip_ref, *refs):
        own_ref, p_ref, o_ref = refs[len(deps):]
        mine = chip_ref[0]
        total = None
        for sh in range(nsh):
            term = jnp.where(mine == sh, own_ref[...], p_ref[sh]).astype(F32)
            total = term if total is None else total + term
        o_ref[...] = total

    return pl.pallas_call(
        body,
        name=name,
        grid_spec=pltpu.PrefetchScalarGridSpec(
            num_scalar_prefetch=1,
            grid=(hr // tr,),
            in_specs=[ANY_SPEC] * len(deps) + [
                pl.BlockSpec((None, tr, c), lambda t, chip: (chip[0], t, 0)),
                pl.BlockSpec((nsh, tr, c), lambda t, chip: (0, t, 0)),
            ],
            out_specs=pl.BlockSpec((tr, c), lambda t, chip: (t, 0)),
        ),
        out_shape=_shape((hr, c), F32),
        compiler_params=_compute_params("parallel"),
    )(_chip_index_operand(), *deps, own, received)


def _adamw(grad, w, m, v, layer, carried, *, name, deps=()):
    n_layers, r, c = w.shape
    in_halves = isinstance(grad, (tuple, list))
    tr = _tile(r // 2 if in_halves else r, 256, 8)
    half_tiles = (r // 2) // tr if in_halves else 0
    bias1 = 1.0 - ADAM_B1 ** ADAM_STEP
    bias2 = 1.0 - ADAM_B2 ** ADAM_STEP
    n_grads = 2 if in_halves else 1

    def body(core_ref, *refs):
        refs = refs[len(deps):]
        w_ref, m_ref, v_ref = refs[n_grads : n_grads + 3]
        go_ref, d_ref, mo_ref, vo_ref, done_ref = refs[-5:]
        done_ref[...] = jnp.zeros_like(done_ref)
        if in_halves:
            in_my_half = pl.program_id(0) // half_tiles == core_ref[0]
            g = jnp.where(in_my_half, refs[0][...], refs[1][...])
        else:
            g = refs[0][...]
        m_new = ADAM_B1 * m_ref[...] + (1.0 - ADAM_B1) * g
        v_new = ADAM_B2 * v_ref[...] + (1.0 - ADAM_B2) * (g * g)
        go_ref[...] = g
        mo_ref[...] = m_new
        vo_ref[...] = v_new
        d_ref[...] = -ADAM_LR * ((m_new / bias1) / (jnp.sqrt(v_new / bias2) + ADAM_EPS) + ADAM_WD * w_ref[...])

    def half_spec(mine):
        def index(t, core):
            first = (core[0] if mine else 1 - core[0]) * half_tiles
            return (jnp.clip(t - first, 0, half_tiles - 1), 0)

        return pl.BlockSpec((tr, c), index)

    layer_tile = pl.BlockSpec((None, tr, c), lambda t, core: (layer, t, 0))
    if in_halves:
        grad_specs, grads = [half_spec(True), half_spec(False)], list(grad)
    else:
        grad_specs, grads = [pl.BlockSpec((tr, c), lambda t, core: (t, 0))], [grad]
    in_specs = [ANY_SPEC] * len(deps) + grad_specs + [layer_tile] * 3
    args = list(deps) + grads + [w, m, v]
    aliases = {}
    if carried is not None:
        aliases = {1 + len(args) + n: n for n in range(4)}
        in_specs += [HBM_SPEC] * 4
        args += list(carried)
    *outs, done = pl.pallas_call(
        body,
        name=name,
        grid_spec=pltpu.PrefetchScalarGridSpec(
            num_scalar_prefetch=1,
            grid=(r // tr,),
            in_specs=in_specs,
            out_specs=[layer_tile] * 4 + [pl.BlockSpec((8, LANES), lambda t, core: (0, 0))],
        ),
        out_shape=[_shape((n_layers, r, c), F32)] * 4 + [_shape((8, LANES), F32)],
        input_output_aliases=aliases,
        compiler_params=_compute_params("arbitrary"),
    )(_core_index(), *args)
    return outs, done


def _place():
    x, y, c = (lax.axis_index(a) for a in MESH_AXES)
    other_chips = [(1 - x, y), (x, 1 - y), (1 - x, 1 - y)]
    return x, y, c, other_chips


def _chip_index(x, y):
    return 2 * x + y


def _gather_small(conv_rows, pool_rows, *, name):
    blocks = (conv_rows, pool_rows)
    n = len(blocks)

    def body(*refs):
        ins, outs = refs[:n], refs[n : 2 * n]
        send_sems, recv_sems = refs[2 * n :]
        x, y, c, other_chips = _place()
        mine = _chip_index(x, y)
        peers = [(x, y, 1 - c)] + [(px, py, c) for px, py in other_chips]
        _handshake(peers)
        sends = []
        for a in range(n):
            for j, peer in enumerate(peers):
                cp = pltpu.make_async_remote_copy(
                    src_ref=ins[a], dst_ref=outs[a].at[mine], send_sem=send_sems.at[a, j], recv_sem=recv_sems.at[a, j],
                    device_id=peer, device_id_type=MESH_ID,
                )
                cp.start()
                sends.append(cp)
        for a in range(n):
            for j, (px, py, _) in enumerate(peers):
                pltpu.make_async_remote_copy(
                    src_ref=ins[a], dst_ref=outs[a].at[_chip_index(px, py)], send_sem=send_sems.at[a, j],
                    recv_sem=recv_sems.at[a, j], device_id=peers[j], device_id_type=MESH_ID,
                ).wait_recv()
        for cp in sends:
            cp.wait_send()

    return _sequencer_call(
        body,
        [_shape((N_CHIPS,) + blk.shape, F32) for blk in blocks],
        [pltpu.SemaphoreType.DMA((n, 4)), pltpu.SemaphoreType.DMA((n, 4))],
        GATHER_SMALL_COLLECTIVE_ID,
        name,
    )(*blocks)


def _allreduce_small(vec, *, name):
    rows, n = vec.shape
    n_dev = 8

    def body(v_ref, o_ref, slots, send_sems, recv_sems):
        x, y, c, _ = _place()
        me = 4 * x + 2 * y + c
        slots[me] = v_ref[...]
        sends = []
        for mask in range(1, n_dev):
            fx, fy, fc = (mask >> 2) & 1, (mask >> 1) & 1, mask & 1
            peer = (x ^ fx, y ^ fy, c ^ fc)
            cp = pltpu.make_async_remote_copy(
                src_ref=v_ref, dst_ref=slots.at[me], send_sem=send_sems.at[mask - 1], recv_sem=recv_sems.at[mask - 1],
                device_id=peer, device_id_type=MESH_ID,
            )
            cp.start()
            sends.append(cp)
        for mask in range(1, n_dev):
            fx, fy, fc = (mask >> 2) & 1, (mask >> 1) & 1, mask & 1
            peer = (x ^ fx, y ^ fy, c ^ fc)
            pltpu.make_async_remote_copy(
                src_ref=v_ref, dst_ref=slots.at[4 * peer[0] + 2 * peer[1] + peer[2]], send_sem=send_sems.at[mask - 1],
                recv_sem=recv_sems.at[mask - 1], device_id=peer, device_id_type=MESH_ID,
            ).wait_recv()
        total = slots[0]
        for dev in range(1, n_dev):
            total = total + slots[dev]
        o_ref[...] = total
        for cp in sends:
            cp.wait_send()

    return pl.pallas_call(
        body,
        name=name,
        in_specs=[VMEM_SPEC],
        out_specs=VMEM_SPEC,
        out_shape=_shape((rows, n), F32),
        scratch_shapes=[
            pltpu.VMEM((n_dev, rows, n), F32),
            pltpu.SemaphoreType.DMA((n_dev - 1,)),
            pltpu.SemaphoreType.DMA((n_dev - 1,)),
        ],
    )(vec)


def _handshake(peers):
    barrier = pltpu.get_barrier_semaphore()
    for peer in peers:
        pl.semaphore_signal(barrier, inc=1, device_id=peer, device_id_type=MESH_ID)
    pl.semaphore_wait(barrier, len(peers))


def _sequencer_call(body, out_type, scratch_types, collective_id, name):
    return pl.kernel(
        body,
        name=name,
        out_type=out_type,
        mesh=plsc.ScalarSubcoreMesh(axis_name="sequencer", num_cores=1),
        scratch_types=scratch_types,
        compiler_params=pltpu.CompilerParams(collective_id=collective_id),
    )


GATHER_COLLECTIVE_ID = 1
EXCHANGE_COLLECTIVE_ID = 2
SCATTER_COLLECTIVE_ID = 3
SHARE_COLLECTIVE_ID = 4
GATHER_SMALL_COLLECTIVE_ID = 5


def _gather_weights(stacked, layer, *, name):
    n = len(stacked)

    def body(*refs):
        ins, outs = refs[:n], refs[n : 2 * n]
        own_sems, send_sems, recv_sems, pass_send_sems, pass_recv_sems = refs[2 * n :]
        x, y, c, other_chips = _place()
        mine = _chip_index(x, y)
        sibling = (x, y, 1 - c)
        _handshake([sibling] + [(px, py, c) for px, py in other_chips])
        pending = []

        def send_my_half(a):
            hr = ins[a].shape[1] // 2
            my_half = pl.ds(c * hr, hr)
            cp = pltpu.make_async_remote_copy(
                src_ref=ins[a].at[layer], dst_ref=outs[a].at[mine], send_sem=own_sems.at[0, a],
                recv_sem=own_sems.at[1, a], device_id=sibling, device_id_type=MESH_ID,
            )
            cp.start()
            pending.append(cp.wait)
            for j, (px, py) in enumerate(other_chips):
                cp = pltpu.make_async_remote_copy(
                    src_ref=ins[a].at[layer, my_half], dst_ref=outs[a].at[mine, my_half], send_sem=send_sems.at[a, j],
                    recv_sem=recv_sems.at[a, j], device_id=(px, py, c), device_id_type=MESH_ID,
                )
                cp.start()
                pending.append(cp.wait_send)

        for a in range(min(2, n)):
            send_my_half(a)
        for a in range(n):
            hr = ins[a].shape[1] // 2
            my_half = pl.ds(c * hr, hr)
            for j, (px, py) in enumerate(other_chips):
                landed = outs[a].at[_chip_index(px, py), my_half]
                pltpu.make_async_remote_copy(
                    src_ref=landed, dst_ref=landed, send_sem=send_sems.at[a, j], recv_sem=recv_sems.at[a, j],
                    device_id=(px, py, c), device_id_type=MESH_ID,
                ).wait_recv()
                cp = pltpu.make_async_remote_copy(
                    src_ref=landed, dst_ref=landed, send_sem=pass_send_sems.at[a, j], recv_sem=pass_recv_sems.at[a, j],
                    device_id=sibling, device_id_type=MESH_ID,
                )
                cp.start()
                pending.append(cp.wait_send)
            if a + 2 < n:
                send_my_half(a + 2)
        for a in range(n):
            hr = ins[a].shape[1] // 2
            sibling_half = pl.ds((1 - c) * hr, hr)
            for j, (px, py) in enumerate(other_chips):
                passed = outs[a].at[_chip_index(px, py), sibling_half]
                pltpu.make_async_remote_copy(
                    src_ref=passed, dst_ref=passed, send_sem=pass_send_sems.at[a, j], recv_sem=pass_recv_sems.at[a, j],
                    device_id=sibling, device_id_type=MESH_ID,
                ).wait_recv()
        for wait in pending:
            wait()

    return _sequencer_call(
        body,
        [_shape((N_CHIPS,) + a.shape[1:], BF16) for a in stacked],
        [pltpu.SemaphoreType.DMA((2, n))] + [pltpu.SemaphoreType.DMA((n, 3))] * 4,
        GATHER_COLLECTIVE_ID,
        name,
    )(*stacked)


def _exchange_halves(grads, *, name):
    n = len(grads)

    def body(*refs):
        ins, outs = refs[:n], refs[n : 2 * n]
        send_sems, recv_sems = refs[2 * n :]
        x, y, c, _ = _place()
        sibling = (x, y, 1 - c)
        _handshake([sibling])
        copies = []
        for a in range(n):
            hr = ins[a].shape[1] // 2
            cp = pltpu.make_async_remote_copy(
                src_ref=ins[a].at[:, pl.ds((1 - c) * hr, hr), :], dst_ref=outs[a], send_sem=send_sems.at[a],
                recv_sem=recv_sems.at[a], device_id=sibling, device_id_type=MESH_ID,
            )
            cp.start()
            copies.append(cp)
        for cp in copies:
            cp.wait()

    return _sequencer_call(
        body,
        [_shape((g.shape[0], g.shape[1] // 2, g.shape[2]), BF16) for g in grads],
        [pltpu.SemaphoreType.DMA((n,)), pltpu.SemaphoreType.DMA((n,))],
        EXCHANGE_COLLECTIVE_ID,
        name,
    )(*grads)


def _scatter_partials(partials, *, name):
    n = len(partials)

    def body(*refs):
        ins, outs = refs[:n], refs[n : 2 * n]
        send_sems, recv_sems = refs[2 * n :]
        x, y, c, other_chips = _place()
        mine = _chip_index(x, y)
        _handshake([(px, py, c) for px, py in other_chips])
        pending = []
        for a in range(n):
            for j, (px, py) in enumerate(other_chips):
                cp = pltpu.make_async_remote_copy(
                    src_ref=ins[a].at[_chip_index(px, py)], dst_ref=outs[a].at[mine], send_sem=send_sems.at[a, j],
                    recv_sem=recv_sems.at[a, j], device_id=(px, py, c), device_id_type=MESH_ID,
                )
                cp.start()
                pending.append(cp.wait_send)
        for a in range(n):
            for j, (px, py) in enumerate(other_chips):
                landed = outs[a].at[_chip_index(px, py)]
                pltpu.make_async_remote_copy(
                    src_ref=landed, dst_ref=landed, send_sem=send_sems.at[a, j], recv_sem=recv_sems.at[a, j],
                    device_id=(px, py, c), device_id_type=MESH_ID,
                ).wait_recv()
        for wait in pending:
            wait()

    return _sequencer_call(
        body,
        [_shape(p.shape, BF16) for p in partials],
        [pltpu.SemaphoreType.DMA((n, 3)), pltpu.SemaphoreType.DMA((n, 3))],
        SCATTER_COLLECTIVE_ID,
        name,
    )(*partials)


def _share_halves(halves, *, name):
    n = len(halves)

    def body(*refs):
        ins, outs = refs[:n], refs[n : 2 * n]
        send_sems, recv_sems = refs[2 * n :]
        x, y, c, _ = _place()
        sibling = (x, y, 1 - c)
        _handshake([sibling])
        copies = []
        for a in range(n):
            cp = pltpu.make_async_remote_copy(
                src_ref=ins[a], dst_ref=outs[a], send_sem=send_sems.at[a], recv_sem=recv_sems.at[a],
                device_id=sibling, device_id_type=MESH_ID,
            )
            cp.start()
            copies.append(cp)
        for cp in copies:
            cp.wait()

    return _sequencer_call(
        body,
        [_shape(h.shape, F32) for h in halves],
        [pltpu.SemaphoreType.DMA((n,)), pltpu.SemaphoreType.DMA((n,))],
        SHARE_COLLECTIVE_ID,
        name,
    )(*halves)


class _ReduceToOwner:
    def __init__(self, grads, tag):
        self.grads, self.tag = grads, tag
        self.received = _exchange_halves(grads, name=f"exchange_halves_{tag}")

    def add_sibling(self, after):
        partials = [
            _add_sibling_half(g, r, name=f"add_sibling_{self.tag}_{a}", deps=after)
            for a, (g, r) in enumerate(zip(self.grads, self.received))
        ]
        self.partials = partials
        self.slots = _scatter_partials(partials, name=f"scatter_partials_{self.tag}")
        return partials

    def sum_chips(self, after):
        halves = [
            _sum_chip_partials(p, s, name=f"sum_partials_{self.tag}_{a}", deps=after)
            for a, (p, s) in enumerate(zip(self.partials, self.slots))
        ]
        self.halves = halves
        self.theirs = _share_halves(halves, name=f"share_halves_{self.tag}")
        return halves

    def totals(self):
        return list(zip(self.halves, self.theirs))


def kernel(x, w_in, conv_w, pool_w, pool_scale, w_out, norm_mix, norm_mlp, w_up, w_down, norm_final, loss_target, m_w_in, m_conv_w, m_pool_w, m_pool_scale, m_w_out, m_norm_mix, m_norm_mlp, m_w_up, m_w_down, m_norm_final, v_w_in, v_conv_w, v_pool_w, v_pool_scale, v_w_out, v_norm_mix, v_norm_mlp, v_w_up, v_w_down, v_norm_final):
    n_layers, d, _ = w_in.shape
    s = x.shape[1]
    dc = conv_w.shape[2] * N_CHIPS
    n_groups, cg_rows, cg = pool_w.shape[1:]
    dp = n_groups * cg
    x0 = x.reshape(s, d)
    target = loss_target.reshape(s, d)

    big = [w_in, w_out, w_up, w_down]
    big_bf16 = [
        _cast_bf16(w.reshape(-1, w.shape[2]), name=f"cast_{t}").reshape(w.shape)
        for t, w in zip(("w_in", "w_out", "w_up", "w_down"), big)
    ]

    conv_rows = jnp.pad(conv_w, ((0, 0), (0, 8 - CONV_TAPS), (0, 0))).reshape(n_layers * 8, -1)
    pool_rows = pool_w.reshape(n_layers * n_groups * cg_rows, cg)
    conv_all, pool_all = _gather_small(conv_rows, pool_rows, name="gather_small")
    conv_full = conv_all.reshape(N_CHIPS, n_layers, 8, -1).transpose(1, 2, 0, 3).reshape(n_layers, 8, dc)
    pool_full = (
        pool_all.reshape(N_CHIPS, n_layers, n_groups, cg_rows, cg).transpose(1, 2, 0, 3, 4)
        .reshape(n_layers, n_groups, cg, cg).astype(BF16)
    )

    saved = []
    xl = x0
    for l in range(n_layers):
        win_g, wout_g, wup_g, wdown_g = [
            _gather_weights([w], l, name=f"gather_weights_l{l}_{t}")[0] for t, w in enumerate(big_bf16)
        ]
        gain_mix = norm_mix[l].reshape(1, d)
        gain_mlp = norm_mlp[l].reshape(1, d)
        scale = pool_scale[l].reshape(1, dp)
        h1 = _rmsnorm(xl, gain_mix, name=f"norm_mix_l{l}")
        proj = _matmul_cols(h1, win_g, relu2=False, name=f"in_proj_l{l}")
        y = _mixer_fwd(proj, conv_full[l], pool_full[l], scale, name=f"mixer_fwd_l{l}")
        x_mid = _matmul_residual(y, wout_g.reshape(-1, d), xl, name=f"out_proj_l{l}")
        h2 = _rmsnorm(x_mid, gain_mlp, name=f"norm_mlp_l{l}")
        u, u2 = _matmul_cols(h2, wup_g, relu2=True, name=f"up_proj_l{l}")
        x_next = _matmul_residual(u2, wdown_g.reshape(-1, d), x_mid, name=f"down_proj_l{l}")
        saved.append((xl, h1, proj, y, x_mid, h2, u, u2, win_g, wout_g, wup_g, wdown_g, gain_mix, gain_mlp, scale))
        xl = x_next

    loss_part, dx, dx_bf16, d_norm_final = _loss_and_grad(xl, norm_final.reshape(1, d), target, name="loss_head")
    loss = lax.psum(loss_part[0, 0], MESH_AXES)

    small_grads = [None] * n_layers
    carried = [None] * 5
    pool_params = tuple(p.reshape(n_layers, n_groups * cg_rows, cg) for p in (pool_w, m_pool_w, v_pool_w))
    params = [(w_in, m_w_in, v_w_in), (w_out, m_w_out, v_w_out), (w_up, m_w_up, v_w_up), (w_down, m_w_down, v_w_down),
              pool_params]
    DOWN, UP_OUT, IN_POOL = (3,), (2, 1), (0, 4)

    def update(reduce, which, layer, after):
        dones = []
        for a, total in zip(which, reduce.totals()):
            w, m, v = params[a]
            carried[a], done = _adamw(total, w, m, v, layer, carried[a], name=f"adamw_{a}_l{layer}", deps=after)
            dones.append(done)
        return dones

    up_out_above = in_pool_above = None
    for l in reversed(range(n_layers)):
        xl, h1, proj, y, x_mid, h2, u, u2, win_g, wout_g, wup_g, wdown_g, gain_mix, gain_mlp, scale = saved[l]
        above = up_out_above is not None
        deps = in_pool_above.add_sibling([dx_bf16]) if above else []
        g_down = _matmul_tn(u2, dx_bf16, n_shards=N_CHIPS, shard_cols=False, name=f"grad_w_down_l{l}", deps=deps)
        down = _ReduceToOwner([g_down], f"down_l{l}")
        deps = up_out_above.sum_chips([g_down]) if above else [g_down]
        da = _matmul_nt(dx_bf16, wdown_g.reshape(-1, d), u=u, name=f"grad_act_l{l}", deps=deps)
        g_up = _matmul_tn(
            h2, da, n_shards=N_CHIPS, shard_cols=True, name=f"grad_w_up_l{l}", deps=down.add_sibling([da])
        )
        deps = in_pool_above.sum_chips([g_up]) if above else [g_up]
        dx_mid, dx_mid_bf16, d_gain_mlp = _matmul_nt_norm_bwd(
            da, wup_g, x_mid, gain_mlp, dx, name=f"grad_mid_l{l}", deps=deps
        )
        g_out = _matmul_tn(
            y, dx_mid_bf16, n_shards=N_CHIPS, shard_cols=False, name=f"grad_w_out_l{l}",
            deps=down.sum_chips([dx_mid_bf16]),
        )
        up_out = _ReduceToOwner([g_up, g_out], f"up_out_l{l}")
        deps = update(up_out_above, UP_OUT, l + 1, [g_out]) if above else [g_out]
        dy = _matmul_nt(dx_mid_bf16, wout_g.reshape(-1, d), name=f"grad_mixed_l{l}", deps=deps)
        dproj, d_conv, d_pool, d_scale = _mixer_bwd(dy, proj, conv_full[l], pool_full[l], scale, name=f"mixer_bwd_l{l}")
        g_in = _matmul_tn(
            h1, dproj, n_shards=N_CHIPS, shard_cols=True, name=f"grad_w_in_l{l}", deps=up_out.add_sibling([dproj])
        )
        g_pool = (
            d_pool.reshape(n_groups, N_CHIPS, cg_rows, cg).transpose(1, 0, 2, 3)
            .reshape(N_CHIPS, n_groups * cg_rows, cg).astype(BF16)
        )
        in_pool = _ReduceToOwner([g_in, g_pool], f"in_pool_l{l}")
        deps = update(down, DOWN, l, [g_in])
        if above:
            deps = deps + update(in_pool_above, IN_POOL, l + 1, [g_in])
        dx, dx_bf16, d_gain_mix = _matmul_nt_norm_bwd(
            dproj, win_g, xl, gain_mix, dx_mid, name=f"grad_x_l{l}", deps=deps
        )
        small_grads[l] = jnp.concatenate(
            [d_conv[:CONV_TAPS].reshape(-1), d_scale.reshape(-1), d_gain_mix.reshape(-1), d_gain_mlp.reshape(-1)]
        )
        up_out_above, in_pool_above = up_out, in_pool
    deps = in_pool_above.add_sibling([dx_bf16])
    deps = up_out_above.sum_chips(deps)
    deps = in_pool_above.sum_chips(deps)
    deps = update(up_out_above, UP_OUT, 0, deps)
    update(in_pool_above, IN_POOL, 0, deps)

    vec = jnp.concatenate(small_grads + [d_norm_final.reshape(-1)])
    vec = _allreduce_small(vec.reshape(8, -1), name="allreduce_small").reshape(-1)
    per_layer = vec[: n_layers * (CONV_TAPS * dc + dp + 2 * d)].reshape(n_layers, -1)
    chip = _chip_index(lax.axis_index("x"), lax.axis_index("y"))
    dcs = dc // N_CHIPS
    g_conv = lax.dynamic_slice_in_dim(per_layer[:, : CONV_TAPS * dc].reshape(n_layers, CONV_TAPS, dc), chip * dcs, dcs, axis=2)
    g_scale = per_layer[:, CONV_TAPS * dc : CONV_TAPS * dc + dp]
    g_mix = per_layer[:, CONV_TAPS * dc + dp : CONV_TAPS * dc + dp + d]
    g_mlp = per_layer[:, CONV_TAPS * dc + dp + d :]
    g_final = vec[n_layers * (CONV_TAPS * dc + dp + 2 * d) :]

    def small_adamw(g, w, m, v, tag):
        flat = lambda t: t.reshape(1, -1, t.shape[-1])
        out, _ = _adamw(flat(g)[0], flat(w), flat(m), flat(v), 0, None, name=f"adamw_{tag}")
        return [o.reshape(w.shape) for o in out]

    o_conv = small_adamw(g_conv, conv_w, m_conv_w, v_conv_w, "conv_w")
    o_scale = small_adamw(g_scale, pool_scale, m_pool_scale, v_pool_scale, "pool_scale")
    o_mix = small_adamw(g_mix, norm_mix, m_norm_mix, v_norm_mix, "norm_mix")
    o_mlp = small_adamw(g_mlp, norm_mlp, m_norm_mlp, v_norm_mlp, "norm_mlp")
    o_final = small_adamw(g_final, norm_final, m_norm_final, v_norm_final, "norm_final")
    o_in, o_out, o_up, o_down, o_pool = carried
    o_pool = [o.reshape(pool_w.shape) for o in o_pool]

    ordered = [o_in, o_conv, o_pool, o_scale, o_out, o_mix, o_mlp, o_up, o_down, o_final]
    return (loss, dx.reshape(x.shape), *[o[0] for o in ordered], *[o[1] for o in ordered], *[o[2] for o in ordered],
            *[o[3] for o in ordered])
```

```python
import functools

import jax
import jax.numpy as jnp
from jax import lax
from jax.experimental import pallas as pl
from jax.experimental.pallas import tpu as pltpu
from jax.experimental.pallas import tpu_sc as plsc

F32 = jnp.float32
BF16 = jnp.bfloat16

EPS = 1e-6
POOL_WINDOWS = (2, 4, 8, 16)
CONV_TAPS = 3
HALO = 16

ADAM_LR = 0.001
ADAM_B1 = 0.9
ADAM_B2 = 0.999
ADAM_EPS = 1e-08
ADAM_WD = 0.01
ADAM_STEP = 10

N_CHIPS = 4
MESH_AXES = ("x", "y", "c")
V7X_VMEM_LIMIT_BYTES = 56 * 1024 * 1024
SUBLANES_BF16 = 16
LANES = 128

HBM_SPEC = pl.BlockSpec(memory_space=pltpu.HBM)
VMEM_SPEC = pl.BlockSpec(memory_space=pltpu.VMEM)
MESH_ID = pl.DeviceIdType.MESH


def _tile(dim, target, align):
    if dim <= target:
        return dim
    t = (target // align) * align
    while dim % t:
        t -= align
    assert t > 0, (dim, target, align)
    return t


def _compute_params(*semantics):
    return pltpu.CompilerParams(dimension_semantics=semantics, vmem_limit_bytes=V7X_VMEM_LIMIT_BYTES)


def _shape(shape, dtype):
    return jax.ShapeDtypeStruct(shape, dtype)


ANY_SPEC = pl.BlockSpec(memory_space=pl.ANY)


def _behind(body, deps):
    return lambda *refs: body(*refs[len(deps):])


def _rmsnorm_bwd(dh, x, gain, dres):
    r = lax.rsqrt(jnp.mean(x * x, axis=-1, keepdims=True) + EPS)
    xn = x * r
    dgain = jnp.sum(dh * xn, axis=0, keepdims=True)
    dxn = dh * gain
    dx = r * (dxn - xn * jnp.mean(dxn * xn, axis=-1, keepdims=True))
    if dres is not None:
        dx = dx + dres
    return dx, dgain


MXU_ROWS = 512


def _rmsnorm(x, gain, *, name):
    s, d = x.shape
    tm = _tile(s, 256, SUBLANES_BF16)

    def body(x_ref, g_ref, h_ref):
        xf = x_ref[...]
        r = lax.rsqrt(jnp.mean(xf * xf, axis=-1, keepdims=True) + EPS)
        h_ref[...] = ((xf * r) * g_ref[...]).astype(BF16)

    row_tile = pl.BlockSpec((tm, d), lambda i: (i, 0))
    return pl.pallas_call(
        body, name=name, grid=(s // tm,), in_specs=[row_tile, pl.BlockSpec((1, d), lambda i: (0, 0))],
        out_specs=row_tile, out_shape=_shape((s, d), BF16), compiler_params=_compute_params("parallel"),
    )(x, gain)


def _matmul_cols(h, w, *, relu2, name):
    s, k = h.shape
    nsh, _, c = w.shape
    tm = _tile(s, 2048, MXU_ROWS)
    tn = _tile(c, 1024, LANES)
    rows = min(MXU_ROWS, tm)
    per_shard = c // tn

    def body(h_ref, w_ref, *out_refs):
        def step(t, carry):
            rs = pl.ds(pl.multiple_of(t * rows, rows), rows)
            acc = jnp.dot(h_ref[rs, :], w_ref[...], preferred_element_type=F32)
            if relu2:
                u = jnp.maximum(acc, 0.0)
                out_refs[0][rs, :] = u.astype(BF16)
                out_refs[1][rs, :] = (u * u).astype(BF16)
            else:
                out_refs[0][rs, :] = acc.astype(BF16)
            return carry

        lax.fori_loop(0, tm // rows, step, 0)

    tile_out = pl.BlockSpec((tm, tn), lambda i, j: (i, j))
    n = nsh * c
    outs = [_shape((s, n), BF16)] * (2 if relu2 else 1)
    result = pl.pallas_call(
        body,
        name=name,
        grid=(s // tm, n // tn),
        in_specs=[
            pl.BlockSpec((tm, k), lambda i, j: (i, 0)),
            pl.BlockSpec((None, k, tn), lambda i, j: (j // per_shard, 0, j % per_shard)),
        ],
        out_specs=[tile_out] * len(outs),
        out_shape=outs,
        compiler_params=_compute_params("parallel", "arbitrary"),
    )(h, w)
    return result if relu2 else result[0]


def _matmul_residual(a, w, res, *, name):
    s, k = a.shape
    _, n = w.shape
    tm = _tile(s, 1024, MXU_ROWS)
    tn = _tile(n, 1024, LANES)
    tk = _tile(k, 2048, LANES)
    rows = min(MXU_ROWS, tm)

    def body(a_ref, w_ref, r_ref, o_ref):
        kk = pl.program_id(2)

        @pl.when(kk == 0)
        def _():
            o_ref[...] = r_ref[...]

        def step(t, carry):
            rs = pl.ds(pl.multiple_of(t * rows, rows), rows)
            o_ref[rs, :] += jnp.dot(a_ref[rs, :], w_ref[...], preferred_element_type=F32)
            return carry

        lax.fori_loop(0, tm // rows, step, 0, unroll=True)

    return pl.pallas_call(
        body,
        name=name,
        grid=(s // tm, n // tn, k // tk),
        in_specs=[
            pl.BlockSpec((tm, tk), lambda i, j, kk: (i, kk)),
            pl.BlockSpec((tk, tn), lambda i, j, kk: (kk, j)),
            pl.BlockSpec((tm, tn), lambda i, j, kk: (i, j)),
        ],
        out_specs=pl.BlockSpec((tm, tn), lambda i, j, kk: (i, j)),
        out_shape=_shape((s, n), F32),
        compiler_params=_compute_params("parallel", "parallel", "arbitrary"),
    )(a, w, res)


def _matmul_tn(a, g, *, n_shards, shard_cols, name, deps=()):
    m, kd = a.shape
    _, n = g.shape
    r, c = (kd, n // n_shards) if shard_cols else (kd // n_shards, n)
    tr = _tile(kd, 2048, LANES)
    tn = _tile(c, 1024, LANES)
    tm = _tile(m, 2048, SUBLANES_BF16)
    nm = m // tm
    cols_per_shard = c // tn

    out_rows = min(MXU_ROWS, tr)

    def body(a_ref, g_ref, o_ref, acc_ref):
        mm = pl.program_id(2)

        @pl.when(mm == 0)
        def _():
            acc_ref[...] = jnp.zeros_like(acc_ref)

        for t in range(tr // out_rows):
            rs = slice(t * out_rows, (t + 1) * out_rows)
            acc_ref[rs, :] += lax.dot_general(
                a_ref[:, rs], g_ref[...], (((0,), (0,)), ((), ())), preferred_element_type=F32
            )

        @pl.when(mm == nm - 1)
        def _():
            o_ref[...] = acc_ref[...].astype(BF16)

    if shard_cols:
        out_spec = pl.BlockSpec((None, tr, tn), lambda i, j, mm: (j // cols_per_shard, i, j % cols_per_shard))
        out_shape = _shape((n_shards, kd, c), BF16)
    else:
        out_spec = pl.BlockSpec((tr, tn), lambda i, j, mm: (i, j))
        out_shape = _shape((kd, n), BF16)
    out = pl.pallas_call(
        _behind(body, deps),
        name=name,
        grid=(kd // tr, n // tn, nm),
        in_specs=[ANY_SPEC] * len(deps) + [
            pl.BlockSpec((tm, tr), lambda i, j, mm: (mm, i)),
            pl.BlockSpec((tm, tn), lambda i, j, mm: (mm, j)),
        ],
        out_specs=out_spec,
        out_shape=out_shape,
        scratch_shapes=[pltpu.VMEM((tr, tn), F32)],
        compiler_params=_compute_params("parallel", "parallel", "arbitrary"),
    )(*deps, a, g)
    return out.reshape(n_shards, r, c)


def _matmul_nt(g, w, *, u=None, name, deps=()):
    m, n = g.shape
    kd, _ = w.shape
    tm = _tile(m, 2048, MXU_ROWS)
    tj = _tile(kd, 1024, LANES)
    rows = min(MXU_ROWS, tm)

    def body(*refs):
        if u is None:
            g_ref, w_ref, o_ref = refs
        else:
            g_ref, w_ref, u_ref, o_ref = refs

        def step(t, carry):
            rs = pl.ds(pl.multiple_of(t * rows, rows), rows)
            prod = lax.dot_general(g_ref[rs, :], w_ref[...], (((1,), (1,)), ((), ())), preferred_element_type=F32)
            if u is None:
                o_ref[rs, :] = prod
            else:
                o_ref[rs, :] = (prod * (2.0 * u_ref[rs, :].astype(F32))).astype(BF16)
            return carry

        lax.fori_loop(0, tm // rows, step, 0)

    tile_out = pl.BlockSpec((tm, tj), lambda i, j: (i, j))
    in_specs = [pl.BlockSpec((tm, n), lambda i, j: (i, 0)), pl.BlockSpec((tj, n), lambda i, j: (j, 0))]
    args = [g, w]
    if u is not None:
        in_specs.append(tile_out)
        args.append(u)
    return pl.pallas_call(
        _behind(body, deps),
        name=name,
        grid=(m // tm, kd // tj),
        in_specs=[ANY_SPEC] * len(deps) + in_specs,
        out_specs=tile_out,
        out_shape=_shape((m, kd), F32 if u is None else BF16),
        compiler_params=_compute_params("parallel", "arbitrary"),
    )(*deps, *args)


EPILOGUE_ROWS = 128


def _matmul_nt_norm_bwd(g, w, x, gain, dres, *, name, deps=()):
    s, n = g.shape
    nsh, d, c = w.shape
    tm = _tile(s, 1024, MXU_ROWS)
    tk = _tile(c, 1024, LANES)
    per_shard = c // tk
    nk = n // tk
    n_blocks = s // tm
    mxu_rows = min(MXU_ROWS, tm)
    rows = min(EPILOGUE_ROWS, tm)
    piece = tm // nk
    assert piece * nk == tm and piece % 8 == 0

    def body(g_ref, w_ref, gain_ref, x_hbm, dres_hbm, dx_hbm, dxb_hbm, dg_ref, acc_ref, x_buf, dres_buf, dx_buf,
             dxb_buf, in_sems, out_sems):
        i = pl.program_id(0)
        kk = pl.program_id(1)

        @pl.when((kk == 0) & (i == 0))
        def _():
            dg_ref[...] = jnp.zeros_like(dg_ref)

        block_rows = pl.ds(pl.multiple_of(i * tm, tm), tm)
        pieces = pl.ds(pl.multiple_of(kk * piece, piece), piece)
        piece_rows = pl.ds(pl.multiple_of(i * tm + kk * piece, piece), piece)
        pltpu.make_async_copy(x_hbm.at[piece_rows, :], x_buf.at[pieces, :], in_sems.at[0]).start()
        pltpu.make_async_copy(dres_hbm.at[piece_rows, :], dres_buf.at[pieces, :], in_sems.at[1]).start()

        @pl.when(kk == 0)
        def _():
            acc_ref[...] = jnp.zeros_like(acc_ref)

        def accumulate(t, carry):
            rs = pl.ds(pl.multiple_of(t * mxu_rows, mxu_rows), mxu_rows)
            acc_ref[rs, :] += lax.dot_general(
                g_ref[rs, :], w_ref[...], (((1,), (1,)), ((), ())), preferred_element_type=F32
            )
            return carry

        lax.fori_loop(0, tm // mxu_rows, accumulate, 0, unroll=True)

        def stores():
            return (
                pltpu.make_async_copy(dx_buf, dx_hbm.at[block_rows, :], out_sems.at[0]),
                pltpu.make_async_copy(dxb_buf, dxb_hbm.at[block_rows, :], out_sems.at[1]),
            )

        @pl.when(kk == nk - 1)
        def _():
            pltpu.make_async_copy(x_hbm.at[block_rows, :], x_buf, in_sems.at[0]).wait()
            pltpu.make_async_copy(dres_hbm.at[block_rows, :], dres_buf, in_sems.at[1]).wait()

            @pl.when(i > 0)
            def _():
                for cp in stores():
                    cp.wait()

            def step(j, carry):
                rs = pl.ds(pl.multiple_of(j * rows, rows), rows)
                dx, dgain = _rmsnorm_bwd(acc_ref[rs, :], x_buf[rs, :], gain_ref[...], dres_buf[rs, :])
                dx_buf[rs, :] = dx
                dxb_buf[rs, :] = dx.astype(BF16)
                dg_ref[...] += dgain
                return carry

            lax.fori_loop(0, tm // rows, step, 0)
            for cp in stores():
                cp.start()

            @pl.when(i == n_blocks - 1)
            def _():
                for cp in stores():
                    cp.wait()

    vec = pl.BlockSpec((1, d), lambda i, kk: (0, 0))
    return pl.pallas_call(
        _behind(body, deps),
        name=name,
        grid=(n_blocks, nk),
        in_specs=[ANY_SPEC] * len(deps) + [
            pl.BlockSpec((tm, tk), lambda i, kk: (i, kk)),
            pl.BlockSpec((None, d, tk), lambda i, kk: (kk // per_shard, 0, kk % per_shard)),
            vec,
            ANY_SPEC,
            ANY_SPEC,
        ],
        out_specs=[ANY_SPEC, ANY_SPEC, vec],
        out_shape=[_shape((s, d), F32), _shape((s, d), BF16), _shape((1, d), F32)],
        scratch_shapes=[
            pltpu.VMEM((tm, d), F32),
            pltpu.VMEM((tm, d), F32),
            pltpu.VMEM((tm, d), F32),
            pltpu.VMEM((tm, d), F32),
            pltpu.VMEM((tm, d), BF16),
            pltpu.SemaphoreType.DMA((2,)),
            pltpu.SemaphoreType.DMA((2,)),
        ],
        compiler_params=_compute_params("arbitrary", "arbitrary"),
    )(*deps, g, w, gain, x, dres)


def _loss_and_grad(x, gain, target, *, name):
    s, d = x.shape
    tm = _tile(s, 256, SUBLANES_BF16)

    def body(x_ref, gain_ref, t_ref, loss_ref, dx_ref, dxb_ref, dg_ref):
        @pl.when(pl.program_id(0) == 0)
        def _():
            loss_ref[...] = jnp.zeros_like(loss_ref)
            dg_ref[...] = jnp.zeros_like(dg_ref)

        xf = x_ref[...]
        r = lax.rsqrt(jnp.mean(xf * xf, axis=-1, keepdims=True) + EPS)
        err = (xf * r) * gain_ref[...] - t_ref[...]
        loss_ref[...] += 0.5 * jnp.sum(jnp.mean(err * err, axis=-1, keepdims=True))
        dx, dgain = _rmsnorm_bwd(err * (1.0 / d), xf, gain_ref[...], None)
        dx_ref[...] = dx
        dxb_ref[...] = dx.astype(BF16)
        dg_ref[...] += dgain

    row_tile = pl.BlockSpec((tm, d), lambda i: (i, 0))
    vec = pl.BlockSpec((1, d), lambda i: (0, 0))
    return pl.pallas_call(
        body,
        name=name,
        grid=(s // tm,),
        in_specs=[row_tile, vec, row_tile],
        out_specs=[pl.BlockSpec((1, LANES), lambda i: (0, 0)), row_tile, row_tile, vec],
        out_shape=[_shape((1, LANES), F32), _shape((s, d), F32), _shape((s, d), BF16), _shape((1, d), F32)],
        compiler_params=_compute_params("arbitrary"),
    )(x, gain, target)


def _trailing_sums(v_ext, window):
    acc, span = v_ext, 1
    while span < window:
        acc = acc + pltpu.roll(acc, span, 0)
        span *= 2
    return acc


def _leading_sums(q_ext, window):
    n = q_ext.shape[0]
    acc, span = q_ext, 1
    while span < window:
        acc = acc + pltpu.roll(acc, n - span, 0)
        span *= 2
    return acc


def _inverse_counts(first_token, rows, window):
    t = first_token + lax.broadcasted_iota(jnp.int32, (rows, 1), 0)
    return 1.0 / jnp.minimum(t + 1, window).astype(F32)


def _mixer_fwd(proj, conv_w, pool_w, pool_scale, *, name):
    s, _ = proj.shape
    dc = conv_w.shape[1]
    n_groups, cg, _ = pool_w.shape
    dp = n_groups * cg
    assert dc == dp and all(w & (w - 1) == 0 and w <= HALO for w in POOL_WINDOWS)
    ts = _tile(s, 256, HALO)
    halo_blocks = ts // HALO

    def body(b_ref, c_ref, xt_ref, v_ref, ch_ref, xth_ref, vh_ref, cw_ref, pw_ref, ps_ref, y_ref):
        i = pl.program_id(0)
        has_past = i > 0
        u_ext = jnp.concatenate(
            [
                jnp.where(has_past, ch_ref[...].astype(F32) * xth_ref[...].astype(F32), 0.0),
                c_ref[...].astype(F32) * xt_ref[...].astype(F32),
            ],
            axis=0,
        )
        conv = (
            cw_ref[2:3, :] * u_ext[HALO:]
            + cw_ref[1:2, :] * pltpu.roll(u_ext, 1, 0)[HALO:]
            + cw_ref[0:1, :] * pltpu.roll(u_ext, 2, 0)[HALO:]
        )
        y_ref[:, 0:dc] = (b_ref[...].astype(F32) * conv).astype(BF16)
        for gi, window in enumerate(POOL_WINDOWS):
            cols = slice(gi * cg, (gi + 1) * cg)
            v_ext = jnp.concatenate(
                [jnp.where(has_past, vh_ref[:, cols].astype(F32), 0.0), v_ref[:, cols].astype(F32)], axis=0
            )
            mean = _trailing_sums(v_ext, window)[HALO:] * _inverse_counts(i * ts, ts, window)
            diff = (mean - v_ext[HALO:]).astype(BF16)
            z = jnp.dot(diff, pw_ref[gi], preferred_element_type=F32)
            y_ref[:, dc + gi * cg : dc + (gi + 1) * cg] = (z * ps_ref[:, cols]).astype(BF16)

    def col(jc):
        return pl.BlockSpec((ts, dc), lambda i: (i, jc))

    def past(jc):
        return pl.BlockSpec((HALO, dc), lambda i: (jnp.maximum(i * halo_blocks - 1, 0), jc))

    return pl.pallas_call(
        body,
        name=name,
        grid=(s // ts,),
        in_specs=[
            col(0), col(1), col(2), col(3), past(1), past(2), past(3),
            pl.BlockSpec((8, dc), lambda i: (0, 0)),
            pl.BlockSpec((n_groups, cg, cg), lambda i: (0, 0, 0)),
            pl.BlockSpec((1, dp), lambda i: (0, 0)),
        ],
        out_specs=pl.BlockSpec((ts, dc + dp), lambda i: (i, 0)),
        out_shape=_shape((s, dc + dp), BF16),
        compiler_params=_compute_params("parallel"),
    )(proj, proj, proj, proj, proj, proj, proj, conv_w, pool_w, pool_scale)


def _mixer_bwd(dy, proj, conv_w, pool_w, pool_scale, *, name):
    s, e = proj.shape
    dc = conv_w.shape[1]
    n_groups, cg, _ = pool_w.shape
    dp = n_groups * cg
    ts = _tile(s, 256, HALO)
    halo_blocks = ts // HALO
    n_tiles = s // ts
    n_halo_blocks = s // HALO
    n_ext = ts + HALO

    def body(dyc_ref, dyp_ref, dycn_ref, dypn_ref, b_ref, c_ref, xt_ref, v_ref, bn_ref, ch_ref, xth_ref, vh_ref,
             cw_ref, pw_ref, ps_ref, dproj_ref, dcw_ref, dpw_ref, dps_ref):
        i = pl.program_id(0)
        has_past = i > 0
        has_next = i < n_tiles - 1

        @pl.when(i == 0)
        def _():
            dcw_ref[...] = jnp.zeros_like(dcw_ref)
            dpw_ref[...] = jnp.zeros_like(dpw_ref)
            dps_ref[...] = jnp.zeros_like(dps_ref)

        c_now, xt_now, b_now = c_ref[...].astype(F32), xt_ref[...].astype(F32), b_ref[...].astype(F32)
        u_ext = jnp.concatenate(
            [jnp.where(has_past, ch_ref[...].astype(F32) * xth_ref[...].astype(F32), 0.0), c_now * xt_now], axis=0
        )
        u0 = u_ext[HALO:]
        u1 = pltpu.roll(u_ext, 1, 0)[HALO:]
        u2 = pltpu.roll(u_ext, 2, 0)[HALO:]
        dyc = dyc_ref[...]
        conv = cw_ref[2:3, :] * u0 + cw_ref[1:2, :] * u1 + cw_ref[0:1, :] * u2
        dproj_ref[:, 0:dc] = (dyc * conv).astype(BF16)
        dconv = dyc * b_now
        dconv_ext = jnp.concatenate(
            [dconv, jnp.where(has_next, dycn_ref[...] * bn_ref[...].astype(F32), 0.0)], axis=0
        )
        du = (
            cw_ref[2:3, :] * dconv
            + cw_ref[1:2, :] * pltpu.roll(dconv_ext, n_ext - 1, 0)[:ts]
            + cw_ref[0:1, :] * pltpu.roll(dconv_ext, n_ext - 2, 0)[:ts]
        )
        dproj_ref[:, dc : 2 * dc] = (du * xt_now).astype(BF16)
        dproj_ref[:, 2 * dc : 3 * dc] = (du * c_now).astype(BF16)
        dcw_ref[0:1, :] += jnp.sum(dconv * u2, axis=0, keepdims=True)
        dcw_ref[1:2, :] += jnp.sum(dconv * u1, axis=0, keepdims=True)
        dcw_ref[2:3, :] += jnp.sum(dconv * u0, axis=0, keepdims=True)

        for gi, window in enumerate(POOL_WINDOWS):
            cols = slice(gi * cg, (gi + 1) * cg)
            v_ext = jnp.concatenate(
                [jnp.where(has_past, vh_ref[:, cols].astype(F32), 0.0), v_ref[:, cols].astype(F32)], axis=0
            )
            mean = _trailing_sums(v_ext, window)[HALO:] * _inverse_counts(i * ts, ts, window)
            diff = (mean - v_ext[HALO:]).astype(BF16)
            z = jnp.dot(diff, pw_ref[gi], preferred_element_type=F32)
            dyp = dyp_ref[:, cols]
            dps_ref[:, cols] += jnp.sum(dyp * z, axis=0, keepdims=True)
            scale = ps_ref[:, cols]
            dz_ext = jnp.concatenate([dyp * scale, jnp.where(has_next, dypn_ref[:, cols] * scale, 0.0)], axis=0)
            dz_ext = dz_ext.astype(BF16)
            dpw_ref[gi] += lax.dot_general(
                diff, dz_ext[:ts], (((0,), (0,)), ((), ())), preferred_element_type=F32
            )
            ddiff_ext = lax.dot_general(
                dz_ext, pw_ref[gi], (((1,), (1,)), ((), ())), preferred_element_type=F32
            )
            q_ext = ddiff_ext * _inverse_counts(i * ts, n_ext, window)
            dv = _leading_sums(q_ext, window)[:ts] - ddiff_ext[:ts]
            dproj_ref[:, 3 * dc + gi * cg : 3 * dc + (gi + 1) * cg] = dv.astype(BF16)

    def col(jc):
        return pl.BlockSpec((ts, dc), lambda i: (i, jc))

    def past(jc):
        return pl.BlockSpec((HALO, dc), lambda i: (jnp.maximum(i * halo_blocks - 1, 0), jc))

    def following(jc):
        return pl.BlockSpec((HALO, dc), lambda i: (jnp.minimum((i + 1) * halo_blocks, n_halo_blocks - 1), jc))

    return pl.pallas_call(
        body,
        name=name,
        grid=(n_tiles,),
        in_specs=[
            col(0), col(1), following(0), following(1),
            col(0), col(1), col(2), col(3), following(0), past(1), past(2), past(3),
            pl.BlockSpec((8, dc), lambda i: (0, 0)),
            pl.BlockSpec((n_groups, cg, cg), lambda i: (0, 0, 0)),
            pl.BlockSpec((1, dp), lambda i: (0, 0)),
        ],
        out_specs=[
            pl.BlockSpec((ts, e), lambda i: (i, 0)),
            pl.BlockSpec((8, dc), lambda i: (0, 0)),
            pl.BlockSpec((n_groups, cg, cg), lambda i: (0, 0, 0)),
            pl.BlockSpec((1, dp), lambda i: (0, 0)),
        ],
        out_shape=[_shape((s, e), BF16), _shape((8, dc), F32), _shape((n_groups, cg, cg), F32), _shape((1, dp), F32)],
        compiler_params=_compute_params("arbitrary"),
    )(dy, dy, dy, dy, proj, proj, proj, proj, proj, proj, proj, proj, conv_w, pool_w, pool_scale)


def _cast_bf16(a, *, name):
    rows, cols = a.shape
    tr = _tile(rows, 512, SUBLANES_BF16)

    def body(a_ref, o_ref):
        o_ref[...] = a_ref[...].astype(BF16)

    spec = pl.BlockSpec((tr, cols), lambda i: (i, 0))
    return pl.pallas_call(
        body, name=name, grid=(rows // tr,), in_specs=[spec], out_specs=spec, out_shape=_shape((rows, cols), BF16),
        compiler_params=_compute_params("parallel"),
    )(a)


def _core_index():
    return lax.axis_index("c").astype(jnp.int32).reshape((1,))


def _add_sibling_half(grads, received, *, name, deps=()):
    nsh, r, c = grads.shape
    hr = r // 2
    tr = _tile(hr, 512, SUBLANES_BF16)
    tiles = hr // tr

    def body(core_ref, *refs):
        g_ref, r_ref, o_ref = refs[len(deps):]
        o_ref[...] = (g_ref[...].astype(F32) + r_ref[...].astype(F32)).astype(BF16)

    half = pl.BlockSpec((None, tr, c), lambda sh, t, core: (sh, t, 0))
    return pl.pallas_call(
        body,
        name=name,
        grid_spec=pltpu.PrefetchScalarGridSpec(
            num_scalar_prefetch=1,
            grid=(nsh, tiles),
            in_specs=[ANY_SPEC] * len(deps)
            + [pl.BlockSpec((None, tr, c), lambda sh, t, core: (sh, core[0] * tiles + t, 0)), half],
            out_specs=half,
        ),
        out_shape=_shape((nsh, hr, c), BF16),
        compiler_params=_compute_params("parallel", "parallel"),
    )(_core_index(), *deps, grads, received)


def _chip_index_operand():
    return _chip_index(lax.axis_index("x"), lax.axis_index("y")).astype(jnp.int32).reshape((1,))


def _sum_chip_partials(own, received, *, name, deps=()):
    nsh, hr, c = received.shape
    tr = _tile(hr, 256, SUBLANES_BF16)

    def body(chip_ref, *refs):
        own_ref, p_ref, o_ref = refs[len(deps):]
        mine = chip_ref[0]
        total = None
        for sh in range(nsh):
            term = jnp.where(mine == sh, own_ref[...], p_ref[sh]).astype(F32)
            total = term if total is None else total + term
        o_ref[...] = total

    return pl.pallas_call(
        body,
        name=name,
        grid_spec=pltpu.PrefetchScalarGridSpec(
            num_scalar_prefetch=1,
            grid=(hr // tr,),
            in_specs=[ANY_SPEC] * len(deps) + [
                pl.BlockSpec((None, tr, c), lambda t, chip: (chip[0], t, 0)),
                pl.BlockSpec((nsh, tr, c), lambda t, chip: (0, t, 0)),
            ],
            out_specs=pl.BlockSpec((tr, c), lambda t, chip: (t, 0)),
        ),
        out_shape=_shape((hr, c), F32),
        compiler_params=_compute_params("parallel"),
    )(_chip_index_operand(), *deps, own, received)


def _adamw(grad, w, m, v, layer, carried, *, name, deps=()):
    n_layers, r, c = w.shape
    in_halves = isinstance(grad, (tuple, list))
    tr = _tile(r // 2 if in_halves else r, 256, 8)
    half_tiles = (r // 2) // tr if in_halves else 0
    bias1 = 1.0 - ADAM_B1 ** ADAM_STEP
    bias2 = 1.0 - ADAM_B2 ** ADAM_STEP
    n_grads = 2 if in_halves else 1

    def body(core_ref, *refs):
        refs = refs[len(deps):]
        w_ref, m_ref, v_ref = refs[n_grads : n_grads + 3]
        go_ref, d_ref, mo_ref, vo_ref, done_ref = refs[-5:]
        done_ref[...] = jnp.zeros_like(done_ref)
        if in_halves:
            in_my_half = pl.program_id(0) // half_tiles == core_ref[0]
            g = jnp.where(in_my_half, refs[0][...], refs[1][...])
        else:
            g = refs[0][...]
        m_new = ADAM_B1 * m_ref[...] + (1.0 - ADAM_B1) * g
        v_new = ADAM_B2 * v_ref[...] + (1.0 - ADAM_B2) * (g * g)
        go_ref[...] = g
        mo_ref[...] = m_new
        vo_ref[...] = v_new
        d_ref[...] = -ADAM_LR * ((m_new / bias1) / (jnp.sqrt(v_new / bias2) + ADAM_EPS) + ADAM_WD * w_ref[...])

    def half_spec(mine):
        def index(t, core):
            first = (core[0] if mine else 1 - core[0]) * half_tiles
            return (jnp.clip(t - first, 0, half_tiles - 1), 0)

        return pl.BlockSpec((tr, c), index)

    layer_tile = pl.BlockSpec((None, tr, c), lambda t, core: (layer, t, 0))
    if in_halves:
        grad_specs, grads = [half_spec(True), half_spec(False)], list(grad)
    else:
        grad_specs, grads = [pl.BlockSpec((tr, c), lambda t, core: (t, 0))], [grad]
    in_specs = [ANY_SPEC] * len(deps) + grad_specs + [layer_tile] * 3
    args = list(deps) + grads + [w, m, v]
    aliases = {}
    if carried is not None:
        aliases = {1 + len(args) + n: n for n in range(4)}
        in_specs += [HBM_SPEC] * 4
        args += list(carried)
    *outs, done = pl.pallas_call(
        body,
        name=name,
        grid_spec=pltpu.PrefetchScalarGridSpec(
            num_scalar_prefetch=1,
            grid=(r // tr,),
            in_specs=in_specs,
            out_specs=[layer_tile] * 4 + [pl.BlockSpec((8, LANES), lambda t, core: (0, 0))],
        ),
        out_shape=[_shape((n_layers, r, c), F32)] * 4 + [_shape((8, LANES), F32)],
        input_output_aliases=aliases,
        compiler_params=_compute_params("arbitrary"),
    )(_core_index(), *args)
    return outs, done


def _place():
    x, y, c = (lax.axis_index(a) for a in MESH_AXES)
    other_chips = [(1 - x, y), (x, 1 - y), (1 - x, 1 - y)]
    return x, y, c, other_chips


def _chip_index(x, y):
    return 2 * x + y


def _gather_small(conv_rows, pool_rows, *, name):
    blocks = (conv_rows, pool_rows)
    n = len(blocks)

    def body(*refs):
        ins, outs = refs[:n], refs[n : 2 * n]
        send_sems, recv_sems = refs[2 * n :]
        x, y, c, other_chips = _place()
        mine = _chip_index(x, y)
        peers = [(x, y, 1 - c)] + [(px, py, c) for px, py in other_chips]
        _handshake(peers)
        sends = []
        for a in range(n):
            for j, peer in enumerate(peers):
                cp = pltpu.make_async_remote_copy(
                    src_ref=ins[a], dst_ref=outs[a].at[mine], send_sem=send_sems.at[a, j], recv_sem=recv_sems.at[a, j],
                    device_id=peer, device_id_type=MESH_ID,
                )
                cp.start()
                sends.append(cp)
        for a in range(n):
            for j, (px, py, _) in enumerate(peers):
                pltpu.make_async_remote_copy(
                    src_ref=ins[a], dst_ref=outs[a].at[_chip_index(px, py)], send_sem=send_sems.at[a, j],
                    recv_sem=recv_sems.at[a, j], device_id=peers[j], device_id_type=MESH_ID,
                ).wait_recv()
        for cp in sends:
            cp.wait_send()

    return _sequencer_call(
        body,
        [_shape((N_CHIPS,) + blk.shape, F32) for blk in blocks],
        [pltpu.SemaphoreType.DMA((n, 4)), pltpu.SemaphoreType.DMA((n, 4))],
        GATHER_SMALL_COLLECTIVE_ID,
        name,
    )(*blocks)


def _allreduce_small(vec, *, name):
    rows, n = vec.shape
    n_dev = 8

    def body(v_ref, o_ref, slots, send_sems, recv_sems):
        x, y, c, _ = _place()
        me = 4 * x + 2 * y + c
        slots[me] = v_ref[...]
        sends = []
        for mask in range(1, n_dev):
            fx, fy, fc = (mask >> 2) & 1, (mask >> 1) & 1, mask & 1
            peer = (x ^ fx, y ^ fy, c ^ fc)
            cp = pltpu.make_async_remote_copy(
                src_ref=v_ref, dst_ref=slots.at[me], send_sem=send_sems.at[mask - 1], recv_sem=recv_sems.at[mask - 1],
                device_id=peer, device_id_type=MESH_ID,
            )
            cp.start()
            sends.append(cp)
        for mask in range(1, n_dev):
            fx, fy, fc = (mask >> 2) & 1, (mask >> 1) & 1, mask & 1
            peer = (x ^ fx, y ^ fy, c ^ fc)
            pltpu.make_async_remote_copy(
                src_ref=v_ref, dst_ref=slots.at[4 * peer[0] + 2 * peer[1] + peer[2]], send_sem=send_sems.at[mask - 1],
                recv_sem=recv_sems.at[mask - 1], device_id=peer, device_id_type=MESH_ID,
            ).wait_recv()
        total = slots[0]
        for dev in range(1, n_dev):
            total = total + slots[dev]
        o_ref[...] = total
        for cp in sends:
            cp.wait_send()

    return pl.pallas_call(
        body,
        name=name,
        in_specs=[VMEM_SPEC],
        out_specs=VMEM_SPEC,
        out_shape=_shape((rows, n), F32),
        scratch_shapes=[
            pltpu.VMEM((n_dev, rows, n), F32),
            pltpu.SemaphoreType.DMA((n_dev - 1,)),
            pltpu.SemaphoreType.DMA((n_dev - 1,)),
        ],
    )(vec)


def _handshake(peers):
    barrier = pltpu.get_barrier_semaphore()
    for peer in peers:
        pl.semaphore_signal(barrier, inc=1, device_id=peer, device_id_type=MESH_ID)
    pl.semaphore_wait(barrier, len(peers))


def _sequencer_call(body, out_type, scratch_types, collective_id, name):
    return pl.kernel(
        body,
        name=name,
        out_type=out_type,
        mesh=plsc.ScalarSubcoreMesh(axis_name="sequencer", num_cores=1),
        scratch_types=scratch_types,
        compiler_params=pltpu.CompilerParams(collective_id=collective_id),
    )


GATHER_COLLECTIVE_ID = 1
EXCHANGE_COLLECTIVE_ID = 2
SCATTER_COLLECTIVE_ID = 3
SHARE_COLLECTIVE_ID = 4
GATHER_SMALL_COLLECTIVE_ID = 5


def _gather_weights(stacked, layer, *, name):
    n = len(stacked)

    def body(*refs):
        ins, outs = refs[:n], refs[n : 2 * n]
        own_sems, send_sems, recv_sems, pass_send_sems, pass_recv_sems = refs[2 * n :]
        x, y, c, other_chips = _place()
        mine = _chip_index(x, y)
        sibling = (x, y, 1 - c)
        _handshake([sibling] + [(px, py, c) for px, py in other_chips])
        pending = []

        def send_my_half(a):
            hr = ins[a].shape[1] // 2
            my_half = pl.ds(c * hr, hr)
            cp = pltpu.make_async_remote_copy(
                src_ref=ins[a].at[layer], dst_ref=outs[a].at[mine], send_sem=own_sems.at[0, a],
                recv_sem=own_sems.at[1, a], device_id=sibling, device_id_type=MESH_ID,
            )
            cp.start()
            pending.append(cp.wait)
            for j, (px, py) in enumerate(other_chips):
                cp = pltpu.make_async_remote_copy(
                    src_ref=ins[a].at[layer, my_half], dst_ref=outs[a].at[mine, my_half], send_sem=send_sems.at[a, j],
                    recv_sem=recv_sems.at[a, j], device_id=(px, py, c), device_id_type=MESH_ID,
                )
                cp.start()
                pending.append(cp.wait_send)

        for a in range(min(2, n)):
            send_my_half(a)
        for a in range(n):
            hr = ins[a].shape[1] // 2
            my_half = pl.ds(c * hr, hr)
            for j, (px, py) in enumerate(other_chips):
                landed = outs[a].at[_chip_index(px, py), my_half]
                pltpu.make_async_remote_copy(
                    src_ref=landed, dst_ref=landed, send_sem=send_sems.at[a, j], recv_sem=recv_sems.at[a, j],
                    device_id=(px, py, c), device_id_type=MESH_ID,
                ).wait_recv()
                cp = pltpu.make_async_remote_copy(
                    src_ref=landed, dst_ref=landed, send_sem=pass_send_sems.at[a, j], recv_sem=pass_recv_sems.at[a, j],
                    device_id=sibling, device_id_type=MESH_ID,
                )
                cp.start()
                pending.append(cp.wait_send)
            if a + 2 < n:
                send_my_half(a + 2)
        for a in range(n):
            hr = ins[a].shape[1] // 2
            sibling_half = pl.ds((1 - c) * hr, hr)
            for j, (px, py) in enumerate(other_chips):
                passed = outs[a].at[_chip_index(px, py), sibling_half]
                pltpu.make_async_remote_copy(
                    src_ref=passed, dst_ref=passed, send_sem=pass_send_sems.at[a, j], recv_sem=pass_recv_sems.at[a, j],
                    device_id=sibling, device_id_type=MESH_ID,
                ).wait_recv()
        for wait in pending:
            wait()

    return _sequencer_call(
        body,
        [_shape((N_CHIPS,) + a.shape[1:], BF16) for a in stacked],
        [pltpu.SemaphoreType.DMA((2, n))] + [pltpu.SemaphoreType.DMA((n, 3))] * 4,
        GATHER_COLLECTIVE_ID,
        name,
    )(*stacked)


def _exchange_halves(grads, *, name):
    n = len(grads)

    def body(*refs):
        ins, outs = refs[:n], refs[n : 2 * n]
        send_sems, recv_sems = refs[2 * n :]
        x, y, c, _ = _place()
        sibling = (x, y, 1 - c)
        _handshake([sibling])
        copies = []
        for a in range(n):
            hr = ins[a].shape[1] // 2
            cp = pltpu.make_async_remote_copy(
                src_ref=ins[a].at[:, pl.ds((1 - c) * hr, hr), :], dst_ref=outs[a], send_sem=send_sems.at[a],
                recv_sem=recv_sems.at[a], device_id=sibling, device_id_type=MESH_ID,
            )
            cp.start()
            copies.append(cp)
        for cp in copies:
            cp.wait()

    return _sequencer_call(
        body,
        [_shape((g.shape[0], g.shape[1] // 2, g.shape[2]), BF16) for g in grads],
        [pltpu.SemaphoreType.DMA((n,)), pltpu.SemaphoreType.DMA((n,))],
        EXCHANGE_COLLECTIVE_ID,
        name,
    )(*grads)


def _scatter_partials(partials, *, name):
    n = len(partials)

    def body(*refs):
        ins, outs = refs[:n], refs[n : 2 * n]
        send_sems, recv_sems = refs[2 * n :]
        x, y, c, other_chips = _place()
        mine = _chip_index(x, y)
        _handshake([(px, py, c) for px, py in other_chips])
        pending = []
        for a in range(n):
            for j, (px, py) in enumerate(other_chips):
                cp = pltpu.make_async_remote_copy(
                    src_ref=ins[a].at[_chip_index(px, py)], dst_ref=outs[a].at[mine], send_sem=send_sems.at[a, j],
                    recv_sem=recv_sems.at[a, j], device_id=(px, py, c), device_id_type=MESH_ID,
                )
                cp.start()
                pending.append(cp.wait_send)
        for a in range(n):
            for j, (px, py) in enumerate(other_chips):
                landed = outs[a].at[_chip_index(px, py)]
                pltpu.make_async_remote_copy(
                    src_ref=landed, dst_ref=landed, send_sem=send_sems.at[a, j], recv_sem=recv_sems.at[a, j],
                    device_id=(px, py, c), device_id_type=MESH_ID,
                ).wait_recv()
        for wait in pending:
            wait()

    return _sequencer_call(
        body,
        [_shape(p.shape, BF16) for p in partials],
        [pltpu.SemaphoreType.DMA((n, 3)), pltpu.SemaphoreType.DMA((n, 3))],
        SCATTER_COLLECTIVE_ID,
        name,
    )(*partials)


def _share_halves(halves, *, name):
    n = len(halves)

    def body(*refs):
        ins, outs = refs[:n], refs[n : 2 * n]
        send_sems, recv_sems = refs[2 * n :]
        x, y, c, _ = _place()
        sibling = (x, y, 1 - c)
        _handshake([sibling])
        copies = []
        for a in range(n):
            cp = pltpu.make_async_remote_copy(
                src_ref=ins[a], dst_ref=outs[a], send_sem=send_sems.at[a], recv_sem=recv_sems.at[a],
                device_id=sibling, device_id_type=MESH_ID,
            )
            cp.start()
            copies.append(cp)
        for cp in copies:
            cp.wait()

    return _sequencer_call(
        body,
        [_shape(h.shape, F32) for h in halves],
        [pltpu.SemaphoreType.DMA((n,)), pltpu.SemaphoreType.DMA((n,))],
        SHARE_COLLECTIVE_ID,
        name,
    )(*halves)


class _ReduceToOwner:
    def __init__(self, grads, tag):
        self.grads, self.tag = grads, tag
        self.received = _exchange_halves(grads, name=f"exchange_halves_{tag}")

    def add_sibling(self, after):
        partials = [
            _add_sibling_half(g, r, name=f"add_sibling_{self.tag}_{a}", deps=after)
            for a, (g, r) in enumerate(zip(self.grads, self.received))
        ]
        self.partials = partials
        self.slots = _scatter_partials(partials, name=f"scatter_partials_{self.tag}")
        return partials

    def sum_chips(self, after):
        halves = [
            _sum_chip_partials(p, s, name=f"sum_partials_{self.tag}_{a}", deps=after)
            for a, (p, s) in enumerate(zip(self.partials, self.slots))
        ]
        self.halves = halves
        self.theirs = _share_halves(halves, name=f"share_halves_{self.tag}")
        return halves

    def totals(self):
        return list(zip(self.halves, self.theirs))


def kernel(x, w_in, conv_w, pool_w, pool_scale, w_out, norm_mix, norm_mlp, w_up, w_down, norm_final, loss_target, m_w_in, m_conv_w, m_pool_w, m_pool_scale, m_w_out, m_norm_mix, m_norm_mlp, m_w_up, m_w_down, m_norm_final, v_w_in, v_conv_w, v_pool_w, v_pool_scale, v_w_out, v_norm_mix, v_norm_mlp, v_w_up, v_w_down, v_norm_final):
    n_layers, d, _ = w_in.shape
    s = x.shape[1]
    dc = conv_w.shape[2] * N_CHIPS
    n_groups, cg_rows, cg = pool_w.shape[1:]
    dp = n_groups * cg
    x0 = x.reshape(s, d)
    target = loss_target.reshape(s, d)

    big = [w_in, w_out, w_up, w_down]
    big_bf16 = [
        _cast_bf16(w.reshape(-1, w.shape[2]), name=f"cast_{t}").reshape(w.shape)
        for t, w in zip(("w_in", "w_out", "w_up", "w_down"), big)
    ]

    conv_rows = jnp.pad(conv_w, ((0, 0), (0, 8 - CONV_TAPS), (0, 0))).reshape(n_layers * 8, -1)
    pool_rows = pool_w.reshape(n_layers * n_groups * cg_rows, cg)
    conv_all, pool_all = _gather_small(conv_rows, pool_rows, name="gather_small")
    conv_full = conv_all.reshape(N_CHIPS, n_layers, 8, -1).transpose(1, 2, 0, 3).reshape(n_layers, 8, dc)
    pool_full = (
        pool_all.reshape(N_CHIPS, n_layers, n_groups, cg_rows, cg).transpose(1, 2, 0, 3, 4)
        .reshape(n_layers, n_groups, cg, cg).astype(BF16)
    )

    saved = []
    xl = x0
    for l in range(n_layers):
        win_g, wout_g, wup_g, wdown_g = [
            _gather_weights([w], l, name=f"gather_weights_l{l}_{t}")[0] for t, w in enumerate(big_bf16)
        ]
        gain_mix = norm_mix[l].reshape(1, d)
        gain_mlp = norm_mlp[l].reshape(1, d)
        scale = pool_scale[l].reshape(1, dp)
        h1 = _rmsnorm(xl, gain_mix, name=f"norm_mix_l{l}")
        proj = _matmul_cols(h1, win_g, relu2=False, name=f"in_proj_l{l}")
        y = _mixer_fwd(proj, conv_full[l], pool_full[l], scale, name=f"mixer_fwd_l{l}")
        x_mid = _matmul_residual(y, wout_g.reshape(-1, d), xl, name=f"out_proj_l{l}")
        h2 = _rmsnorm(x_mid, gain_mlp, name=f"norm_mlp_l{l}")
        u, u2 = _matmul_cols(h2, wup_g, relu2=True, name=f"up_proj_l{l}")
        x_next = _matmul_residual(u2, wdown_g.reshape(-1, d), x_mid, name=f"down_proj_l{l}")
        saved.append((xl, h1, proj, y, x_mid, h2, u, u2, win_g, wout_g, wup_g, wdown_g, gain_mix, gain_mlp, scale))
        xl = x_next

    loss_part, dx, dx_bf16, d_norm_final = _loss_and_grad(xl, norm_final.reshape(1, d), target, name="loss_head")
    loss = lax.psum(loss_part[0, 0], MESH_AXES)

    small_grads = [None] * n_layers
    carried = [None] * 5
    pool_params = tuple(p.reshape(n_layers, n_groups * cg_rows, cg) for p in (pool_w, m_pool_w, v_pool_w))
    params = [(w_in, m_w_in, v_w_in), (w_out, m_w_out, v_w_out), (w_up, m_w_up, v_w_up), (w_down, m_w_down, v_w_down),
              pool_params]
    DOWN, UP_OUT, IN_POOL = (3,), (2, 1), (0, 4)

    def update(reduce, which, layer, after):
        dones = []
        for a, total in zip(which, reduce.totals()):
            w, m, v = params[a]
            carried[a], done = _adamw(total, w, m, v, layer, carried[a], name=f"adamw_{a}_l{layer}", deps=after)
            dones.append(done)
        return dones

    up_out_above = in_pool_above = None
    for l in reversed(range(n_layers)):
        xl, h1, proj, y, x_mid, h2, u, u2, win_g, wout_g, wup_g, wdown_g, gain_mix, gain_mlp, scale = saved[l]
        above = up_out_above is not None
        deps = in_pool_above.add_sibling([dx_bf16]) if above else []
        g_down = _matmul_tn(u2, dx_bf16, n_shards=N_CHIPS, shard_cols=False, name=f"grad_w_down_l{l}", deps=deps)
        down = _ReduceToOwner([g_down], f"down_l{l}")
        deps = up_out_above.sum_chips([g_down]) if above else [g_down]
        da = _matmul_nt(dx_bf16, wdown_g.reshape(-1, d), u=u, name=f"grad_act_l{l}", deps=deps)
        g_up = _matmul_tn(
            h2, da, n_shards=N_CHIPS, shard_cols=True, name=f"grad_w_up_l{l}", deps=down.add_sibling([da])
        )
        deps = in_pool_above.sum_chips([g_up]) if above else [g_up]
        dx_mid, dx_mid_bf16, d_gain_mlp = _matmul_nt_norm_bwd(
            da, wup_g, x_mid, gain_mlp, dx, name=f"grad_mid_l{l}", deps=deps
        )
        g_out = _matmul_tn(
            y, dx_mid_bf16, n_shards=N_CHIPS, shard_cols=False, name=f"grad_w_out_l{l}",
            deps=down.sum_chips([dx_mid_bf16]),
        )
        up_out = _ReduceToOwner([g_up, g_out], f"up_out_l{l}")
        deps = update(up_out_above, UP_OUT, l + 1, [g_out]) if above else [g_out]
        dy = _matmul_nt(dx_mid_bf16, wout_g.reshape(-1, d), name=f"grad_mixed_l{l}", deps=deps)
        dproj, d_conv, d_pool, d_scale = _mixer_bwd(dy, proj, conv_full[l], pool_full[l], scale, name=f"mixer_bwd_l{l}")
        g_in = _matmul_tn(
            h1, dproj, n_shards=N_CHIPS, shard_cols=True, name=f"grad_w_in_l{l}", deps=up_out.add_sibling([dproj])
        )
        g_pool = (
            d_pool.reshape(n_groups, N_CHIPS, cg_rows, cg).transpose(1, 0, 2, 3)
            .reshape(N_CHIPS, n_groups * cg_rows, cg).astype(BF16)
        )
        in_pool = _ReduceToOwner([g_in, g_pool], f"in_pool_l{l}")
        deps = update(down, DOWN, l, [g_in])
        if above:
            deps = deps + update(in_pool_above, IN_POOL, l + 1, [g_in])
        dx, dx_bf16, d_gain_mix = _matmul_nt_norm_bwd(
            dproj, win_g, xl, gain_mix, dx_mid, name=f"grad_x_l{l}", deps=deps
        )
        small_grads[l] = jnp.concatenate(
            [d_conv[:CONV_TAPS].reshape(-1), d_scale.reshape(-1), d_gain_mix.reshape(-1), d_gain_mlp.reshape(-1)]
        )
        up_out_above, in_pool_above = up_out, in_pool
    deps = in_pool_above.add_sibling([dx_bf16])
    deps = up_out_above.sum_chips(deps)
    deps = in_pool_above.sum_chips(deps)
    deps = update(up_out_above, UP_OUT, 0, deps)
    update(in_pool_above, IN_POOL, 0, deps)

    vec = jnp.concatenate(small_grads + [d_norm_final.reshape(-1)])
    vec = _allreduce_small(vec.reshape(8, -1), name="allreduce_small").reshape(-1)
    per_layer = vec[: n_layers * (CONV_TAPS * dc + dp + 2 * d)].reshape(n_layers, -1)
    chip = _chip_index(lax.axis_index("x"), lax.axis_index("y"))
    dcs = dc // N_CHIPS
    g_conv = lax.dynamic_slice_in_dim(per_layer[:, : CONV_TAPS * dc].reshape(n_layers, CONV_TAPS, dc), chip * dcs, dcs, axis=2)
    g_scale = per_layer[:, CONV_TAPS * dc : CONV_TAPS * dc + dp]
    g_mix = per_layer[:, CONV_TAPS * dc + dp : CONV_TAPS * dc + dp + d]
    g_mlp = per_layer[:, CONV_TAPS * dc + dp + d :]
    g_final = vec[n_layers * (CONV_TAPS * dc + dp + 2 * d) :]

    def small_adamw(g, w, m, v, tag):
        flat = lambda t: t.reshape(1, -1, t.shape[-1])
        out, _ = _adamw(flat(g)[0], flat(w), flat(m), flat(v), 0, None, name=f"adamw_{tag}")
        return [o.reshape(w.shape) for o in out]

    o_conv = small_adamw(g_conv, conv_w, m_conv_w, v_conv_w, "conv_w")
    o_scale = small_adamw(g_scale, pool_scale, m_pool_scale, v_pool_scale, "pool_scale")
    o_mix = small_adamw(g_mix, norm_mix, m_norm_mix, v_norm_mix, "norm_mix")
    o_mlp = small_adamw(g_mlp, norm_mlp, m_norm_mlp, v_norm_mlp, "norm_mlp")
    o_final = small_adamw(g_final, norm_final, m_norm_final, v_norm_final, "norm_final")
    o_in, o_out, o_up, o_down, o_pool = carried
    o_pool = [o.reshape(pool_w.shape) for o in o_pool]

    ordered = [o_in, o_conv, o_pool, o_scale, o_out, o_mix, o_mlp, o_up, o_down, o_final]
    return (loss, dx.reshape(x.shape), *[o[0] for o in ordered], *[o[1] for o in ordered], *[o[2] for o in ordered],
            *[o[3] for o in ordered])
```

```python
import functools

import jax
import jax.numpy as jnp
from jax import lax
from jax.experimental import pallas as pl
from jax.experimental.pallas import tpu as pltpu
from jax.experimental.pallas import tpu_sc as plsc

F32 = jnp.float32
BF16 = jnp.bfloat16

EPS = 1e-6
POOL_WINDOWS = (2, 4, 8, 16)
CONV_TAPS = 3
HALO = 16

ADAM_LR = 0.001
ADAM_B1 = 0.9
ADAM_B2 = 0.999
ADAM_EPS = 1e-08
ADAM_WD = 0.01
ADAM_STEP = 10

N_CHIPS = 4
MESH_AXES = ("x", "y", "c")
V7X_VMEM_LIMIT_BYTES = 56 * 1024 * 1024
SUBLANES_BF16 = 16
LANES = 128

HBM_SPEC = pl.BlockSpec(memory_space=pltpu.HBM)
VMEM_SPEC = pl.BlockSpec(memory_space=pltpu.VMEM)
MESH_ID = pl.DeviceIdType.MESH


def _tile(dim, target, align):
    if dim <= target:
        return dim
    t = (target // align) * align
    while dim % t:
        t -= align
    assert t > 0, (dim, target, align)
    return t


def _compute_params(*semantics):
    return pltpu.CompilerParams(dimension_semantics=semantics, vmem_limit_bytes=V7X_VMEM_LIMIT_BYTES)


def _shape(shape, dtype):
    return jax.ShapeDtypeStruct(shape, dtype)


ANY_SPEC = pl.BlockSpec(memory_space=pl.ANY)


def _behind(body, deps):
    return lambda *refs: body(*refs[len(deps):])


def _rmsnorm_bwd(dh, x, gain, dres):
    r = lax.rsqrt(jnp.mean(x * x, axis=-1, keepdims=True) + EPS)
    xn = x * r
    dgain = jnp.sum(dh * xn, axis=0, keepdims=True)
    dxn = dh * gain
    dx = r * (dxn - xn * jnp.mean(dxn * xn, axis=-1, keepdims=True))
    if dres is not None:
        dx = dx + dres
    return dx, dgain


MXU_ROWS = 512


def _rmsnorm(x, gain, *, name):
    s, d = x.shape
    tm = _tile(s, 512, SUBLANES_BF16)

    def body(x_ref, g_ref, h_ref):
        xf = x_ref[...]
        r = lax.rsqrt(jnp.mean(xf * xf, axis=-1, keepdims=True) + EPS)
        h_ref[...] = ((xf * r) * g_ref[...]).astype(BF16)

    row_tile = pl.BlockSpec((tm, d), lambda i: (i, 0))
    return pl.pallas_call(
        body, name=name, grid=(s // tm,), in_specs=[row_tile, pl.BlockSpec((1, d), lambda i: (0, 0))],
        out_specs=row_tile, out_shape=_shape((s, d), BF16), compiler_params=_compute_params("parallel"),
    )(x, gain)


def _matmul_cols(h, w, *, relu2, name):
    s, k = h.shape
    nsh, _, c = w.shape
    tm = _tile(s, 2048, MXU_ROWS)
    tn = _tile(c, 1024, LANES)
    rows = min(MXU_ROWS, tm)
    per_shard = c // tn

    def body(h_ref, w_ref, *out_refs):
        def step(t, carry):
            rs = pl.ds(pl.multiple_of(t * rows, rows), rows)
            acc = jnp.dot(h_ref[rs, :], w_ref[...], preferred_element_type=F32)
            if relu2:
                u = jnp.maximum(acc, 0.0)
                out_refs[0][rs, :] = u.astype(BF16)
                out_refs[1][rs, :] = (u * u).astype(BF16)
            else:
                out_refs[0][rs, :] = acc.astype(BF16)
            return carry

        lax.fori_loop(0, tm // rows, step, 0)

    tile_out = pl.BlockSpec((tm, tn), lambda i, j: (i, j))
    n = nsh * c
    outs = [_shape((s, n), BF16)] * (2 if relu2 else 1)
    result = pl.pallas_call(
        body,
        name=name,
        grid=(s // tm, n // tn),
        in_specs=[
            pl.BlockSpec((tm, k), lambda i, j: (i, 0)),
            pl.BlockSpec((None, k, tn), lambda i, j: (j // per_shard, 0, j % per_shard)),
        ],
        out_specs=[tile_out] * len(outs),
        out_shape=outs,
        compiler_params=_compute_params("parallel", "arbitrary"),
    )(h, w)
    return result if relu2 else result[0]


def _matmul_residual(a, w, res, *, name):
    s, k = a.shape
    _, n = w.shape
    tm = _tile(s, 1024, MXU_ROWS)
    tn = _tile(n, 1024, LANES)
    tk = _tile(k, 2048, LANES)
    rows = min(MXU_ROWS, tm)

    def body(a_ref, w_ref, r_ref, o_ref):
        kk = pl.program_id(2)

        @pl.when(kk == 0)
        def _():
            o_ref[...] = r_ref[...]

        def step(t, carry):
            rs = pl.ds(pl.multiple_of(t * rows, rows), rows)
            o_ref[rs, :] += jnp.dot(a_ref[rs, :], w_ref[...], preferred_element_type=F32)
            return carry

        lax.fori_loop(0, tm // rows, step, 0, unroll=True)

    return pl.pallas_call(
        body,
        name=name,
        grid=(s // tm, n // tn, k // tk),
        in_specs=[
            pl.BlockSpec((tm, tk), lambda i, j, kk: (i, kk)),
            pl.BlockSpec((tk, tn), lambda i, j, kk: (kk, j)),
            pl.BlockSpec((tm, tn), lambda i, j, kk: (i, j)),
        ],
        out_specs=pl.BlockSpec((tm, tn), lambda i, j, kk: (i, j)),
        out_shape=_shape((s, n), F32),
        compiler_params=_compute_params("parallel", "parallel", "arbitrary"),
    )(a, w, res)


def _matmul_tn(a, g, *, n_shards, shard_cols, name, deps=()):
    m, kd = a.shape
    _, n = g.shape
    r, c = (kd, n // n_shards) if shard_cols else (kd // n_shards, n)
    tr = _tile(kd, 2048, LANES)
    tn = _tile(c, 1024, LANES)
    tm = _tile(m, 2048, SUBLANES_BF16)
    nm = m // tm
    cols_per_shard = c // tn

    out_rows = min(MXU_ROWS, tr)

    def body(a_ref, g_ref, o_ref, acc_ref):
        mm = pl.program_id(2)

        @pl.when(mm == 0)
        def _():
            acc_ref[...] = jnp.zeros_like(acc_ref)

        for t in range(tr // out_rows):
            rs = slice(t * out_rows, (t + 1) * out_rows)
            acc_ref[rs, :] += lax.dot_general(
                a_ref[:, rs], g_ref[...], (((0,), (0,)), ((), ())), preferred_element_type=F32
            )

        @pl.when(mm == nm - 1)
        def _():
            o_ref[...] = acc_ref[...].astype(BF16)

    if shard_cols:
        out_spec = pl.BlockSpec((None, tr, tn), lambda i, j, mm: (j // cols_per_shard, i, j % cols_per_shard))
        out_shape = _shape((n_shards, kd, c), BF16)
    else:
        out_spec = pl.BlockSpec((tr, tn), lambda i, j, mm: (i, j))
        out_shape = _shape((kd, n), BF16)
    out = pl.pallas_call(
        _behind(body, deps),
        name=name,
        grid=(kd // tr, n // tn, nm),
        in_specs=[ANY_SPEC] * len(deps) + [
            pl.BlockSpec((tm, tr), lambda i, j, mm: (mm, i)),
            pl.BlockSpec((tm, tn), lambda i, j, mm: (mm, j)),
        ],
        out_specs=out_spec,
        out_shape=out_shape,
        scratch_shapes=[pltpu.VMEM((tr, tn), F32)],
        compiler_params=_compute_params("parallel", "parallel", "arbitrary"),
    )(*deps, a, g)
    return out.reshape(n_shards, r, c)


def _matmul_nt(g, w, *, u=None, name, deps=()):
    m, n = g.shape
    kd, _ = w.shape
    tm = _tile(m, 2048, MXU_ROWS)
    tj = _tile(kd, 1024, LANES)
    rows = min(MXU_ROWS, tm)

    def body(*refs):
        if u is None:
            g_ref, w_ref, o_ref = refs
        else:
            g_ref, w_ref, u_ref, o_ref = refs

        def step(t, carry):
            rs = pl.ds(pl.multiple_of(t * rows, rows), rows)
            prod = lax.dot_general(g_ref[rs, :], w_ref[...], (((1,), (1,)), ((), ())), preferred_element_type=F32)
            if u is None:
                o_ref[rs, :] = prod
            else:
                o_ref[rs, :] = (prod * (2.0 * u_ref[rs, :].astype(F32))).astype(BF16)
            return carry

        lax.fori_loop(0, tm // rows, step, 0)

    tile_out = pl.BlockSpec((tm, tj), lambda i, j: (i, j))
    in_specs = [pl.BlockSpec((tm, n), lambda i, j: (i, 0)), pl.BlockSpec((tj, n), lambda i, j: (j, 0))]
    args = [g, w]
    if u is not None:
        in_specs.append(tile_out)
        args.append(u)
    return pl.pallas_call(
        _behind(body, deps),
        name=name,
        grid=(m // tm, kd // tj),
        in_specs=[ANY_SPEC] * len(deps) + in_specs,
        out_specs=tile_out,
        out_shape=_shape((m, kd), F32 if u is None else BF16),
        compiler_params=_compute_params("parallel", "arbitrary"),
    )(*deps, *args)


EPILOGUE_ROWS = 128


def _matmul_nt_norm_bwd(g, w, x, gain, dres, *, name, deps=()):
    s, n = g.shape
    nsh, d, c = w.shape
    tm = _tile(s, 1024, MXU_ROWS)
    tk = _tile(c, 1024, LANES)
    per_shard = c // tk
    nk = n // tk
    n_blocks = s // tm
    mxu_rows = min(MXU_ROWS, tm)
    rows = min(EPILOGUE_ROWS, tm)
    piece = tm // nk
    assert piece * nk == tm and piece % 8 == 0

    def body(g_ref, w_ref, gain_ref, x_hbm, dres_hbm, dx_hbm, dxb_hbm, dg_ref, acc_ref, x_buf, dres_buf, dx_buf,
             dxb_buf, in_sems, out_sems):
        i = pl.program_id(0)
        kk = pl.program_id(1)

        @pl.when((kk == 0) & (i == 0))
        def _():
            dg_ref[...] = jnp.zeros_like(dg_ref)

        block_rows = pl.ds(pl.multiple_of(i * tm, tm), tm)
        pieces = pl.ds(pl.multiple_of(kk * piece, piece), piece)
        piece_rows = pl.ds(pl.multiple_of(i * tm + kk * piece, piece), piece)
        pltpu.make_async_copy(x_hbm.at[piece_rows, :], x_buf.at[pieces, :], in_sems.at[0]).start()
        pltpu.make_async_copy(dres_hbm.at[piece_rows, :], dres_buf.at[pieces, :], in_sems.at[1]).start()

        @pl.when(kk == 0)
        def _():
            acc_ref[...] = jnp.zeros_like(acc_ref)

        def accumulate(t, carry):
            rs = pl.ds(pl.multiple_of(t * mxu_rows, mxu_rows), mxu_rows)
            acc_ref[rs, :] += lax.dot_general(
                g_ref[rs, :], w_ref[...], (((1,), (1,)), ((), ())), preferred_element_type=F32
            )
            return carry

        lax.fori_loop(0, tm // mxu_rows, accumulate, 0, unroll=True)

        def stores():
            return (
                pltpu.make_async_copy(dx_buf, dx_hbm.at[block_rows, :], out_sems.at[0]),
                pltpu.make_async_copy(dxb_buf, dxb_hbm.at[block_rows, :], out_sems.at[1]),
            )

        @pl.when(kk == nk - 1)
        def _():
            pltpu.make_async_copy(x_hbm.at[block_rows, :], x_buf, in_sems.at[0]).wait()
            pltpu.make_async_copy(dres_hbm.at[block_rows, :], dres_buf, in_sems.at[1]).wait()

            @pl.when(i > 0)
            def _():
                for cp in stores():
                    cp.wait()

            def step(j, carry):
                rs = pl.ds(pl.multiple_of(j * rows, rows), rows)
                dx, dgain = _rmsnorm_bwd(acc_ref[rs, :], x_buf[rs, :], gain_ref[...], dres_buf[rs, :])
                dx_buf[rs, :] = dx
                dxb_buf[rs, :] = dx.astype(BF16)
                dg_ref[...] += dgain
                return carry

            lax.fori_loop(0, tm // rows, step, 0)
            for cp in stores():
                cp.start()

            @pl.when(i == n_blocks - 1)
            def _():
                for cp in stores():
                    cp.wait()

    vec = pl.BlockSpec((1, d), lambda i, kk: (0, 0))
    return pl.pallas_call(
        _behind(body, deps),
        name=name,
        grid=(n_blocks, nk),
        in_specs=[ANY_SPEC] * len(deps) + [
            pl.BlockSpec((tm, tk), lambda i, kk: (i, kk)),
            pl.BlockSpec((None, d, tk), lambda i, kk: (kk // per_shard, 0, kk % per_shard)),
            vec,
            ANY_SPEC,
            ANY_SPEC,
        ],
        out_specs=[ANY_SPEC, ANY_SPEC, vec],
        out_shape=[_shape((s, d), F32), _shape((s, d), BF16), _shape((1, d), F32)],
        scratch_shapes=[
            pltpu.VMEM((tm, d), F32),
            pltpu.VMEM((tm, d), F32),
            pltpu.VMEM((tm, d), F32),
            pltpu.VMEM((tm, d), F32),
            pltpu.VMEM((tm, d), BF16),
            pltpu.SemaphoreType.DMA((2,)),
            pltpu.SemaphoreType.DMA((2,)),
        ],
        compiler_params=_compute_params("arbitrary", "arbitrary"),
    )(*deps, g, w, gain, x, dres)


def _loss_and_grad(x, gain, target, *, name):
    s, d = x.shape
    tm = _tile(s, 256, SUBLANES_BF16)

    def body(x_ref, gain_ref, t_ref, loss_ref, dx_ref, dxb_ref, dg_ref):
        @pl.when(pl.program_id(0) == 0)
        def _():
            loss_ref[...] = jnp.zeros_like(loss_ref)
            dg_ref[...] = jnp.zeros_like(dg_ref)

        xf = x_ref[...]
        r = lax.rsqrt(jnp.mean(xf * xf, axis=-1, keepdims=True) + EPS)
        err = (xf * r) * gain_ref[...] - t_ref[...]
        loss_ref[...] += 0.5 * jnp.sum(jnp.mean(err * err, axis=-1, keepdims=True))
        dx, dgain = _rmsnorm_bwd(err * (1.0 / d), xf, gain_ref[...], None)
        dx_ref[...] = dx
        dxb_ref[...] = dx.astype(BF16)
        dg_ref[...] += dgain

    row_tile = pl.BlockSpec((tm, d), lambda i: (i, 0))
    vec = pl.BlockSpec((1, d), lambda i: (0, 0))
    return pl.pallas_call(
        body,
        name=name,
        grid=(s // tm,),
        in_specs=[row_tile, vec, row_tile],
        out_specs=[pl.BlockSpec((1, LANES), lambda i: (0, 0)), row_tile, row_tile, vec],
        out_shape=[_shape((1, LANES), F32), _shape((s, d), F32), _shape((s, d), BF16), _shape((1, d), F32)],
        compiler_params=_compute_params("arbitrary"),
    )(x, gain, target)


def _trailing_sums(v_ext, window):
    acc, span = v_ext, 1
    while span < window:
        acc = acc + pltpu.roll(acc, span, 0)
        span *= 2
    return acc


def _leading_sums(q_ext, window):
    n = q_ext.shape[0]
    acc, span = q_ext, 1
    while span < window:
        acc = acc + pltpu.roll(acc, n - span, 0)
        span *= 2
    return acc


def _inverse_counts(first_token, rows, window):
    t = first_token + lax.broadcasted_iota(jnp.int32, (rows, 1), 0)
    return 1.0 / jnp.minimum(t + 1, window).astype(F32)


def _mixer_fwd(proj, conv_w, pool_w, pool_scale, *, name):
    s, _ = proj.shape
    dc = conv_w.shape[1]
    n_groups, cg, _ = pool_w.shape
    dp = n_groups * cg
    assert dc == dp and all(w & (w - 1) == 0 and w <= HALO for w in POOL_WINDOWS)
    ts = _tile(s, 256, HALO)
    halo_blocks = ts // HALO

    def body(b_ref, c_ref, xt_ref, v_ref, ch_ref, xth_ref, vh_ref, cw_ref, pw_ref, ps_ref, y_ref):
        i = pl.program_id(0)
        has_past = i > 0
        u_ext = jnp.concatenate(
            [
                jnp.where(has_past, ch_ref[...].astype(F32) * xth_ref[...].astype(F32), 0.0),
                c_ref[...].astype(F32) * xt_ref[...].astype(F32),
            ],
            axis=0,
        )
        conv = (
            cw_ref[2:3, :] * u_ext[HALO:]
            + cw_ref[1:2, :] * pltpu.roll(u_ext, 1, 0)[HALO:]
            + cw_ref[0:1, :] * pltpu.roll(u_ext, 2, 0)[HALO:]
        )
        y_ref[:, 0:dc] = (b_ref[...].astype(F32) * conv).astype(BF16)
        for gi, window in enumerate(POOL_WINDOWS):
            cols = slice(gi * cg, (gi + 1) * cg)
            v_ext = jnp.concatenate(
                [jnp.where(has_past, vh_ref[:, cols].astype(F32), 0.0), v_ref[:, cols].astype(F32)], axis=0
            )
            mean = _trailing_sums(v_ext, window)[HALO:] * _inverse_counts(i * ts, ts, window)
            diff = (mean - v_ext[HALO:]).astype(BF16)
            z = jnp.dot(diff, pw_ref[gi], preferred_element_type=F32)
            y_ref[:, dc + gi * cg : dc + (gi + 1) * cg] = (z * ps_ref[:, cols]).astype(BF16)

    def col(jc):
        return pl.BlockSpec((ts, dc), lambda i: (i, jc))

    def past(jc):
        return pl.BlockSpec((HALO, dc), lambda i: (jnp.maximum(i * halo_blocks - 1, 0), jc))

    return pl.pallas_call(
        body,
        name=name,
        grid=(s // ts,),
        in_specs=[
            col(0), col(1), col(2), col(3), past(1), past(2), past(3),
            pl.BlockSpec((8, dc), lambda i: (0, 0)),
            pl.BlockSpec((n_groups, cg, cg), lambda i: (0, 0, 0)),
            pl.BlockSpec((1, dp), lambda i: (0, 0)),
        ],
        out_specs=pl.BlockSpec((ts, dc + dp), lambda i: (i, 0)),
        out_shape=_shape((s, dc + dp), BF16),
        compiler_params=_compute_params("parallel"),
    )(proj, proj, proj, proj, proj, proj, proj, conv_w, pool_w, pool_scale)


def _mixer_bwd(dy, proj, conv_w, pool_w, pool_scale, *, name, deps=()):
    s, e = proj.shape
    dc = conv_w.shape[1]
    n_groups, cg, _ = pool_w.shape
    dp = n_groups * cg
    ts = _tile(s, 256, HALO)
    halo_blocks = ts // HALO
    n_tiles = s // ts
    n_halo_blocks = s // HALO
    n_ext = ts + HALO

    def body(dyc_ref, dyp_ref, dycn_ref, dypn_ref, b_ref, c_ref, xt_ref, v_ref, bn_ref, ch_ref, xth_ref, vh_ref,
             cw_ref, pw_ref, ps_ref, dproj_ref, dcw_ref, dpw_ref, dps_ref):
        i = pl.program_id(0)
        has_past = i > 0
        has_next = i < n_tiles - 1

        @pl.when(i == 0)
        def _():
            dcw_ref[...] = jnp.zeros_like(dcw_ref)
            dpw_ref[...] = jnp.zeros_like(dpw_ref)
            dps_ref[...] = jnp.zeros_like(dps_ref)

        c_now, xt_now, b_now = c_ref[...].astype(F32), xt_ref[...].astype(F32), b_ref[...].astype(F32)
        u_ext = jnp.concatenate(
            [jnp.where(has_past, ch_ref[...].astype(F32) * xth_ref[...].astype(F32), 0.0), c_now * xt_now], axis=0
        )
        u0 = u_ext[HALO:]
        u1 = pltpu.roll(u_ext, 1, 0)[HALO:]
        u2 = pltpu.roll(u_ext, 2, 0)[HALO:]
        dyc = dyc_ref[...]
        conv = cw_ref[2:3, :] * u0 + cw_ref[1:2, :] * u1 + cw_ref[0:1, :] * u2
        dproj_ref[:, 0:dc] = (dyc * conv).astype(BF16)
        dconv = dyc * b_now
        dconv_ext = jnp.concatenate(
            [dconv, jnp.where(has_next, dycn_ref[...] * bn_ref[...].astype(F32), 0.0)], axis=0
        )
        du = (
            cw_ref[2:3, :] * dconv
            + cw_ref[1:2, :] * pltpu.roll(dconv_ext, n_ext - 1, 0)[:ts]
            + cw_ref[0:1, :] * pltpu.roll(dconv_ext, n_ext - 2, 0)[:ts]
        )
        dproj_ref[:, dc : 2 * dc] = (du * xt_now).astype(BF16)
        dproj_ref[:, 2 * dc : 3 * dc] = (du * c_now).astype(BF16)
        dcw_ref[0:1, :] += jnp.sum(dconv * u2, axis=0, keepdims=True)
        dcw_ref[1:2, :] += jnp.sum(dconv * u1, axis=0, keepdims=True)
        dcw_ref[2:3, :] += jnp.sum(dconv * u0, axis=0, keepdims=True)

        for gi, window in enumerate(POOL_WINDOWS):
            cols = slice(gi * cg, (gi + 1) * cg)
            v_ext = jnp.concatenate(
                [jnp.where(has_past, vh_ref[:, cols].astype(F32), 0.0), v_ref[:, cols].astype(F32)], axis=0
            )
            mean = _trailing_sums(v_ext, window)[HALO:] * _inverse_counts(i * ts, ts, window)
            diff = (mean - v_ext[HALO:]).astype(BF16)
            z = jnp.dot(diff, pw_ref[gi], preferred_element_type=F32)
            dyp = dyp_ref[:, cols]
            dps_ref[:, cols] += jnp.sum(dyp * z, axis=0, keepdims=True)
            scale = ps_ref[:, cols]
            dz_ext = jnp.concatenate([dyp * scale, jnp.where(has_next, dypn_ref[:, cols] * scale, 0.0)], axis=0)
            dz_ext = dz_ext.astype(BF16)
            dpw_ref[gi] += lax.dot_general(
                diff, dz_ext[:ts], (((0,), (0,)), ((), ())), preferred_element_type=F32
            )
            ddiff_ext = lax.dot_general(
                dz_ext, pw_ref[gi], (((1,), (1,)), ((), ())), preferred_element_type=F32
            )
            q_ext = ddiff_ext * _inverse_counts(i * ts, n_ext, window)
            dv = _leading_sums(q_ext, window)[:ts] - ddiff_ext[:ts]
            dproj_ref[:, 3 * dc + gi * cg : 3 * dc + (gi + 1) * cg] = dv.astype(BF16)

    def col(jc):
        return pl.BlockSpec((ts, dc), lambda i: (i, jc))

    def past(jc):
        return pl.BlockSpec((HALO, dc), lambda i: (jnp.maximum(i * halo_blocks - 1, 0), jc))

    def following(jc):
        return pl.BlockSpec((HALO, dc), lambda i: (jnp.minimum((i + 1) * halo_blocks, n_halo_blocks - 1), jc))

    return pl.pallas_call(
        _behind(body, deps),
        name=name,
        grid=(n_tiles,),
        in_specs=[ANY_SPEC] * len(deps) + [
            col(0), col(1), following(0), following(1),
            col(0), col(1), col(2), col(3), following(0), past(1), past(2), past(3),
            pl.BlockSpec((8, dc), lambda i: (0, 0)),
            pl.BlockSpec((n_groups, cg, cg), lambda i: (0, 0, 0)),
            pl.BlockSpec((1, dp), lambda i: (0, 0)),
        ],
        out_specs=[
            pl.BlockSpec((ts, e), lambda i: (i, 0)),
            pl.BlockSpec((8, dc), lambda i: (0, 0)),
            pl.BlockSpec((n_groups, cg, cg), lambda i: (0, 0, 0)),
            pl.BlockSpec((1, dp), lambda i: (0, 0)),
        ],
        out_shape=[_shape((s, e), BF16), _shape((8, dc), F32), _shape((n_groups, cg, cg), F32), _shape((1, dp), F32)],
        compiler_params=_compute_params("arbitrary"),
    )(*deps, dy, dy, dy, dy, proj, proj, proj, proj, proj, proj, proj, proj, conv_w, pool_w, pool_scale)


def _cast_bf16(a, *, name):
    rows, cols = a.shape
    tr = _tile(rows, 512, SUBLANES_BF16)

    def body(a_ref, o_ref):
        o_ref[...] = a_ref[...].astype(BF16)

    spec = pl.BlockSpec((tr, cols), lambda i: (i, 0))
    return pl.pallas_call(
        body, name=name, grid=(rows // tr,), in_specs=[spec], out_specs=spec, out_shape=_shape((rows, cols), BF16),
        compiler_params=_compute_params("parallel"),
    )(a)


def _core_index():
    return lax.axis_index("c").astype(jnp.int32).reshape((1,))


def _add_sibling_half(grads, received, *, name, deps=()):
    nsh, r, c = grads.shape
    hr = r // 2
    tr = _tile(hr, 512, SUBLANES_BF16)
    tiles = hr // tr

    def body(core_ref, *refs):
        g_ref, r_ref, o_ref = refs[len(deps):]
        o_ref[...] = (g_ref[...].astype(F32) + r_ref[...].astype(F32)).astype(BF16)

    half = pl.BlockSpec((None, tr, c), lambda sh, t, core: (sh, t, 0))
    return pl.pallas_call(
        body,
        name=name,
        grid_spec=pltpu.PrefetchScalarGridSpec(
            num_scalar_prefetch=1,
            grid=(nsh, tiles),
            in_specs=[ANY_SPEC] * len(deps)
            + [pl.BlockSpec((None, tr, c), lambda sh, t, core: (sh, core[0] * tiles + t, 0)), half],
            out_specs=half,
        ),
        out_shape=_shape((nsh, hr, c), BF16),
        compiler_params=_compute_params("parallel", "parallel"),
    )(_core_index(), *deps, grads, received)


def _chip_index_operand():
    return _chip_index(lax.axis_index("x"), lax.axis_index("y")).astype(jnp.int32).reshape((1,))


def _sum_chip_partials(own, received, *, name, deps=()):
    nsh, hr, c = received.shape
    tr = _tile(hr, 256, SUBLANES_BF16)

    def body(chip_ref, *refs):
        own_ref, p_ref, o_ref = refs[len(deps):]
        mine = chip_ref[0]
        total = None
        for sh in range(nsh):
            term = jnp.where(mine == sh, own_ref[...], p_ref[sh]).astype(F32)
            total = term if total is None else total + term
        o_ref[...] = total

    return pl.pallas_call(
        body,
        name=name,
        grid_spec=pltpu.PrefetchScalarGridSpec(
            num_scalar_prefetch=1,
            grid=(hr // tr,),
            in_specs=[ANY_SPEC] * len(deps) + [
                pl.BlockSpec((None, tr, c), lambda t, chip: (chip[0], t, 0)),
                pl.BlockSpec((nsh, tr, c), lambda t, chip: (0, t, 0)),
            ],
            out_specs=pl.BlockSpec((tr, c), lambda t, chip: (t, 0)),
        ),
        out_shape=_shape((hr, c), F32),
        compiler_params=_compute_params("parallel"),
    )(_chip_index_operand(), *deps, own, received)


def _adamw(grad, w, m, v, layer, carried, *, name, deps=()):
    n_layers, r, c = w.shape
    in_halves = isinstance(grad, (tuple, list))
    tr = _tile(r // 2 if in_halves else r, 256, 8)
    half_tiles = (r // 2) // tr if in_halves else 0
    bias1 = 1.0 - ADAM_B1 ** ADAM_STEP
    bias2 = 1.0 - ADAM_B2 ** ADAM_STEP
    n_grads = 2 if in_halves else 1

    def body(core_ref, *refs):
        refs = refs[len(deps):]
        w_ref, m_ref, v_ref = refs[n_grads : n_grads + 3]
        go_ref, d_ref, mo_ref, vo_ref, done_ref = refs[-5:]
        done_ref[...] = jnp.zeros_like(done_ref)
        if in_halves:
            in_my_half = pl.program_id(0) // half_tiles == core_ref[0]
            g = jnp.where(in_my_half, refs[0][...], refs[1][...])
        else:
            g = refs[0][...]
        m_new = ADAM_B1 * m_ref[...] + (1.0 - ADAM_B1) * g
        v_new = ADAM_B2 * v_ref[...] + (1.0 - ADAM_B2) * (g * g)
        go_ref[...] = g
        mo_ref[...] = m_new
        vo_ref[...] = v_new
        d_ref[...] = -ADAM_LR * ((m_new / bias1) / (jnp.sqrt(v_new / bias2) + ADAM_EPS) + ADAM_WD * w_ref[...])

    def half_spec(mine):
        def index(t, core):
            first = (core[0] if mine else 1 - core[0]) * half_tiles
            return (jnp.clip(t - first, 0, half_tiles - 1), 0)

        return pl.BlockSpec((tr, c), index)

    layer_tile = pl.BlockSpec((None, tr, c), lambda t, core: (layer, t, 0))
    if in_halves:
        grad_specs, grads = [half_spec(True), half_spec(False)], list(grad)
    else:
        grad_specs, grads = [pl.BlockSpec((tr, c), lambda t, core: (t, 0))], [grad]
    in_specs = [ANY_SPEC] * len(deps) + grad_specs + [layer_tile] * 3
    args = list(deps) + grads + [w, m, v]
    aliases = {}
    if carried is not None:
        aliases = {1 + len(args) + n: n for n in range(4)}
        in_specs += [HBM_SPEC] * 4
        args += list(carried)
    *outs, done = pl.pallas_call(
        body,
        name=name,
        grid_spec=pltpu.PrefetchScalarGridSpec(
            num_scalar_prefetch=1,
            grid=(r // tr,),
            in_specs=in_specs,
            out_specs=[layer_tile] * 4 + [pl.BlockSpec((8, LANES), lambda t, core: (0, 0))],
        ),
        out_shape=[_shape((n_layers, r, c), F32)] * 4 + [_shape((8, LANES), F32)],
        input_output_aliases=aliases,
        compiler_params=_compute_params("arbitrary"),
    )(_core_index(), *args)
    return outs, done


def _place():
    x, y, c = (lax.axis_index(a) for a in MESH_AXES)
    other_chips = [(1 - x, y), (x, 1 - y), (1 - x, 1 - y)]
    return x, y, c, other_chips


def _chip_index(x, y):
    return 2 * x + y


def _gather_small(conv_rows, pool_rows, *, name):
    blocks = (conv_rows, pool_rows)
    n = len(blocks)

    def body(*refs):
        ins, outs = refs[:n], refs[n : 2 * n]
        send_sems, recv_sems = refs[2 * n :]
        x, y, c, other_chips = _place()
        mine = _chip_index(x, y)
        peers = [(x, y, 1 - c)] + [(px, py, c) for px, py in other_chips]
        _handshake(peers)
        sends = []
        for a in range(n):
            for j, peer in enumerate(peers):
                cp = pltpu.make_async_remote_copy(
                    src_ref=ins[a], dst_ref=outs[a].at[mine], send_sem=send_sems.at[a, j], recv_sem=recv_sems.at[a, j],
                    device_id=peer, device_id_type=MESH_ID,
                )
                cp.start()
                sends.append(cp)
        for a in range(n):
            for j, (px, py, _) in enumerate(peers):
                pltpu.make_async_remote_copy(
                    src_ref=ins[a], dst_ref=outs[a].at[_chip_index(px, py)], send_sem=send_sems.at[a, j],
                    recv_sem=recv_sems.at[a, j], device_id=peers[j], device_id_type=MESH_ID,
                ).wait_recv()
        for cp in sends:
            cp.wait_send()

    return _sequencer_call(
        body,
        [_shape((N_CHIPS,) + blk.shape, F32) for blk in blocks],
        [pltpu.SemaphoreType.DMA((n, 4)), pltpu.SemaphoreType.DMA((n, 4))],
        GATHER_SMALL_COLLECTIVE_ID,
        name,
    )(*blocks)


def _allreduce_small(vec, *, name):
    rows, n = vec.shape
    n_dev = 8

    def body(v_ref, o_ref, slots, send_sems, recv_sems):
        x, y, c, _ = _place()
        me = 4 * x + 2 * y + c
        slots[me] = v_ref[...]
        sends = []
        for mask in range(1, n_dev):
            fx, fy, fc = (mask >> 2) & 1, (mask >> 1) & 1, mask & 1
            peer = (x ^ fx, y ^ fy, c ^ fc)
            cp = pltpu.make_async_remote_copy(
                src_ref=v_ref, dst_ref=slots.at[me], send_sem=send_sems.at[mask - 1], recv_sem=recv_sems.at[mask - 1],
                device_id=peer, device_id_type=MESH_ID,
            )
            cp.start()
            sends.append(cp)
        for mask in range(1, n_dev):
            fx, fy, fc = (mask >> 2) & 1, (mask >> 1) & 1, mask & 1
            peer = (x ^ fx, y ^ fy, c ^ fc)
            pltpu.make_async_remote_copy(
                src_ref=v_ref, dst_ref=slots.at[4 * peer[0] + 2 * peer[1] + peer[2]], send_sem=send_sems.at[mask - 1],
                recv_sem=recv_sems.at[mask - 1], device_id=peer, device_id_type=MESH_ID,
            ).wait_recv()
        total = slots[0]
        for dev in range(1, n_dev):
            total = total + slots[dev]
        o_ref[...] = total
        for cp in sends:
            cp.wait_send()

    return pl.pallas_call(
        body,
        name=name,
        in_specs=[VMEM_SPEC],
        out_specs=VMEM_SPEC,
        out_shape=_shape((rows, n), F32),
        scratch_shapes=[
            pltpu.VMEM((n_dev, rows, n), F32),
            pltpu.SemaphoreType.DMA((n_dev - 1,)),
            pltpu.SemaphoreType.DMA((n_dev - 1,)),
        ],
    )(vec)


def _handshake(peers):
    barrier = pltpu.get_barrier_semaphore()
    for peer in peers:
        pl.semaphore_signal(barrier, inc=1, device_id=peer, device_id_type=MESH_ID)
    pl.semaphore_wait(barrier, len(peers))


def _sequencer_call(body, out_type, scratch_types, collective_id, name):
    return pl.kernel(
        body,
        name=name,
        out_type=out_type,
        mesh=plsc.ScalarSubcoreMesh(axis_name="sequencer", num_cores=1),
        scratch_types=scratch_types,
        compiler_params=pltpu.CompilerParams(collective_id=collective_id),
    )


GATHER_COLLECTIVE_ID = 1
EXCHANGE_COLLECTIVE_ID = 2
SCATTER_COLLECTIVE_ID = 3
SHARE_COLLECTIVE_ID = 4
GATHER_SMALL_COLLECTIVE_ID = 5


def _gather_weights(stacked, layer, *, name):
    n = len(stacked)

    def body(*refs):
        ins, outs = refs[:n], refs[n : 2 * n]
        own_sems, send_sems, recv_sems, pass_send_sems, pass_recv_sems = refs[2 * n :]
        x, y, c, other_chips = _place()
        mine = _chip_index(x, y)
        sibling = (x, y, 1 - c)
        _handshake([sibling] + [(px, py, c) for px, py in other_chips])
        pending = []

        def send_my_half(a):
            hr = ins[a].shape[1] // 2
            my_half = pl.ds(c * hr, hr)
            cp = pltpu.make_async_remote_copy(
                src_ref=ins[a].at[layer], dst_ref=outs[a].at[mine], send_sem=own_sems.at[0, a],
                recv_sem=own_sems.at[1, a], device_id=sibling, device_id_type=MESH_ID,
            )
            cp.start()
            pending.append(cp.wait)
            for j, (px, py) in enumerate(other_chips):
                cp = pltpu.make_async_remote_copy(
                    src_ref=ins[a].at[layer, my_half], dst_ref=outs[a].at[mine, my_half], send_sem=send_sems.at[a, j],
                    recv_sem=recv_sems.at[a, j], device_id=(px, py, c), device_id_type=MESH_ID,
                )
                cp.start()
                pending.append(cp.wait_send)

        for a in range(min(2, n)):
            send_my_half(a)
        for a in range(n):
            hr = ins[a].shape[1] // 2
            my_half = pl.ds(c * hr, hr)
            for j, (px, py) in enumerate(other_chips):
                landed = outs[a].at[_chip_index(px, py), my_half]
                pltpu.make_async_remote_copy(
                    src_ref=landed, dst_ref=landed, send_sem=send_sems.at[a, j], recv_sem=recv_sems.at[a, j],
                    device_id=(px, py, c), device_id_type=MESH_ID,
                ).wait_recv()
                cp = pltpu.make_async_remote_copy(
                    src_ref=landed, dst_ref=landed, send_sem=pass_send_sems.at[a, j], recv_sem=pass_recv_sems.at[a, j],
                    device_id=sibling, device_id_type=MESH_ID,
                )
                cp.start()
                pending.append(cp.wait_send)
            if a + 2 < n:
                send_my_half(a + 2)
        for a in range(n):
            hr = ins[a].shape[1] // 2
            sibling_half = pl.ds((1 - c) * hr, hr)
            for j, (px, py) in enumerate(other_chips):
                passed = outs[a].at[_chip_index(px, py), sibling_half]
                pltpu.make_async_remote_copy(
                    src_ref=passed, dst_ref=passed, send_sem=pass_send_sems.at[a, j], recv_sem=pass_recv_sems.at[a, j],
                    device_id=sibling, device_id_type=MESH_ID,
                ).wait_recv()
        for wait in pending:
            wait()

    return _sequencer_call(
        body,
        [_shape((N_CHIPS,) + a.shape[1:], BF16) for a in stacked],
        [pltpu.SemaphoreType.DMA((2, n))] + [pltpu.SemaphoreType.DMA((n, 3))] * 4,
        GATHER_COLLECTIVE_ID,
        name,
    )(*stacked)


def _exchange_halves(grads, *, name):
    n = len(grads)

    def body(*refs):
        ins, outs = refs[:n], refs[n : 2 * n]
        send_sems, recv_sems = refs[2 * n :]
        x, y, c, _ = _place()
        sibling = (x, y, 1 - c)
        _handshake([sibling])
        copies = []
        for a in range(n):
            hr = ins[a].shape[1] // 2
            cp = pltpu.make_async_remote_copy(
                src_ref=ins[a].at[:, pl.ds((1 - c) * hr, hr), :], dst_ref=outs[a], send_sem=send_sems.at[a],
                recv_sem=recv_sems.at[a], device_id=sibling, device_id_type=MESH_ID,
            )
            cp.start()
            copies.append(cp)
        for cp in copies:
            cp.wait()

    return _sequencer_call(
        body,
        [_shape((g.shape[0], g.shape[1] // 2, g.shape[2]), BF16) for g in grads],
        [pltpu.SemaphoreType.DMA((n,)), pltpu.SemaphoreType.DMA((n,))],
        EXCHANGE_COLLECTIVE_ID,
        name,
    )(*grads)


def _scatter_partials(partials, *, name):
    n = len(partials)

    def body(*refs):
        ins, outs = refs[:n], refs[n : 2 * n]
        send_sems, recv_sems = refs[2 * n :]
        x, y, c, other_chips = _place()
        mine = _chip_index(x, y)
        _handshake([(px, py, c) for px, py in other_chips])
        pending = []
        for a in range(n):
            for j, (px, py) in enumerate(other_chips):
                cp = pltpu.make_async_remote_copy(
                    src_ref=ins[a].at[_chip_index(px, py)], dst_ref=outs[a].at[mine], send_sem=send_sems.at[a, j],
                    recv_sem=recv_sems.at[a, j], device_id=(px, py, c), device_id_type=MESH_ID,
                )
                cp.start()
                pending.append(cp.wait_send)
        for a in range(n):
            for j, (px, py) in enumerate(other_chips):
                landed = outs[a].at[_chip_index(px, py)]
                pltpu.make_async_remote_copy(
                    src_ref=landed, dst_ref=landed, send_sem=send_sems.at[a, j], recv_sem=recv_sems.at[a, j],
                    device_id=(px, py, c), device_id_type=MESH_ID,
                ).wait_recv()
        for wait in pending:
            wait()

    return _sequencer_call(
        body,
        [_shape(p.shape, BF16) for p in partials],
        [pltpu.SemaphoreType.DMA((n, 3)), pltpu.SemaphoreType.DMA((n, 3))],
        SCATTER_COLLECTIVE_ID,
        name,
    )(*partials)


def _share_halves(halves, *, name):
    n = len(halves)

    def body(*refs):
        ins, outs = refs[:n], refs[n : 2 * n]
        send_sems, recv_sems = refs[2 * n :]
        x, y, c, _ = _place()
        sibling = (x, y, 1 - c)
        _handshake([sibling])
        copies = []
        for a in range(n):
            cp = pltpu.make_async_remote_copy(
                src_ref=ins[a], dst_ref=outs[a], send_sem=send_sems.at[a], recv_sem=recv_sems.at[a],
                device_id=sibling, device_id_type=MESH_ID,
            )
            cp.start()
            copies.append(cp)
        for cp in copies:
            cp.wait()

    return _sequencer_call(
        body,
        [_shape(h.shape, F32) for h in halves],
        [pltpu.SemaphoreType.DMA((n,)), pltpu.SemaphoreType.DMA((n,))],
        SHARE_COLLECTIVE_ID,
        name,
    )(*halves)


class _ReduceToOwner:
    def __init__(self, grads, tag):
        self.grads, self.tag = grads, tag
        self.received = _exchange_halves(grads, name=f"exchange_halves_{tag}")

    def add_sibling(self, after):
        partials = [
            _add_sibling_half(g, r, name=f"add_sibling_{self.tag}_{a}", deps=after)
            for a, (g, r) in enumerate(zip(self.grads, self.received))
        ]
        self.partials = partials
        self.slots = _scatter_partials(partials, name=f"scatter_partials_{self.tag}")
        return partials

    def sum_chips(self, after):
        halves = [
            _sum_chip_partials(p, s, name=f"sum_partials_{self.tag}_{a}", deps=after)
            for a, (p, s) in enumerate(zip(self.partials, self.slots))
        ]
        self.halves = halves
        self.theirs = _share_halves(halves, name=f"share_halves_{self.tag}")
        return halves

    def totals(self):
        return list(zip(self.halves, self.theirs))


def kernel(x, w_in, conv_w, pool_w, pool_scale, w_out, norm_mix, norm_mlp, w_up, w_down, norm_final, loss_target, m_w_in, m_conv_w, m_pool_w, m_pool_scale, m_w_out, m_norm_mix, m_norm_mlp, m_w_up, m_w_down, m_norm_final, v_w_in, v_conv_w, v_pool_w, v_pool_scale, v_w_out, v_norm_mix, v_norm_mlp, v_w_up, v_w_down, v_norm_final):
    n_layers, d, _ = w_in.shape
    s = x.shape[1]
    dc = conv_w.shape[2] * N_CHIPS
    n_groups, cg_rows, cg = pool_w.shape[1:]
    dp = n_groups * cg
    x0 = x.reshape(s, d)
    target = loss_target.reshape(s, d)

    big = [w_in, w_out, w_up, w_down]
    big_bf16 = [
        _cast_bf16(w.reshape(-1, w.shape[2]), name=f"cast_{t}").reshape(w.shape)
        for t, w in zip(("w_in", "w_out", "w_up", "w_down"), big)
    ]

    conv_rows = jnp.pad(conv_w, ((0, 0), (0, 8 - CONV_TAPS), (0, 0))).reshape(n_layers * 8, -1)
    pool_rows = pool_w.reshape(n_layers * n_groups * cg_rows, cg)
    conv_all, pool_all = _gather_small(conv_rows, pool_rows, name="gather_small")
    conv_full = conv_all.reshape(N_CHIPS, n_layers, 8, -1).transpose(1, 2, 0, 3).reshape(n_layers, 8, dc)
    pool_full = (
        pool_all.reshape(N_CHIPS, n_layers, n_groups, cg_rows, cg).transpose(1, 2, 0, 3, 4)
        .reshape(n_layers, n_groups, cg, cg).astype(BF16)
    )

    saved = []
    xl = x0
    for l in range(n_layers):
        win_g, wout_g, wup_g, wdown_g = [
            _gather_weights([w], l, name=f"gather_weights_l{l}_{t}")[0] for t, w in enumerate(big_bf16)
        ]
        gain_mix = norm_mix[l].reshape(1, d)
        gain_mlp = norm_mlp[l].reshape(1, d)
        scale = pool_scale[l].reshape(1, dp)
        h1 = _rmsnorm(xl, gain_mix, name=f"norm_mix_l{l}")
        proj = _matmul_cols(h1, win_g, relu2=False, name=f"in_proj_l{l}")
        y = _mixer_fwd(proj, conv_full[l], pool_full[l], scale, name=f"mixer_fwd_l{l}")
        x_mid = _matmul_residual(y, wout_g.reshape(-1, d), xl, name=f"out_proj_l{l}")
        h2 = _rmsnorm(x_mid, gain_mlp, name=f"norm_mlp_l{l}")
        u, u2 = _matmul_cols(h2, wup_g, relu2=True, name=f"up_proj_l{l}")
        x_next = _matmul_residual(u2, wdown_g.reshape(-1, d), x_mid, name=f"down_proj_l{l}")
        saved.append((xl, h1, proj, y, x_mid, h2, u, u2, win_g, wout_g, wup_g, wdown_g, gain_mix, gain_mlp, scale))
        xl = x_next

    loss_part, dx, dx_bf16, d_norm_final = _loss_and_grad(xl, norm_final.reshape(1, d), target, name="loss_head")
    loss = lax.psum(loss_part[0, 0], MESH_AXES)

    small_grads = [None] * n_layers
    carried = [None] * 5
    pool_params = tuple(p.reshape(n_layers, n_groups * cg_rows, cg) for p in (pool_w, m_pool_w, v_pool_w))
    params = [(w_in, m_w_in, v_w_in), (w_out, m_w_out, v_w_out), (w_up, m_w_up, v_w_up), (w_down, m_w_down, v_w_down),
              pool_params]
    DOWN, UP_OUT, IN_POOL = (3,), (2, 1), (0, 4)

    def update(reduce, which, layer, after):
        dones = []
        for a, total in zip(which, reduce.totals()):
            w, m, v = params[a]
            carried[a], done = _adamw(total, w, m, v, layer, carried[a], name=f"adamw_{a}_l{layer}", deps=after)
            dones.append(done)
        return dones

    up_out_above = in_pool_above = None
    for l in reversed(range(n_layers)):
        xl, h1, proj, y, x_mid, h2, u, u2, win_g, wout_g, wup_g, wdown_g, gain_mix, gain_mlp, scale = saved[l]
        above = up_out_above is not None
        g_down = _matmul_tn(u2, dx_bf16, n_shards=N_CHIPS, shard_cols=False, name=f"grad_w_down_l{l}")
        down = _ReduceToOwner([g_down], f"down_l{l}")
        deps = up_out_above.sum_chips([g_down]) if above else [g_down]
        da = _matmul_nt(dx_bf16, wdown_g.reshape(-1, d), u=u, name=f"grad_act_l{l}", deps=deps)
        g_up = _matmul_tn(
            h2, da, n_shards=N_CHIPS, shard_cols=True, name=f"grad_w_up_l{l}", deps=down.add_sibling([da])
        )
        deps = in_pool_above.sum_chips([g_up]) if above else [g_up]
        dx_mid, dx_mid_bf16, d_gain_mlp = _matmul_nt_norm_bwd(
            da, wup_g, x_mid, gain_mlp, dx, name=f"grad_mid_l{l}", deps=deps
        )
        g_out = _matmul_tn(
            y, dx_mid_bf16, n_shards=N_CHIPS, shard_cols=False, name=f"grad_w_out_l{l}",
            deps=down.sum_chips([dx_mid_bf16]),
        )
        up_out = _ReduceToOwner([g_up, g_out], f"up_out_l{l}")
        deps = update(up_out_above, UP_OUT, l + 1, [g_out]) if above else [g_out]
        dy = _matmul_nt(dx_mid_bf16, wout_g.reshape(-1, d), name=f"grad_mixed_l{l}", deps=deps)
        dproj, d_conv, d_pool, d_scale = _mixer_bwd(
            dy, proj, conv_full[l], pool_full[l], scale, name=f"mixer_bwd_l{l}", deps=up_out.add_sibling([dy])
        )
        g_in = _matmul_tn(h1, dproj, n_shards=N_CHIPS, shard_cols=True, name=f"grad_w_in_l{l}")
        g_pool = (
            d_pool.reshape(n_groups, N_CHIPS, cg_rows, cg).transpose(1, 0, 2, 3)
            .reshape(N_CHIPS, n_groups * cg_rows, cg).astype(BF16)
        )
        in_pool = _ReduceToOwner([g_in, g_pool], f"in_pool_l{l}")
        deps = update(down, DOWN, l, [g_in])
        if above:
            deps = update(in_pool_above, IN_POOL, l + 1, deps)
        dx, dx_bf16, d_gain_mix = _matmul_nt_norm_bwd(
            dproj, win_g, xl, gain_mix, dx_mid, name=f"grad_x_l{l}", deps=in_pool.add_sibling(deps)
        )
        small_grads[l] = jnp.concatenate(
            [d_conv[:CONV_TAPS].reshape(-1), d_scale.reshape(-1), d_gain_mix.reshape(-1), d_gain_mlp.reshape(-1)]
        )
        up_out_above, in_pool_above = up_out, in_pool
    deps = up_out_above.sum_chips([dx_bf16])
    deps = in_pool_above.sum_chips(deps)
    deps = update(up_out_above, UP_OUT, 0, deps)
    update(in_pool_above, IN_POOL, 0, deps)

    vec = jnp.concatenate(small_grads + [d_norm_final.reshape(-1)])
    vec = _allreduce_small(vec.reshape(8, -1), name="allreduce_small").reshape(-1)
    per_layer = vec[: n_layers * (CONV_TAPS * dc + dp + 2 * d)].reshape(n_layers, -1)
    chip = _chip_index(lax.axis_index("x"), lax.axis_index("y"))
    dcs = dc // N_CHIPS
    g_conv = lax.dynamic_slice_in_dim(per_layer[:, : CONV_TAPS * dc].reshape(n_layers, CONV_TAPS, dc), chip * dcs, dcs, axis=2)
    g_scale = per_layer[:, CONV_TAPS * dc : CONV_TAPS * dc + dp]
    g_mix = per_layer[:, CONV_TAPS * dc + dp : CONV_TAPS * dc + dp + d]
    g_mlp = per_layer[:, CONV_TAPS * dc + dp + d :]
    g_final = vec[n_layers * (CONV_TAPS * dc + dp + 2 * d) :]

    def small_adamw(g, w, m, v, tag):
        flat = lambda t: t.reshape(1, -1, t.shape[-1])
        out, _ = _adamw(flat(g)[0], flat(w), flat(m), flat(v), 0, None, name=f"adamw_{tag}")
        return [o.reshape(w.shape) for o in out]

    o_conv = small_adamw(g_conv, conv_w, m_conv_w, v_conv_w, "conv_w")
    o_scale = small_adamw(g_scale, pool_scale, m_pool_scale, v_pool_scale, "pool_scale")
    o_mix = small_adamw(g_mix, norm_mix, m_norm_mix, v_norm_mix, "norm_mix")
    o_mlp = small_adamw(g_mlp, norm_mlp, m_norm_mlp, v_norm_mlp, "norm_mlp")
    o_final = small_adamw(g_final, norm_final, m_norm_final, v_norm_final, "norm_final")
    o_in, o_out, o_up, o_down, o_pool = carried
    o_pool = [o.reshape(pool_w.shape) for o in o_pool]

    ordered = [o_in, o_conv, o_pool, o_scale, o_out, o_mix, o_mlp, o_up, o_down, o_final]
    return (loss, dx.reshape(x.shape), *[o[0] for o in ordered], *[o[1] for o in ordered], *[o[2] for o in ordered],
            *[o[3] for o in ordered])
```

```python
import functools

import jax
import jax.numpy as jnp
from jax import lax
from jax.experimental import pallas as pl
from jax.experimental.pallas import tpu as pltpu
from jax.experimental.pallas import tpu_sc as plsc

F32 = jnp.float32
BF16 = jnp.bfloat16

EPS = 1e-6
POOL_WINDOWS = (2, 4, 8, 16)
CONV_TAPS = 3
HALO = 16

ADAM_LR = 0.001
ADAM_B1 = 0.9
ADAM_B2 = 0.999
ADAM_EPS = 1e-08
ADAM_WD = 0.01
ADAM_STEP = 10

N_CHIPS = 4
MESH_AXES = ("x", "y", "c")
V7X_VMEM_LIMIT_BYTES = 56 * 1024 * 1024
SUBLANES_BF16 = 16
LANES = 128

HBM_SPEC = pl.BlockSpec(memory_space=pltpu.HBM)
VMEM_SPEC = pl.BlockSpec(memory_space=pltpu.VMEM)
MESH_ID = pl.DeviceIdType.MESH


def _tile(dim, target, align):
    if dim <= target:
        return dim
    t = (target // align) * align
    while dim % t:
        t -= align
    assert t > 0, (dim, target, align)
    return t


def _compute_params(*semantics):
    return pltpu.CompilerParams(dimension_semantics=semantics, vmem_limit_bytes=V7X_VMEM_LIMIT_BYTES)


def _shape(shape, dtype):
    return jax.ShapeDtypeStruct(shape, dtype)


ANY_SPEC = pl.BlockSpec(memory_space=pl.ANY)


def _behind(body, deps):
    return lambda *refs: body(*refs[len(deps):])


def _rmsnorm_bwd(dh, x, gain, dres):
    r = lax.rsqrt(jnp.mean(x * x, axis=-1, keepdims=True) + EPS)
    xn = x * r
    dgain = jnp.sum(dh * xn, axis=0, keepdims=True)
    dxn = dh * gain
    dx = r * (dxn - xn * jnp.mean(dxn * xn, axis=-1, keepdims=True))
    if dres is not None:
        dx = dx + dres
    return dx, dgain


MXU_ROWS = 512


def _rmsnorm(x, gain, *, name):
    s, d = x.shape
    tm = _tile(s, 512, SUBLANES_BF16)

    def body(x_ref, g_ref, h_ref):
        xf = x_ref[...]
        r = lax.rsqrt(jnp.mean(xf * xf, axis=-1, keepdims=True) + EPS)
        h_ref[...] = ((xf * r) * g_ref[...]).astype(BF16)

    row_tile = pl.BlockSpec((tm, d), lambda i: (i, 0))
    return pl.pallas_call(
        body, name=name, grid=(s // tm,), in_specs=[row_tile, pl.BlockSpec((1, d), lambda i: (0, 0))],
        out_specs=row_tile, out_shape=_shape((s, d), BF16), compiler_params=_compute_params("parallel"),
    )(x, gain)


def _matmul_cols(h, w, *, relu2, name):
    s, k = h.shape
    nsh, _, c = w.shape
    tm = _tile(s, 2048, MXU_ROWS)
    tn = _tile(c, 1024, LANES)
    rows = min(MXU_ROWS, tm)
    per_shard = c // tn

    def body(h_ref, w_ref, *out_refs):
        def step(t, carry):
            rs = pl.ds(pl.multiple_of(t * rows, rows), rows)
            acc = jnp.dot(h_ref[rs, :], w_ref[...], preferred_element_type=F32)
            if relu2:
                u = jnp.maximum(acc, 0.0)
                out_refs[0][rs, :] = u.astype(BF16)
                out_refs[1][rs, :] = (u * u).astype(BF16)
            else:
                out_refs[0][rs, :] = acc.astype(BF16)
            return carry

        lax.fori_loop(0, tm // rows, step, 0)

    tile_out = pl.BlockSpec((tm, tn), lambda i, j: (i, j))
    n = nsh * c
    outs = [_shape((s, n), BF16)] * (2 if relu2 else 1)
    result = pl.pallas_call(
        body,
        name=name,
        grid=(s // tm, n // tn),
        in_specs=[
            pl.BlockSpec((tm, k), lambda i, j: (i, 0)),
            pl.BlockSpec((None, k, tn), lambda i, j: (j // per_shard, 0, j % per_shard)),
        ],
        out_specs=[tile_out] * len(outs),
        out_shape=outs,
        compiler_params=_compute_params("parallel", "arbitrary"),
    )(h, w)
    return result if relu2 else result[0]


def _matmul_residual(a, w, res, *, name):
    s, k = a.shape
    _, n = w.shape
    tm = _tile(s, 1024, MXU_ROWS)
    tn = _tile(n, 1024, LANES)
    tk = _tile(k, 2048, LANES)
    rows = min(MXU_ROWS, tm)

    def body(a_ref, w_ref, r_ref, o_ref):
        kk = pl.program_id(2)

        @pl.when(kk == 0)
        def _():
            o_ref[...] = r_ref[...]

        def step(t, carry):
            rs = pl.ds(pl.multiple_of(t * rows, rows), rows)
            o_ref[rs, :] += jnp.dot(a_ref[rs, :], w_ref[...], preferred_element_type=F32)
            return carry

        lax.fori_loop(0, tm // rows, step, 0, unroll=True)

    return pl.pallas_call(
        body,
        name=name,
        grid=(s // tm, n // tn, k // tk),
        in_specs=[
            pl.BlockSpec((tm, tk), lambda i, j, kk: (i, kk)),
            pl.BlockSpec((tk, tn), lambda i, j, kk: (kk, j)),
            pl.BlockSpec((tm, tn), lambda i, j, kk: (i, j)),
        ],
        out_specs=pl.BlockSpec((tm, tn), lambda i, j, kk: (i, j)),
        out_shape=_shape((s, n), F32),
        compiler_params=_compute_params("parallel", "parallel", "arbitrary"),
    )(a, w, res)


def _matmul_tn(a, g, *, n_shards, shard_cols, name, deps=()):
    m, kd = a.shape
    _, n = g.shape
    r, c = (kd, n // n_shards) if shard_cols else (kd // n_shards, n)
    tr = _tile(kd, 2048, LANES)
    tn = _tile(c, 1024, LANES)
    tm = _tile(m, 2048, SUBLANES_BF16)
    nm = m // tm
    cols_per_shard = c // tn

    out_rows = min(MXU_ROWS, tr)

    def body(a_ref, g_ref, o_ref, acc_ref):
        mm = pl.program_id(2)

        @pl.when(mm == 0)
        def _():
            acc_ref[...] = jnp.zeros_like(acc_ref)

        for t in range(tr // out_rows):
            rs = slice(t * out_rows, (t + 1) * out_rows)
            acc_ref[rs, :] += lax.dot_general(
                a_ref[:, rs], g_ref[...], (((0,), (0,)), ((), ())), preferred_element_type=F32
            )

        @pl.when(mm == nm - 1)
        def _():
            o_ref[...] = acc_ref[...].astype(BF16)

    if shard_cols:
        out_spec = pl.BlockSpec((None, tr, tn), lambda i, j, mm: (j // cols_per_shard, i, j % cols_per_shard))
        out_shape = _shape((n_shards, kd, c), BF16)
    else:
        out_spec = pl.BlockSpec((tr, tn), lambda i, j, mm: (i, j))
        out_shape = _shape((kd, n), BF16)
    out = pl.pallas_call(
        _behind(body, deps),
        name=name,
        grid=(kd // tr, n // tn, nm),
        in_specs=[ANY_SPEC] * len(deps) + [
            pl.BlockSpec((tm, tr), lambda i, j, mm: (mm, i)),
            pl.BlockSpec((tm, tn), lambda i, j, mm: (mm, j)),
        ],
        out_specs=out_spec,
        out_shape=out_shape,
        scratch_shapes=[pltpu.VMEM((tr, tn), F32)],
        compiler_params=_compute_params("parallel", "parallel", "arbitrary"),
    )(*deps, a, g)
    return out.reshape(n_shards, r, c)


def _matmul_nt(g, w, *, u=None, name, deps=()):
    m, n = g.shape
    kd, _ = w.shape
    tm = _tile(m, 2048, MXU_ROWS)
    tj = _tile(kd, 1024, LANES)
    rows = min(MXU_ROWS, tm)

    def body(*refs):
        if u is None:
            g_ref, w_ref, o_ref = refs
        else:
            g_ref, w_ref, u_ref, o_ref = refs

        def step(t, carry):
            rs = pl.ds(pl.multiple_of(t * rows, rows), rows)
            prod = lax.dot_general(g_ref[rs, :], w_ref[...], (((1,), (1,)), ((), ())), preferred_element_type=F32)
            if u is None:
                o_ref[rs, :] = prod
            else:
                o_ref[rs, :] = (prod * (2.0 * u_ref[rs, :].astype(F32))).astype(BF16)
            return carry

        lax.fori_loop(0, tm // rows, step, 0)

    tile_out = pl.BlockSpec((tm, tj), lambda i, j: (i, j))
    in_specs = [pl.BlockSpec((tm, n), lambda i, j: (i, 0)), pl.BlockSpec((tj, n), lambda i, j: (j, 0))]
    args = [g, w]
    if u is not None:
        in_specs.append(tile_out)
        args.append(u)
    return pl.pallas_call(
        _behind(body, deps),
        name=name,
        grid=(m // tm, kd // tj),
        in_specs=[ANY_SPEC] * len(deps) + in_specs,
        out_specs=tile_out,
        out_shape=_shape((m, kd), F32 if u is None else BF16),
        compiler_params=_compute_params("parallel", "arbitrary"),
    )(*deps, *args)


EPILOGUE_ROWS = 128


def _matmul_nt_norm_bwd(g, w, x, gain, dres, *, name, deps=()):
    s, n = g.shape
    nsh, d, c = w.shape
    tm = _tile(s, 1024, MXU_ROWS)
    tk = _tile(c, 1024, LANES)
    per_shard = c // tk
    nk = n // tk
    n_blocks = s // tm
    mxu_rows = min(MXU_ROWS, tm)
    rows = min(EPILOGUE_ROWS, tm)
    piece = tm // nk
    assert piece * nk == tm and piece % 8 == 0

    def body(g_ref, w_ref, gain_ref, x_hbm, dres_hbm, dx_hbm, dxb_hbm, dg_ref, acc_ref, x_buf, dres_buf, dx_buf,
             dxb_buf, in_sems, out_sems):
        i = pl.program_id(0)
        kk = pl.program_id(1)

        @pl.when((kk == 0) & (i == 0))
        def _():
            dg_ref[...] = jnp.zeros_like(dg_ref)

        block_rows = pl.ds(pl.multiple_of(i * tm, tm), tm)
        pieces = pl.ds(pl.multiple_of(kk * piece, piece), piece)
        piece_rows = pl.ds(pl.multiple_of(i * tm + kk * piece, piece), piece)
        pltpu.make_async_copy(x_hbm.at[piece_rows, :], x_buf.at[pieces, :], in_sems.at[0]).start()
        pltpu.make_async_copy(dres_hbm.at[piece_rows, :], dres_buf.at[pieces, :], in_sems.at[1]).start()

        @pl.when(kk == 0)
        def _():
            acc_ref[...] = jnp.zeros_like(acc_ref)

        def accumulate(t, carry):
            rs = pl.ds(pl.multiple_of(t * mxu_rows, mxu_rows), mxu_rows)
            acc_ref[rs, :] += lax.dot_general(
                g_ref[rs, :], w_ref[...], (((1,), (1,)), ((), ())), preferred_element_type=F32
            )
            return carry

        lax.fori_loop(0, tm // mxu_rows, accumulate, 0, unroll=True)

        def stores():
            return (
                pltpu.make_async_copy(dx_buf, dx_hbm.at[block_rows, :], out_sems.at[0]),
                pltpu.make_async_copy(dxb_buf, dxb_hbm.at[block_rows, :], out_sems.at[1]),
            )

        @pl.when(kk == nk - 1)
        def _():
            pltpu.make_async_copy(x_hbm.at[block_rows, :], x_buf, in_sems.at[0]).wait()
            pltpu.make_async_copy(dres_hbm.at[block_rows, :], dres_buf, in_sems.at[1]).wait()

            @pl.when(i > 0)
            def _():
                for cp in stores():
                    cp.wait()

            def step(j, carry):
                rs = pl.ds(pl.multiple_of(j * rows, rows), rows)
                dx, dgain = _rmsnorm_bwd(acc_ref[rs, :], x_buf[rs, :], gain_ref[...], dres_buf[rs, :])
                dx_buf[rs, :] = dx
                dxb_buf[rs, :] = dx.astype(BF16)
                dg_ref[...] += dgain
                return carry

            lax.fori_loop(0, tm // rows, step, 0)
            for cp in stores():
                cp.start()

            @pl.when(i == n_blocks - 1)
            def _():
                for cp in stores():
                    cp.wait()

    vec = pl.BlockSpec((1, d), lambda i, kk: (0, 0))
    return pl.pallas_call(
        _behind(body, deps),
        name=name,
        grid=(n_blocks, nk),
        in_specs=[ANY_SPEC] * len(deps) + [
            pl.BlockSpec((tm, tk), lambda i, kk: (i, kk)),
            pl.BlockSpec((None, d, tk), lambda i, kk: (kk // per_shard, 0, kk % per_shard)),
            vec,
            ANY_SPEC,
            ANY_SPEC,
        ],
        out_specs=[ANY_SPEC, ANY_SPEC, vec],
        out_shape=[_shape((s, d), F32), _shape((s, d), BF16), _shape((1, d), F32)],
        scratch_shapes=[
            pltpu.VMEM((tm, d), F32),
            pltpu.VMEM((tm, d), F32),
            pltpu.VMEM((tm, d), F32),
            pltpu.VMEM((tm, d), F32),
            pltpu.VMEM((tm, d), BF16),
            pltpu.SemaphoreType.DMA((2,)),
            pltpu.SemaphoreType.DMA((2,)),
        ],
        compiler_params=_compute_params("arbitrary", "arbitrary"),
    )(*deps, g, w, gain, x, dres)


def _loss_and_grad(x, gain, target, *, name):
    s, d = x.shape
    tm = _tile(s, 256, SUBLANES_BF16)

    def body(x_ref, gain_ref, t_ref, loss_ref, dx_ref, dxb_ref, dg_ref):
        @pl.when(pl.program_id(0) == 0)
        def _():
            loss_ref[...] = jnp.zeros_like(loss_ref)
            dg_ref[...] = jnp.zeros_like(dg_ref)

        xf = x_ref[...]
        r = lax.rsqrt(jnp.mean(xf * xf, axis=-1, keepdims=True) + EPS)
        err = (xf * r) * gain_ref[...] - t_ref[...]
        loss_ref[...] += 0.5 * jnp.sum(jnp.mean(err * err, axis=-1, keepdims=True))
        dx, dgain = _rmsnorm_bwd(err * (1.0 / d), xf, gain_ref[...], None)
        dx_ref[...] = dx
        dxb_ref[...] = dx.astype(BF16)
        dg_ref[...] += dgain

    row_tile = pl.BlockSpec((tm, d), lambda i: (i, 0))
    vec = pl.BlockSpec((1, d), lambda i: (0, 0))
    return pl.pallas_call(
        body,
        name=name,
        grid=(s // tm,),
        in_specs=[row_tile, vec, row_tile],
        out_specs=[pl.BlockSpec((1, LANES), lambda i: (0, 0)), row_tile, row_tile, vec],
        out_shape=[_shape((1, LANES), F32), _shape((s, d), F32), _shape((s, d), BF16), _shape((1, d), F32)],
        compiler_params=_compute_params("arbitrary"),
    )(x, gain, target)


def _trailing_sums(v_ext, window):
    acc, span = v_ext, 1
    while span < window:
        acc = acc + pltpu.roll(acc, span, 0)
        span *= 2
    return acc


def _leading_sums(q_ext, window):
    n = q_ext.shape[0]
    acc, span = q_ext, 1
    while span < window:
        acc = acc + pltpu.roll(acc, n - span, 0)
        span *= 2
    return acc


def _inverse_counts(first_token, rows, window):
    t = first_token + lax.broadcasted_iota(jnp.int32, (rows, 1), 0)
    return 1.0 / jnp.minimum(t + 1, window).astype(F32)


def _mixer_fwd(proj, conv_w, pool_w, pool_scale, *, name):
    s, _ = proj.shape
    dc = conv_w.shape[1]
    n_groups, cg, _ = pool_w.shape
    dp = n_groups * cg
    assert dc == dp and all(w & (w - 1) == 0 and w <= HALO for w in POOL_WINDOWS)
    ts = _tile(s, 256, HALO)
    halo_blocks = ts // HALO

    def body(b_ref, c_ref, xt_ref, v_ref, ch_ref, xth_ref, vh_ref, cw_ref, pw_ref, ps_ref, y_ref):
        i = pl.program_id(0)
        has_past = i > 0
        for cc in range(dc // cg):
            cols = slice(cc * cg, (cc + 1) * cg)
            u_ext = jnp.concatenate(
                [
                    jnp.where(has_past, ch_ref[:, cols].astype(F32) * xth_ref[:, cols].astype(F32), 0.0),
                    c_ref[:, cols].astype(F32) * xt_ref[:, cols].astype(F32),
                ],
                axis=0,
            )
            conv = (
                cw_ref[2:3, cols] * u_ext[HALO:]
                + cw_ref[1:2, cols] * pltpu.roll(u_ext, 1, 0)[HALO:]
                + cw_ref[0:1, cols] * pltpu.roll(u_ext, 2, 0)[HALO:]
            )
            y_ref[:, cols] = (b_ref[:, cols].astype(F32) * conv).astype(BF16)
        for gi, window in enumerate(POOL_WINDOWS):
            cols = slice(gi * cg, (gi + 1) * cg)
            v_ext = jnp.concatenate(
                [jnp.where(has_past, vh_ref[:, cols].astype(F32), 0.0), v_ref[:, cols].astype(F32)], axis=0
            )
            mean = _trailing_sums(v_ext, window)[HALO:] * _inverse_counts(i * ts, ts, window)
            diff = (mean - v_ext[HALO:]).astype(BF16)
            z = jnp.dot(diff, pw_ref[gi], preferred_element_type=F32)
            y_ref[:, dc + gi * cg : dc + (gi + 1) * cg] = (z * ps_ref[:, cols]).astype(BF16)

    def col(jc):
        return pl.BlockSpec((ts, dc), lambda i: (i, jc))

    def past(jc):
        return pl.BlockSpec((HALO, dc), lambda i: (jnp.maximum(i * halo_blocks - 1, 0), jc))

    return pl.pallas_call(
        body,
        name=name,
        grid=(s // ts,),
        in_specs=[
            col(0), col(1), col(2), col(3), past(1), past(2), past(3),
            pl.BlockSpec((8, dc), lambda i: (0, 0)),
            pl.BlockSpec((n_groups, cg, cg), lambda i: (0, 0, 0)),
            pl.BlockSpec((1, dp), lambda i: (0, 0)),
        ],
        out_specs=pl.BlockSpec((ts, dc + dp), lambda i: (i, 0)),
        out_shape=_shape((s, dc + dp), BF16),
        compiler_params=_compute_params("parallel"),
    )(proj, proj, proj, proj, proj, proj, proj, conv_w, pool_w, pool_scale)


def _mixer_bwd(dy, proj, conv_w, pool_w, pool_scale, *, name, deps=()):
    s, e = proj.shape
    dc = conv_w.shape[1]
    n_groups, cg, _ = pool_w.shape
    dp = n_groups * cg
    ts = _tile(s, 256, HALO)
    halo_blocks = ts // HALO
    n_tiles = s // ts
    n_halo_blocks = s // HALO
    n_ext = ts + HALO

    def body(dyc_ref, dyp_ref, dycn_ref, dypn_ref, b_ref, c_ref, xt_ref, v_ref, bn_ref, ch_ref, xth_ref, vh_ref,
             cw_ref, pw_ref, ps_ref, dproj_ref, dcw_ref, dpw_ref, dps_ref):
        i = pl.program_id(0)
        has_past = i > 0
        has_next = i < n_tiles - 1

        @pl.when(i == 0)
        def _():
            dcw_ref[...] = jnp.zeros_like(dcw_ref)
            dpw_ref[...] = jnp.zeros_like(dpw_ref)
            dps_ref[...] = jnp.zeros_like(dps_ref)

        for cc in range(dc // cg):
            cols = slice(cc * cg, (cc + 1) * cg)
            c_now, xt_now = c_ref[:, cols].astype(F32), xt_ref[:, cols].astype(F32)
            u_ext = jnp.concatenate(
                [jnp.where(has_past, ch_ref[:, cols].astype(F32) * xth_ref[:, cols].astype(F32), 0.0), c_now * xt_now],
                axis=0,
            )
            u0 = u_ext[HALO:]
            u1 = pltpu.roll(u_ext, 1, 0)[HALO:]
            u2 = pltpu.roll(u_ext, 2, 0)[HALO:]
            dyc = dyc_ref[:, cols]
            conv = cw_ref[2:3, cols] * u0 + cw_ref[1:2, cols] * u1 + cw_ref[0:1, cols] * u2
            dproj_ref[:, cols] = (dyc * conv).astype(BF16)
            dconv = dyc * b_ref[:, cols].astype(F32)
            dconv_ext = jnp.concatenate(
                [dconv, jnp.where(has_next, dycn_ref[:, cols] * bn_ref[:, cols].astype(F32), 0.0)], axis=0
            )
            du = (
                cw_ref[2:3, cols] * dconv
                + cw_ref[1:2, cols] * pltpu.roll(dconv_ext, n_ext - 1, 0)[:ts]
                + cw_ref[0:1, cols] * pltpu.roll(dconv_ext, n_ext - 2, 0)[:ts]
            )
            dproj_ref[:, dc + cc * cg : dc + (cc + 1) * cg] = (du * xt_now).astype(BF16)
            dproj_ref[:, 2 * dc + cc * cg : 2 * dc + (cc + 1) * cg] = (du * c_now).astype(BF16)
            dcw_ref[0:1, cols] += jnp.sum(dconv * u2, axis=0, keepdims=True)
            dcw_ref[1:2, cols] += jnp.sum(dconv * u1, axis=0, keepdims=True)
            dcw_ref[2:3, cols] += jnp.sum(dconv * u0, axis=0, keepdims=True)

        for gi, window in enumerate(POOL_WINDOWS):
            cols = slice(gi * cg, (gi + 1) * cg)
            v_ext = jnp.concatenate(
                [jnp.where(has_past, vh_ref[:, cols].astype(F32), 0.0), v_ref[:, cols].astype(F32)], axis=0
            )
            mean = _trailing_sums(v_ext, window)[HALO:] * _inverse_counts(i * ts, ts, window)
            diff = (mean - v_ext[HALO:]).astype(BF16)
            z = jnp.dot(diff, pw_ref[gi], preferred_element_type=F32)
            dyp = dyp_ref[:, cols]
            dps_ref[:, cols] += jnp.sum(dyp * z, axis=0, keepdims=True)
            scale = ps_ref[:, cols]
            dz_ext = jnp.concatenate([dyp * scale, jnp.where(has_next, dypn_ref[:, cols] * scale, 0.0)], axis=0)
            dz_ext = dz_ext.astype(BF16)
            dpw_ref[gi] += lax.dot_general(
                diff, dz_ext[:ts], (((0,), (0,)), ((), ())), preferred_element_type=F32
            )
            ddiff_ext = lax.dot_general(
                dz_ext, pw_ref[gi], (((1,), (1,)), ((), ())), preferred_element_type=F32
            )
            q_ext = ddiff_ext * _inverse_counts(i * ts, n_ext, window)
            dv = _leading_sums(q_ext, window)[:ts] - ddiff_ext[:ts]
            dproj_ref[:, 3 * dc + gi * cg : 3 * dc + (gi + 1) * cg] = dv.astype(BF16)

    def col(jc):
        return pl.BlockSpec((ts, dc), lambda i: (i, jc))

    def past(jc):
        return pl.BlockSpec((HALO, dc), lambda i: (jnp.maximum(i * halo_blocks - 1, 0), jc))

    def following(jc):
        return pl.BlockSpec((HALO, dc), lambda i: (jnp.minimum((i + 1) * halo_blocks, n_halo_blocks - 1), jc))

    return pl.pallas_call(
        _behind(body, deps),
        name=name,
        grid=(n_tiles,),
        in_specs=[ANY_SPEC] * len(deps) + [
            col(0), col(1), following(0), following(1),
            col(0), col(1), col(2), col(3), following(0), past(1), past(2), past(3),
            pl.BlockSpec((8, dc), lambda i: (0, 0)),
            pl.BlockSpec((n_groups, cg, cg), lambda i: (0, 0, 0)),
            pl.BlockSpec((1, dp), lambda i: (0, 0)),
        ],
        out_specs=[
            pl.BlockSpec((ts, e), lambda i: (i, 0)),
            pl.BlockSpec((8, dc), lambda i: (0, 0)),
            pl.BlockSpec((n_groups, cg, cg), lambda i: (0, 0, 0)),
            pl.BlockSpec((1, dp), lambda i: (0, 0)),
        ],
        out_shape=[_shape((s, e), BF16), _shape((8, dc), F32), _shape((n_groups, cg, cg), F32), _shape((1, dp), F32)],
        compiler_params=_compute_params("arbitrary"),
    )(*deps, dy, dy, dy, dy, proj, proj, proj, proj, proj, proj, proj, proj, conv_w, pool_w, pool_scale)


def _cast_bf16(a, layer, *, name):
    _, rows, cols = a.shape
    tr = _tile(rows, 512, SUBLANES_BF16)

    def body(a_ref, o_ref):
        o_ref[...] = a_ref[...].astype(BF16)

    return pl.pallas_call(
        body, name=name, grid=(rows // tr,),
        in_specs=[pl.BlockSpec((None, tr, cols), lambda i: (layer, i, 0))],
        out_specs=pl.BlockSpec((None, tr, cols), lambda i: (0, i, 0)),
        out_shape=_shape((1, rows, cols), BF16),
        compiler_params=_compute_params("parallel"),
    )(a)


def _core_index():
    return lax.axis_index("c").astype(jnp.int32).reshape((1,))


def _add_sibling_half(grads, received, *, name, deps=()):
    nsh, r, c = grads.shape
    hr = r // 2
    tr = _tile(hr, 512, SUBLANES_BF16)
    tiles = hr // tr

    def body(core_ref, *refs):
        g_ref, r_ref, o_ref = refs[len(deps):]
        o_ref[...] = (g_ref[...].astype(F32) + r_ref[...].astype(F32)).astype(BF16)

    half = pl.BlockSpec((None, tr, c), lambda sh, t, core: (sh, t, 0))
    return pl.pallas_call(
        body,
        name=name,
        grid_spec=pltpu.PrefetchScalarGridSpec(
            num_scalar_prefetch=1,
            grid=(nsh, tiles),
            in_specs=[ANY_SPEC] * len(deps)
            + [pl.BlockSpec((None, tr, c), lambda sh, t, core: (sh, core[0] * tiles + t, 0)), half],
            out_specs=half,
        ),
        out_shape=_shape((nsh, hr, c), BF16),
        compiler_params=_compute_params("parallel", "parallel"),
    )(_core_index(), *deps, grads, received)


def _chip_index_operand():
    return _chip_index(lax.axis_index("x"), lax.axis_index("y")).astype(jnp.int32).reshape((1,))


def _sum_chip_partials(own, received, *, name, deps=()):
    nsh, hr, c = received.shape
    tr = _tile(hr, 256, SUBLANES_BF16)

    def body(chip_ref, *refs):
        own_ref, p_ref, o_ref = refs[len(deps):]
        mine = chip_ref[0]
        total = None
        for sh in range(nsh):
            term = jnp.where(mine == sh, own_ref[...], p_ref[sh]).astype(F32)
            total = term if total is None else total + term
        o_ref[...] = total

    return pl.pallas_call(
        body,
        name=name,
        grid_spec=pltpu.PrefetchScalarGridSpec(
            num_scalar_prefetch=1,
            grid=(hr // tr,),
            in_specs=[ANY_SPEC] * len(deps) + [
                pl.BlockSpec((None, tr, c), lambda t, chip: (chip[0], t, 0)),
                pl.BlockSpec((nsh, tr, c), lambda t, chip: (0, t, 0)),
            ],
            out_specs=pl.BlockSpec((tr, c), lambda t, chip: (t, 0)),
        ),
        out_shape=_shape((hr, c), F32),
        compiler_params=_compute_params("parallel"),
    )(_chip_index_operand(), *deps, own, received)


def _adamw(grad, w, m, v, layer, carried, *, name, deps=()):
    n_layers, r, c = w.shape
    in_halves = isinstance(grad, (tuple, list))
    tr = _tile(r // 2 if in_halves else r, 256, 8)
    half_tiles = (r // 2) // tr if in_halves else 0
    bias1 = 1.0 - ADAM_B1 ** ADAM_STEP
    bias2 = 1.0 - ADAM_B2 ** ADAM_STEP
    n_grads = 2 if in_halves else 1

    def body(core_ref, *refs):
        refs = refs[len(deps):]
        w_ref, m_ref, v_ref = refs[n_grads : n_grads + 3]
        go_ref, d_ref, mo_ref, vo_ref, done_ref = refs[-5:]
        done_ref[...] = jnp.zeros_like(done_ref)
        if in_halves:
            in_my_half = pl.program_id(0) // half_tiles == core_ref[0]
            g = jnp.where(in_my_half, refs[0][...], refs[1][...])
        else:
            g = refs[0][...]
        m_new = ADAM_B1 * m_ref[...] + (1.0 - ADAM_B1) * g
        v_new = ADAM_B2 * v_ref[...] + (1.0 - ADAM_B2) * (g * g)
        go_ref[...] = g
        mo_ref[...] = m_new
        vo_ref[...] = v_new
        d_ref[...] = -ADAM_LR * ((m_new / bias1) / (jnp.sqrt(v_new / bias2) + ADAM_EPS) + ADAM_WD * w_ref[...])

    def half_spec(mine):
        def index(t, core):
            first = (core[0] if mine else 1 - core[0]) * half_tiles
            return (jnp.clip(t - first, 0, half_tiles - 1), 0)

        return pl.BlockSpec((tr, c), index)

    layer_tile = pl.BlockSpec((None, tr, c), lambda t, core: (layer, t, 0))
    if in_halves:
        grad_specs, grads = [half_spec(True), half_spec(False)], list(grad)
    else:
        grad_specs, grads = [pl.BlockSpec((tr, c), lambda t, core: (t, 0))], [grad]
    in_specs = [ANY_SPEC] * len(deps) + grad_specs + [layer_tile] * 3
    args = list(deps) + grads + [w, m, v]
    aliases = {}
    if carried is not None:
        aliases = {1 + len(args) + n: n for n in range(4)}
        in_specs += [HBM_SPEC] * 4
        args += list(carried)
    *outs, done = pl.pallas_call(
        body,
        name=name,
        grid_spec=pltpu.PrefetchScalarGridSpec(
            num_scalar_prefetch=1,
            grid=(r // tr,),
            in_specs=in_specs,
            out_specs=[layer_tile] * 4 + [pl.BlockSpec((8, LANES), lambda t, core: (0, 0))],
        ),
        out_shape=[_shape((n_layers, r, c), F32)] * 4 + [_shape((8, LANES), F32)],
        input_output_aliases=aliases,
        compiler_params=_compute_params("arbitrary"),
    )(_core_index(), *args)
    return outs, done


def _place():
    x, y, c = (lax.axis_index(a) for a in MESH_AXES)
    other_chips = [(1 - x, y), (x, 1 - y), (1 - x, 1 - y)]
    return x, y, c, other_chips


def _chip_index(x, y):
    return 2 * x + y


def _gather_small(conv_rows, pool_rows, *, name):
    blocks = (conv_rows, pool_rows)
    n = len(blocks)

    def body(*refs):
        ins, outs = refs[:n], refs[n : 2 * n]
        send_sems, recv_sems = refs[2 * n :]
        x, y, c, other_chips = _place()
        mine = _chip_index(x, y)
        peers = [(x, y, 1 - c)] + [(px, py, c) for px, py in other_chips]
        _handshake(peers)
        sends = []
        for a in range(n):
            for j, peer in enumerate(peers):
                cp = pltpu.make_async_remote_copy(
                    src_ref=ins[a], dst_ref=outs[a].at[mine], send_sem=send_sems.at[a, j], recv_sem=recv_sems.at[a, j],
                    device_id=peer, device_id_type=MESH_ID,
                )
                cp.start()
                sends.append(cp)
        for a in range(n):
            for j, (px, py, _) in enumerate(peers):
                pltpu.make_async_remote_copy(
                    src_ref=ins[a], dst_ref=outs[a].at[_chip_index(px, py)], send_sem=send_sems.at[a, j],
                    recv_sem=recv_sems.at[a, j], device_id=peers[j], device_id_type=MESH_ID,
                ).wait_recv()
        for cp in sends:
            cp.wait_send()

    return _sequencer_call(
        body,
        [_shape((N_CHIPS,) + blk.shape, F32) for blk in blocks],
        [pltpu.SemaphoreType.DMA((n, 4)), pltpu.SemaphoreType.DMA((n, 4))],
        GATHER_SMALL_COLLECTIVE_ID,
        name,
    )(*blocks)


def _allreduce_small(vec, *, name):
    rows, n = vec.shape
    n_dev = 8

    def body(v_ref, o_ref, slots, send_sems, recv_sems):
        x, y, c, _ = _place()
        me = 4 * x + 2 * y + c
        slots[me] = v_ref[...]
        sends = []
        for mask in range(1, n_dev):
            fx, fy, fc = (mask >> 2) & 1, (mask >> 1) & 1, mask & 1
            peer = (x ^ fx, y ^ fy, c ^ fc)
            cp = pltpu.make_async_remote_copy(
                src_ref=v_ref, dst_ref=slots.at[me], send_sem=send_sems.at[mask - 1], recv_sem=recv_sems.at[mask - 1],
                device_id=peer, device_id_type=MESH_ID,
            )
            cp.start()
            sends.append(cp)
        for mask in range(1, n_dev):
            fx, fy, fc = (mask >> 2) & 1, (mask >> 1) & 1, mask & 1
            peer = (x ^ fx, y ^ fy, c ^ fc)
            pltpu.make_async_remote_copy(
                src_ref=v_ref, dst_ref=slots.at[4 * peer[0] + 2 * peer[1] + peer[2]], send_sem=send_sems.at[mask - 1],
                recv_sem=recv_sems.at[mask - 1], device_id=peer, device_id_type=MESH_ID,
            ).wait_recv()
        total = slots[0]
        for dev in range(1, n_dev):
            total = total + slots[dev]
        o_ref[...] = total
        for cp in sends:
            cp.wait_send()

    return pl.pallas_call(
        body,
        name=name,
        in_specs=[VMEM_SPEC],
        out_specs=VMEM_SPEC,
        out_shape=_shape((rows, n), F32),
        scratch_shapes=[
            pltpu.VMEM((n_dev, rows, n), F32),
            pltpu.SemaphoreType.DMA((n_dev - 1,)),
            pltpu.SemaphoreType.DMA((n_dev - 1,)),
        ],
    )(vec)


def _handshake(peers):
    barrier = pltpu.get_barrier_semaphore()
    for peer in peers:
        pl.semaphore_signal(barrier, inc=1, device_id=peer, device_id_type=MESH_ID)
    pl.semaphore_wait(barrier, len(peers))


def _sequencer_call(body, out_type, scratch_types, collective_id, name):
    return pl.kernel(
        body,
        name=name,
        out_type=out_type,
        mesh=plsc.ScalarSubcoreMesh(axis_name="sequencer", num_cores=1),
        scratch_types=scratch_types,
        compiler_params=pltpu.CompilerParams(collective_id=collective_id),
    )


GATHER_COLLECTIVE_ID = 1
EXCHANGE_COLLECTIVE_ID = 2
SCATTER_COLLECTIVE_ID = 3
SHARE_COLLECTIVE_ID = 4
GATHER_SMALL_COLLECTIVE_ID = 5


def _gather_weights(stacked, layer, *, name):
    n = len(stacked)

    def body(*refs):
        ins, outs = refs[:n], refs[n : 2 * n]
        own_sems, send_sems, recv_sems, pass_send_sems, pass_recv_sems = refs[2 * n :]
        x, y, c, other_chips = _place()
        mine = _chip_index(x, y)
        sibling = (x, y, 1 - c)
        _handshake([sibling] + [(px, py, c) for px, py in other_chips])
        pending = []

        def send_my_half(a):
            hr = ins[a].shape[1] // 2
            my_half = pl.ds(c * hr, hr)
            cp = pltpu.make_async_remote_copy(
                src_ref=ins[a].at[layer], dst_ref=outs[a].at[mine], send_sem=own_sems.at[0, a],
                recv_sem=own_sems.at[1, a], device_id=sibling, device_id_type=MESH_ID,
            )
            cp.start()
            pending.append(cp.wait)
            for j, (px, py) in enumerate(other_chips):
                cp = pltpu.make_async_remote_copy(
                    src_ref=ins[a].at[layer, my_half], dst_ref=outs[a].at[mine, my_half], send_sem=send_sems.at[a, j],
                    recv_sem=recv_sems.at[a, j], device_id=(px, py, c), device_id_type=MESH_ID,
                )
                cp.start()
                pending.append(cp.wait_send)

        for a in range(min(2, n)):
            send_my_half(a)
        for a in range(n):
            hr = ins[a].shape[1] // 2
            my_half = pl.ds(c * hr, hr)
            for j, (px, py) in enumerate(other_chips):
                landed = outs[a].at[_chip_index(px, py), my_half]
                pltpu.make_async_remote_copy(
                    src_ref=landed, dst_ref=landed, send_sem=send_sems.at[a, j], recv_sem=recv_sems.at[a, j],
                    device_id=(px, py, c), device_id_type=MESH_ID,
                ).wait_recv()
                cp = pltpu.make_async_remote_copy(
                    src_ref=landed, dst_ref=landed, send_sem=pass_send_sems.at[a, j], recv_sem=pass_recv_sems.at[a, j],
                    device_id=sibling, device_id_type=MESH_ID,
                )
                cp.start()
                pending.append(cp.wait_send)
            if a + 2 < n:
                send_my_half(a + 2)
        for a in range(n):
            hr = ins[a].shape[1] // 2
            sibling_half = pl.ds((1 - c) * hr, hr)
            for j, (px, py) in enumerate(other_chips):
                passed = outs[a].at[_chip_index(px, py), sibling_half]
                pltpu.make_async_remote_copy(
                    src_ref=passed, dst_ref=passed, send_sem=pass_send_sems.at[a, j], recv_sem=pass_recv_sems.at[a, j],
                    device_id=sibling, device_id_type=MESH_ID,
                ).wait_recv()
        for wait in pending:
            wait()

    return _sequencer_call(
        body,
        [_shape((N_CHIPS,) + a.shape[1:], BF16) for a in stacked],
        [pltpu.SemaphoreType.DMA((2, n))] + [pltpu.SemaphoreType.DMA((n, 3))] * 4,
        GATHER_COLLECTIVE_ID,
        name,
    )(*stacked)


def _exchange_halves(grads, *, name):
    n = len(grads)

    def body(*refs):
        ins, outs = refs[:n], refs[n : 2 * n]
        send_sems, recv_sems = refs[2 * n :]
        x, y, c, _ = _place()
        sibling = (x, y, 1 - c)
        _handshake([sibling])
        copies = []
        for a in range(n):
            hr = ins[a].shape[1] // 2
            cp = pltpu.make_async_remote_copy(
                src_ref=ins[a].at[:, pl.ds((1 - c) * hr, hr), :], dst_ref=outs[a], send_sem=send_sems.at[a],
                recv_sem=recv_sems.at[a], device_id=sibling, device_id_type=MESH_ID,
            )
            cp.start()
            copies.append(cp)
        for cp in copies:
            cp.wait()

    return _sequencer_call(
        body,
        [_shape((g.shape[0], g.shape[1] // 2, g.shape[2]), BF16) for g in grads],
        [pltpu.SemaphoreType.DMA((n,)), pltpu.SemaphoreType.DMA((n,))],
        EXCHANGE_COLLECTIVE_ID,
        name,
    )(*grads)


def _scatter_partials(partials, *, name):
    n = len(partials)

    def body(*refs):
        ins, outs = refs[:n], refs[n : 2 * n]
        send_sems, recv_sems = refs[2 * n :]
        x, y, c, other_chips = _place()
        mine = _chip_index(x, y)
        _handshake([(px, py, c) for px, py in other_chips])
        pending = []
        for a in range(n):
            for j, (px, py) in enumerate(other_chips):
                cp = pltpu.make_async_remote_copy(
                    src_ref=ins[a].at[_chip_index(px, py)], dst_ref=outs[a].at[mine], send_sem=send_sems.at[a, j],
                    recv_sem=recv_sems.at[a, j], device_id=(px, py, c), device_id_type=MESH_ID,
                )
                cp.start()
                pending.append(cp.wait_send)
        for a in range(n):
            for j, (px, py) in enumerate(other_chips):
                landed = outs[a].at[_chip_index(px, py)]
                pltpu.make_async_remote_copy(
                    src_ref=landed, dst_ref=landed, send_sem=send_sems.at[a, j], recv_sem=recv_sems.at[a, j],
                    device_id=(px, py, c), device_id_type=MESH_ID,
                ).wait_recv()
        for wait in pending:
            wait()

    return _sequencer_call(
        body,
        [_shape(p.shape, BF16) for p in partials],
        [pltpu.SemaphoreType.DMA((n, 3)), pltpu.SemaphoreType.DMA((n, 3))],
        SCATTER_COLLECTIVE_ID,
        name,
    )(*partials)


def _share_halves(halves, *, name):
    n = len(halves)

    def body(*refs):
        ins, outs = refs[:n], refs[n : 2 * n]
        send_sems, recv_sems = refs[2 * n :]
        x, y, c, _ = _place()
        sibling = (x, y, 1 - c)
        _handshake([sibling])
        copies = []
        for a in range(n):
            cp = pltpu.make_async_remote_copy(
                src_ref=ins[a], dst_ref=outs[a], send_sem=send_sems.at[a], recv_sem=recv_sems.at[a],
                device_id=sibling, device_id_type=MESH_ID,
            )
            cp.start()
            copies.append(cp)
        for cp in copies:
            cp.wait()

    return _sequencer_call(
        body,
        [_shape(h.shape, F32) for h in halves],
        [pltpu.SemaphoreType.DMA((n,)), pltpu.SemaphoreType.DMA((n,))],
        SHARE_COLLECTIVE_ID,
        name,
    )(*halves)


class _ReduceToOwner:
    def __init__(self, grads, tag):
        self.grads, self.tag = grads, tag
        self.received = _exchange_halves(grads, name=f"exchange_halves_{tag}")

    def add_sibling(self, after):
        partials = [
            _add_sibling_half(g, r, name=f"add_sibling_{self.tag}_{a}", deps=after)
            for a, (g, r) in enumerate(zip(self.grads, self.received))
        ]
        self.partials = partials
        self.slots = _scatter_partials(partials, name=f"scatter_partials_{self.tag}")
        return partials

    def sum_chips(self, after):
        halves = [
            _sum_chip_partials(p, s, name=f"sum_partials_{self.tag}_{a}", deps=after)
            for a, (p, s) in enumerate(zip(self.partials, self.slots))
        ]
        self.halves = halves
        self.theirs = _share_halves(halves, name=f"share_halves_{self.tag}")
        return halves

    def totals(self):
        return list(zip(self.halves, self.theirs))


def kernel(x, w_in, conv_w, pool_w, pool_scale, w_out, norm_mix, norm_mlp, w_up, w_down, norm_final, loss_target, m_w_in, m_conv_w, m_pool_w, m_pool_scale, m_w_out, m_norm_mix, m_norm_mlp, m_w_up, m_w_down, m_norm_final, v_w_in, v_conv_w, v_pool_w, v_pool_scale, v_w_out, v_norm_mix, v_norm_mlp, v_w_up, v_w_down, v_norm_final):
    n_layers, d, _ = w_in.shape
    s = x.shape[1]
    dc = conv_w.shape[2] * N_CHIPS
    n_groups, cg_rows, cg = pool_w.shape[1:]
    dp = n_groups * cg
    x0 = x.reshape(s, d)
    target = loss_target.reshape(s, d)

    big = [w_in, w_out, w_up, w_down]

    conv_rows = jnp.pad(conv_w, ((0, 0), (0, 8 - CONV_TAPS), (0, 0))).reshape(n_layers * 8, -1)
    pool_rows = pool_w.reshape(n_layers * n_groups * cg_rows, cg)
    conv_all, pool_all = _gather_small(conv_rows, pool_rows, name="gather_small")
    conv_full = conv_all.reshape(N_CHIPS, n_layers, 8, -1).transpose(1, 2, 0, 3).reshape(n_layers, 8, dc)
    pool_full = (
        pool_all.reshape(N_CHIPS, n_layers, n_groups, cg_rows, cg).transpose(1, 2, 0, 3, 4)
        .reshape(n_layers, n_groups, cg, cg).astype(BF16)
    )

    saved = []
    xl = x0
    for l in range(n_layers):
        win_g, wout_g, wup_g, wdown_g = [
            _gather_weights([_cast_bf16(w, l, name=f"cast_l{l}_{t}")], 0, name=f"gather_weights_l{l}_{t}")[0]
            for t, w in enumerate(big)
        ]
        gain_mix = norm_mix[l].reshape(1, d)
        gain_mlp = norm_mlp[l].reshape(1, d)
        scale = pool_scale[l].reshape(1, dp)
        h1 = _rmsnorm(xl, gain_mix, name=f"norm_mix_l{l}")
        proj = _matmul_cols(h1, win_g, relu2=False, name=f"in_proj_l{l}")
        y = _mixer_fwd(proj, conv_full[l], pool_full[l], scale, name=f"mixer_fwd_l{l}")
        x_mid = _matmul_residual(y, wout_g.reshape(-1, d), xl, name=f"out_proj_l{l}")
        h2 = _rmsnorm(x_mid, gain_mlp, name=f"norm_mlp_l{l}")
        u, u2 = _matmul_cols(h2, wup_g, relu2=True, name=f"up_proj_l{l}")
        x_next = _matmul_residual(u2, wdown_g.reshape(-1, d), x_mid, name=f"down_proj_l{l}")
        saved.append((xl, h1, proj, y, x_mid, h2, u, u2, win_g, wout_g, wup_g, wdown_g, gain_mix, gain_mlp, scale))
        xl = x_next

    loss_part, dx, dx_bf16, d_norm_final = _loss_and_grad(xl, norm_final.reshape(1, d), target, name="loss_head")
    loss = lax.psum(loss_part[0, 0], MESH_AXES)

    small_grads = [None] * n_layers
    carried = [None] * 5
    pool_params = tuple(p.reshape(n_layers, n_groups * cg_rows, cg) for p in (pool_w, m_pool_w, v_pool_w))
    params = [(w_in, m_w_in, v_w_in), (w_out, m_w_out, v_w_out), (w_up, m_w_up, v_w_up), (w_down, m_w_down, v_w_down),
              pool_params]
    DOWN, UP_OUT, IN_POOL = (3,), (2, 1), (0, 4)

    def update(reduce, which, layer, after):
        dones = []
        for a, total in zip(which, reduce.totals()):
            w, m, v = params[a]
            carried[a], done = _adamw(total, w, m, v, layer, carried[a], name=f"adamw_{a}_l{layer}", deps=after)
            dones.append(done)
        return dones

    up_out_above = in_pool_above = None
    for l in reversed(range(n_layers)):
        xl, h1, proj, y, x_mid, h2, u, u2, win_g, wout_g, wup_g, wdown_g, gain_mix, gain_mlp, scale = saved[l]
        above = up_out_above is not None
        g_down = _matmul_tn(u2, dx_bf16, n_shards=N_CHIPS, shard_cols=False, name=f"grad_w_down_l{l}")
        down = _ReduceToOwner([g_down], f"down_l{l}")
        deps = up_out_above.sum_chips([g_down]) if above else [g_down]
        da = _matmul_nt(dx_bf16, wdown_g.reshape(-1, d), u=u, name=f"grad_act_l{l}", deps=deps)
        g_up = _matmul_tn(
            h2, da, n_shards=N_CHIPS, shard_cols=True, name=f"grad_w_up_l{l}", deps=down.add_sibling([da])
        )
        deps = in_pool_above.sum_chips([g_up]) if above else [g_up]
        dx_mid, dx_mid_bf16, d_gain_mlp = _matmul_nt_norm_bwd(
            da, wup_g, x_mid, gain_mlp, dx, name=f"grad_mid_l{l}", deps=deps
        )
        g_out = _matmul_tn(
            y, dx_mid_bf16, n_shards=N_CHIPS, shard_cols=False, name=f"grad_w_out_l{l}",
            deps=down.sum_chips([dx_mid_bf16]),
        )
        up_out = _ReduceToOwner([g_up, g_out], f"up_out_l{l}")
        deps = update(up_out_above, UP_OUT, l + 1, [g_out]) if above else [g_out]
        dy = _matmul_nt(dx_mid_bf16, wout_g.reshape(-1, d), name=f"grad_mixed_l{l}", deps=deps)
        dproj, d_conv, d_pool, d_scale = _mixer_bwd(
            dy, proj, conv_full[l], pool_full[l], scale, name=f"mixer_bwd_l{l}", deps=up_out.add_sibling([dy])
        )
        g_in = _matmul_tn(h1, dproj, n_shards=N_CHIPS, shard_cols=True, name=f"grad_w_in_l{l}")
        g_pool = (
            d_pool.reshape(n_groups, N_CHIPS, cg_rows, cg).transpose(1, 0, 2, 3)
            .reshape(N_CHIPS, n_groups * cg_rows, cg).astype(BF16)
        )
        in_pool = _ReduceToOwner([g_in, g_pool], f"in_pool_l{l}")
        deps = update(down, DOWN, l, [g_in])
        if above:
            deps = update(in_pool_above, IN_POOL, l + 1, deps)
        dx, dx_bf16, d_gain_mix = _matmul_nt_norm_bwd(
            dproj, win_g, xl, gain_mix, dx_mid, name=f"grad_x_l{l}", deps=in_pool.add_sibling(deps)
        )
        small_grads[l] = jnp.concatenate(
            [d_conv[:CONV_TAPS].reshape(-1), d_scale.reshape(-1), d_gain_mix.reshape(-1), d_gain_mlp.reshape(-1)]
        )
        up_out_above, in_pool_above = up_out, in_pool
    deps = up_out_above.sum_chips([dx_bf16])
    deps = in_pool_above.sum_chips(deps)
    deps = update(up_out_above, UP_OUT, 0, deps)
    update(in_pool_above, IN_POOL, 0, deps)

    vec = jnp.concatenate(small_grads + [d_norm_final.reshape(-1)])
    vec = _allreduce_small(vec.reshape(8, -1), name="allreduce_small").reshape(-1)
    per_layer = vec[: n_layers * (CONV_TAPS * dc + dp + 2 * d)].reshape(n_layers, -1)
    chip = _chip_index(lax.axis_index("x"), lax.axis_index("y"))
    dcs = dc // N_CHIPS
    g_conv = lax.dynamic_slice_in_dim(per_layer[:, : CONV_TAPS * dc].reshape(n_layers, CONV_TAPS, dc), chip * dcs, dcs, axis=2)
    g_scale = per_layer[:, CONV_TAPS * dc : CONV_TAPS * dc + dp]
    g_mix = per_layer[:, CONV_TAPS * dc + dp : CONV_TAPS * dc + dp + d]
    g_mlp = per_layer[:, CONV_TAPS * dc + dp + d :]
    g_final = vec[n_layers * (CONV_TAPS * dc + dp + 2 * d) :]

    def small_adamw(g, w, m, v, tag):
        flat = lambda t: t.reshape(1, -1, t.shape[-1])
        out, _ = _adamw(flat(g)[0], flat(w), flat(m), flat(v), 0, None, name=f"adamw_{tag}")
        return [o.reshape(w.shape) for o in out]

    o_conv = small_adamw(g_conv, conv_w, m_conv_w, v_conv_w, "conv_w")
    o_scale = small_adamw(g_scale, pool_scale, m_pool_scale, v_pool_scale, "pool_scale")
    o_mix = small_adamw(g_mix, norm_mix, m_norm_mix, v_norm_mix, "norm_mix")
    o_mlp = small_adamw(g_mlp, norm_mlp, m_norm_mlp, v_norm_mlp, "norm_mlp")
    o_final = small_adamw(g_final, norm_final, m_norm_final, v_norm_final, "norm_final")
    o_in, o_out, o_up, o_down, o_pool = carried
    o_pool = [o.reshape(pool_w.shape) for o in o_pool]

    ordered = [o_in, o_conv, o_pool, o_scale, o_out, o_mix, o_mlp, o_up, o_down, o_final]
    return (loss, dx.reshape(x.shape), *[o[0] for o in ordered], *[o[1] for o in ordered], *[o[2] for o in ordered],
            *[o[3] for o in ordered])
```

```python
import functools

import jax
import jax.numpy as jnp
from jax import lax
from jax.experimental import pallas as pl
from jax.experimental.pallas import tpu as pltpu
from jax.experimental.pallas import tpu_sc as plsc

F32 = jnp.float32
BF16 = jnp.bfloat16

EPS = 1e-6
POOL_WINDOWS = (2, 4, 8, 16)
CONV_TAPS = 3
HALO = 16

ADAM_LR = 0.001
ADAM_B1 = 0.9
ADAM_B2 = 0.999
ADAM_EPS = 1e-08
ADAM_WD = 0.01
ADAM_STEP = 10

N_CHIPS = 4
MESH_AXES = ("x", "y", "c")
V7X_VMEM_LIMIT_BYTES = 56 * 1024 * 1024
SUBLANES_BF16 = 16
LANES = 128

HBM_SPEC = pl.BlockSpec(memory_space=pltpu.HBM)
VMEM_SPEC = pl.BlockSpec(memory_space=pltpu.VMEM)
MESH_ID = pl.DeviceIdType.MESH


def _tile(dim, target, align):
    if dim <= target:
        return dim
    t = (target // align) * align
    while dim % t:
        t -= align
    assert t > 0, (dim, target, align)
    return t


def _compute_params(*semantics):
    return pltpu.CompilerParams(dimension_semantics=semantics, vmem_limit_bytes=V7X_VMEM_LIMIT_BYTES)


def _shape(shape, dtype):
    return jax.ShapeDtypeStruct(shape, dtype)


ANY_SPEC = pl.BlockSpec(memory_space=pl.ANY)


def _behind(body, deps):
    return lambda *refs: body(*refs[len(deps):])


def _rmsnorm_bwd(dh, x, gain, dres):
    r = lax.rsqrt(jnp.mean(x * x, axis=-1, keepdims=True) + EPS)
    xn = x * r
    dgain = jnp.sum(dh * xn, axis=0, keepdims=True)
    dxn = dh * gain
    dx = r * (dxn - xn * jnp.mean(dxn * xn, axis=-1, keepdims=True))
    if dres is not None:
        dx = dx + dres
    return dx, dgain


MXU_ROWS = 512


def _rmsnorm(x, gain, *, name):
    s, d = x.shape
    tm = _tile(s, 512, SUBLANES_BF16)

    def body(x_ref, g_ref, h_ref):
        xf = x_ref[...]
        r = lax.rsqrt(jnp.mean(xf * xf, axis=-1, keepdims=True) + EPS)
        h_ref[...] = ((xf * r) * g_ref[...]).astype(BF16)

    row_tile = pl.BlockSpec((tm, d), lambda i: (i, 0))
    return pl.pallas_call(
        body, name=name, grid=(s // tm,), in_specs=[row_tile, pl.BlockSpec((1, d), lambda i: (0, 0))],
        out_specs=row_tile, out_shape=_shape((s, d), BF16), compiler_params=_compute_params("parallel"),
    )(x, gain)


def _matmul_cols(h, w, *, relu2, name):
    s, k = h.shape
    nsh, _, c = w.shape
    tm = _tile(s, 2048, MXU_ROWS)
    tn = _tile(c, 1024, LANES)
    rows = min(MXU_ROWS, tm)
    per_shard = c // tn

    def body(h_ref, w_ref, *out_refs):
        def step(t, carry):
            rs = pl.ds(pl.multiple_of(t * rows, rows), rows)
            acc = jnp.dot(h_ref[rs, :], w_ref[...], preferred_element_type=F32)
            if relu2:
                u = jnp.maximum(acc, 0.0)
                out_refs[0][rs, :] = u.astype(BF16)
                out_refs[1][rs, :] = (u * u).astype(BF16)
            else:
                out_refs[0][rs, :] = acc.astype(BF16)
            return carry

        lax.fori_loop(0, tm // rows, step, 0)

    tile_out = pl.BlockSpec((tm, tn), lambda i, j: (i, j))
    n = nsh * c
    outs = [_shape((s, n), BF16)] * (2 if relu2 else 1)
    result = pl.pallas_call(
        body,
        name=name,
        grid=(s // tm, n // tn),
        in_specs=[
            pl.BlockSpec((tm, k), lambda i, j: (i, 0)),
            pl.BlockSpec((None, k, tn), lambda i, j: (j // per_shard, 0, j % per_shard)),
        ],
        out_specs=[tile_out] * len(outs),
        out_shape=outs,
        compiler_params=_compute_params("parallel", "arbitrary"),
    )(h, w)
    return result if relu2 else result[0]


def _matmul_residual(a, w, res, *, name):
    s, k = a.shape
    _, n = w.shape
    tm = _tile(s, 1024, MXU_ROWS)
    tn = _tile(n, 1024, LANES)
    tk = _tile(k, 2048, LANES)
    rows = min(MXU_ROWS, tm)

    def body(a_ref, w_ref, r_ref, o_ref):
        kk = pl.program_id(2)

        @pl.when(kk == 0)
        def _():
            o_ref[...] = r_ref[...]

        def step(t, carry):
            rs = pl.ds(pl.multiple_of(t * rows, rows), rows)
            o_ref[rs, :] += jnp.dot(a_ref[rs, :], w_ref[...], preferred_element_type=F32)
            return carry

        lax.fori_loop(0, tm // rows, step, 0, unroll=True)

    return pl.pallas_call(
        body,
        name=name,
        grid=(s // tm, n // tn, k // tk),
        in_specs=[
            pl.BlockSpec((tm, tk), lambda i, j, kk: (i, kk)),
            pl.BlockSpec((tk, tn), lambda i, j, kk: (kk, j)),
            pl.BlockSpec((tm, tn), lambda i, j, kk: (i, j)),
        ],
        out_specs=pl.BlockSpec((tm, tn), lambda i, j, kk: (i, j)),
        out_shape=_shape((s, n), F32),
        compiler_params=_compute_params("parallel", "parallel", "arbitrary"),
    )(a, w, res)


def _matmul_tn(a, g, *, n_shards, shard_cols, name, deps=()):
    m, kd = a.shape
    _, n = g.shape
    r, c = (kd, n // n_shards) if shard_cols else (kd // n_shards, n)
    tr = _tile(kd, 2048, LANES)
    tn = _tile(c, 1024, LANES)
    tm = _tile(m, 2048, SUBLANES_BF16)
    nm = m // tm
    cols_per_shard = c // tn

    out_rows = min(MXU_ROWS, tr)

    def body(a_ref, g_ref, o_ref, acc_ref):
        mm = pl.program_id(2)

        @pl.when(mm == 0)
        def _():
            acc_ref[...] = jnp.zeros_like(acc_ref)

        for t in range(tr // out_rows):
            rs = slice(t * out_rows, (t + 1) * out_rows)
            acc_ref[rs, :] += lax.dot_general(
                a_ref[:, rs], g_ref[...], (((0,), (0,)), ((), ())), preferred_element_type=F32
            )

        @pl.when(mm == nm - 1)
        def _():
            o_ref[...] = acc_ref[...].astype(BF16)

    if shard_cols:
        out_spec = pl.BlockSpec((None, tr, tn), lambda i, j, mm: (j // cols_per_shard, i, j % cols_per_shard))
        out_shape = _shape((n_shards, kd, c), BF16)
    else:
        out_spec = pl.BlockSpec((tr, tn), lambda i, j, mm: (i, j))
        out_shape = _shape((kd, n), BF16)
    out = pl.pallas_call(
        _behind(body, deps),
        name=name,
        grid=(kd // tr, n // tn, nm),
        in_specs=[ANY_SPEC] * len(deps) + [
            pl.BlockSpec((tm, tr), lambda i, j, mm: (mm, i)),
            pl.BlockSpec((tm, tn), lambda i, j, mm: (mm, j)),
        ],
        out_specs=out_spec,
        out_shape=out_shape,
        scratch_shapes=[pltpu.VMEM((tr, tn), F32)],
        compiler_params=_compute_params("parallel", "parallel", "arbitrary"),
    )(*deps, a, g)
    return out.reshape(n_shards, r, c)


def _matmul_nt(g, w, *, u=None, name, deps=()):
    m, n = g.shape
    kd, _ = w.shape
    tm = _tile(m, 2048, MXU_ROWS)
    tj = _tile(kd, 1024, LANES)
    rows = min(MXU_ROWS, tm)

    def body(*refs):
        if u is None:
            g_ref, w_ref, o_ref = refs
        else:
            g_ref, w_ref, u_ref, o_ref = refs

        def step(t, carry):
            rs = pl.ds(pl.multiple_of(t * rows, rows), rows)
            prod = lax.dot_general(g_ref[rs, :], w_ref[...], (((1,), (1,)), ((), ())), preferred_element_type=F32)
            if u is None:
                o_ref[rs, :] = prod
            else:
                o_ref[rs, :] = (prod * (2.0 * u_ref[rs, :].astype(F32))).astype(BF16)
            return carry

        lax.fori_loop(0, tm // rows, step, 0)

    tile_out = pl.BlockSpec((tm, tj), lambda i, j: (i, j))
    in_specs = [pl.BlockSpec((tm, n), lambda i, j: (i, 0)), pl.BlockSpec((tj, n), lambda i, j: (j, 0))]
    args = [g, w]
    if u is not None:
        in_specs.append(tile_out)
        args.append(u)
    return pl.pallas_call(
        _behind(body, deps),
        name=name,
        grid=(m // tm, kd // tj),
        in_specs=[ANY_SPEC] * len(deps) + in_specs,
        out_specs=tile_out,
        out_shape=_shape((m, kd), F32 if u is None else BF16),
        compiler_params=_compute_params("parallel", "arbitrary"),
    )(*deps, *args)


EPILOGUE_ROWS = 128


def _matmul_nt_norm_bwd(g, w, x, gain, dres, *, name, deps=()):
    s, n = g.shape
    nsh, d, c = w.shape
    tm = _tile(s, 1024, MXU_ROWS)
    tk = _tile(c, 1024, LANES)
    per_shard = c // tk
    nk = n // tk
    n_blocks = s // tm
    mxu_rows = min(MXU_ROWS, tm)
    rows = min(EPILOGUE_ROWS, tm)
    piece = tm // nk
    assert piece * nk == tm and piece % 8 == 0

    def body(g_ref, w_ref, gain_ref, x_hbm, dres_hbm, dx_hbm, dxb_hbm, dg_ref, acc_ref, x_buf, dres_buf, dx_buf,
             dxb_buf, in_sems, out_sems):
        i = pl.program_id(0)
        kk = pl.program_id(1)

        @pl.when((kk == 0) & (i == 0))
        def _():
            dg_ref[...] = jnp.zeros_like(dg_ref)

        block_rows = pl.ds(pl.multiple_of(i * tm, tm), tm)
        pieces = pl.ds(pl.multiple_of(kk * piece, piece), piece)
        piece_rows = pl.ds(pl.multiple_of(i * tm + kk * piece, piece), piece)
        pltpu.make_async_copy(x_hbm.at[piece_rows, :], x_buf.at[pieces, :], in_sems.at[0]).start()
        pltpu.make_async_copy(dres_hbm.at[piece_rows, :], dres_buf.at[pieces, :], in_sems.at[1]).start()

        @pl.when(kk == 0)
        def _():
            acc_ref[...] = jnp.zeros_like(acc_ref)

        def accumulate(t, carry):
            rs = pl.ds(pl.multiple_of(t * mxu_rows, mxu_rows), mxu_rows)
            acc_ref[rs, :] += lax.dot_general(
                g_ref[rs, :], w_ref[...], (((1,), (1,)), ((), ())), preferred_element_type=F32
            )
            return carry

        lax.fori_loop(0, tm // mxu_rows, accumulate, 0, unroll=True)

        def stores():
            return (
                pltpu.make_async_copy(dx_buf, dx_hbm.at[block_rows, :], out_sems.at[0]),
                pltpu.make_async_copy(dxb_buf, dxb_hbm.at[block_rows, :], out_sems.at[1]),
            )

        @pl.when(kk == nk - 1)
        def _():
            pltpu.make_async_copy(x_hbm.at[block_rows, :], x_buf, in_sems.at[0]).wait()
            pltpu.make_async_copy(dres_hbm.at[block_rows, :], dres_buf, in_sems.at[1]).wait()

            @pl.when(i > 0)
            def _():
                for cp in stores():
                    cp.wait()

            def step(j, carry):
                rs = pl.ds(pl.multiple_of(j * rows, rows), rows)
                dx, dgain = _rmsnorm_bwd(acc_ref[rs, :], x_buf[rs, :], gain_ref[...], dres_buf[rs, :])
                dx_buf[rs, :] = dx
                dxb_buf[rs, :] = dx.astype(BF16)
                dg_ref[...] += dgain
                return carry

            lax.fori_loop(0, tm // rows, step, 0)
            for cp in stores():
                cp.start()

            @pl.when(i == n_blocks - 1)
            def _():
                for cp in stores():
                    cp.wait()

    vec = pl.BlockSpec((1, d), lambda i, kk: (0, 0))
    return pl.pallas_call(
        _behind(body, deps),
        name=name,
        grid=(n_blocks, nk),
        in_specs=[ANY_SPEC] * len(deps) + [
            pl.BlockSpec((tm, tk), lambda i, kk: (i, kk)),
            pl.BlockSpec((None, d, tk), lambda i, kk: (kk // per_shard, 0, kk % per_shard)),
            vec,
            ANY_SPEC,
            ANY_SPEC,
        ],
        out_specs=[ANY_SPEC, ANY_SPEC, vec],
        out_shape=[_shape((s, d), F32), _shape((s, d), BF16), _shape((1, d), F32)],
        scratch_shapes=[
            pltpu.VMEM((tm, d), F32),
            pltpu.VMEM((tm, d), F32),
            pltpu.VMEM((tm, d), F32),
            pltpu.VMEM((tm, d), F32),
            pltpu.VMEM((tm, d), BF16),
            pltpu.SemaphoreType.DMA((2,)),
            pltpu.SemaphoreType.DMA((2,)),
        ],
        compiler_params=_compute_params("arbitrary", "arbitrary"),
    )(*deps, g, w, gain, x, dres)


def _loss_and_grad(x, gain, target, *, name):
    s, d = x.shape
    tm = _tile(s, 256, SUBLANES_BF16)

    def body(x_ref, gain_ref, t_ref, loss_ref, dx_ref, dxb_ref, dg_ref):
        @pl.when(pl.program_id(0) == 0)
        def _():
            loss_ref[...] = jnp.zeros_like(loss_ref)
            dg_ref[...] = jnp.zeros_like(dg_ref)

        xf = x_ref[...]
        r = lax.rsqrt(jnp.mean(xf * xf, axis=-1, keepdims=True) + EPS)
        err = (xf * r) * gain_ref[...] - t_ref[...]
        loss_ref[...] += 0.5 * jnp.sum(jnp.mean(err * err, axis=-1, keepdims=True))
        dx, dgain = _rmsnorm_bwd(err * (1.0 / d), xf, gain_ref[...], None)
        dx_ref[...] = dx
        dxb_ref[...] = dx.astype(BF16)
        dg_ref[...] += dgain

    row_tile = pl.BlockSpec((tm, d), lambda i: (i, 0))
    vec = pl.BlockSpec((1, d), lambda i: (0, 0))
    return pl.pallas_call(
        body,
        name=name,
        grid=(s // tm,),
        in_specs=[row_tile, vec, row_tile],
        out_specs=[pl.BlockSpec((1, LANES), lambda i: (0, 0)), row_tile, row_tile, vec],
        out_shape=[_shape((1, LANES), F32), _shape((s, d), F32), _shape((s, d), BF16), _shape((1, d), F32)],
        compiler_params=_compute_params("arbitrary"),
    )(x, gain, target)


def _trailing_sums(v_ext, window):
    acc, span = v_ext, 1
    while span < window:
        acc = acc + pltpu.roll(acc, span, 0)
        span *= 2
    return acc


def _leading_sums(q_ext, window):
    n = q_ext.shape[0]
    acc, span = q_ext, 1
    while span < window:
        acc = acc + pltpu.roll(acc, n - span, 0)
        span *= 2
    return acc


def _inverse_counts(first_token, rows, window):
    t = first_token + lax.broadcasted_iota(jnp.int32, (rows, 1), 0)
    return 1.0 / jnp.minimum(t + 1, window).astype(F32)


def _mixer_fwd(proj, conv_w, pool_w, pool_scale, *, name):
    s, _ = proj.shape
    dc = conv_w.shape[1]
    n_groups, cg, _ = pool_w.shape
    dp = n_groups * cg
    assert dc == dp and all(w & (w - 1) == 0 and w <= HALO for w in POOL_WINDOWS)
    ts = _tile(s, 256, HALO)
    halo_blocks = ts // HALO

    def body(b_ref, c_ref, xt_ref, v_ref, ch_ref, xth_ref, vh_ref, cw_ref, pw_ref, ps_ref, y_ref):
        i = pl.program_id(0)
        has_past = i > 0
        for cc in range(dc // cg):
            cols = slice(cc * cg, (cc + 1) * cg)
            u_ext = jnp.concatenate(
                [
                    jnp.where(has_past, ch_ref[:, cols].astype(F32) * xth_ref[:, cols].astype(F32), 0.0),
                    c_ref[:, cols].astype(F32) * xt_ref[:, cols].astype(F32),
                ],
                axis=0,
            )
            conv = (
                cw_ref[2:3, cols] * u_ext[HALO:]
                + cw_ref[1:2, cols] * pltpu.roll(u_ext, 1, 0)[HALO:]
                + cw_ref[0:1, cols] * pltpu.roll(u_ext, 2, 0)[HALO:]
            )
            y_ref[:, cols] = (b_ref[:, cols].astype(F32) * conv).astype(BF16)
        for gi, window in enumerate(POOL_WINDOWS):
            cols = slice(gi * cg, (gi + 1) * cg)
            v_ext = jnp.concatenate(
                [jnp.where(has_past, vh_ref[:, cols].astype(F32), 0.0), v_ref[:, cols].astype(F32)], axis=0
            )
            mean = _trailing_sums(v_ext, window)[HALO:] * _inverse_counts(i * ts, ts, window)
            diff = (mean - v_ext[HALO:]).astype(BF16)
            z = jnp.dot(diff, pw_ref[gi], preferred_element_type=F32)
            y_ref[:, dc + gi * cg : dc + (gi + 1) * cg] = (z * ps_ref[:, cols]).astype(BF16)

    def col(jc):
        return pl.BlockSpec((ts, dc), lambda i: (i, jc))

    def past(jc):
        return pl.BlockSpec((HALO, dc), lambda i: (jnp.maximum(i * halo_blocks - 1, 0), jc))

    return pl.pallas_call(
        body,
        name=name,
        grid=(s // ts,),
        in_specs=[
            col(0), col(1), col(2), col(3), past(1), past(2), past(3),
            pl.BlockSpec((8, dc), lambda i: (0, 0)),
            pl.BlockSpec((n_groups, cg, cg), lambda i: (0, 0, 0)),
            pl.BlockSpec((1, dp), lambda i: (0, 0)),
        ],
        out_specs=pl.BlockSpec((ts, dc + dp), lambda i: (i, 0)),
        out_shape=_shape((s, dc + dp), BF16),
        compiler_params=_compute_params("parallel"),
    )(proj, proj, proj, proj, proj, proj, proj, conv_w, pool_w, pool_scale)


def _mixer_bwd(dy, proj, conv_w, pool_w, pool_scale, *, name, deps=()):
    s, e = proj.shape
    dc = conv_w.shape[1]
    n_groups, cg, _ = pool_w.shape
    dp = n_groups * cg
    ts = _tile(s, 256, HALO)
    halo_blocks = ts // HALO
    n_tiles = s // ts
    n_halo_blocks = s // HALO
    n_ext = ts + HALO

    def body(dyc_ref, dyp_ref, dycn_ref, dypn_ref, b_ref, c_ref, xt_ref, v_ref, bn_ref, ch_ref, xth_ref, vh_ref,
             cw_ref, pw_ref, ps_ref, dproj_ref, dcw_ref, dpw_ref, dps_ref):
        i = pl.program_id(0)
        has_past = i > 0
        has_next = i < n_tiles - 1

        @pl.when(i == 0)
        def _():
            dcw_ref[...] = jnp.zeros_like(dcw_ref)
            dpw_ref[...] = jnp.zeros_like(dpw_ref)
            dps_ref[...] = jnp.zeros_like(dps_ref)

        for cc in range(dc // cg):
            cols = slice(cc * cg, (cc + 1) * cg)
            c_now, xt_now = c_ref[:, cols].astype(F32), xt_ref[:, cols].astype(F32)
            u_ext = jnp.concatenate(
                [jnp.where(has_past, ch_ref[:, cols].astype(F32) * xth_ref[:, cols].astype(F32), 0.0), c_now * xt_now],
                axis=0,
            )
            u0 = u_ext[HALO:]
            u1 = pltpu.roll(u_ext, 1, 0)[HALO:]
            u2 = pltpu.roll(u_ext, 2, 0)[HALO:]
            dyc = dyc_ref[:, cols]
            conv = cw_ref[2:3, cols] * u0 + cw_ref[1:2, cols] * u1 + cw_ref[0:1, cols] * u2
            dproj_ref[:, cols] = (dyc * conv).astype(BF16)
            dconv = dyc * b_ref[:, cols].astype(F32)
            dconv_ext = jnp.concatenate(
                [dconv, jnp.where(has_next, dycn_ref[:, cols] * bn_ref[:, cols].astype(F32), 0.0)], axis=0
            )
            du = (
                cw_ref[2:3, cols] * dconv
                + cw_ref[1:2, cols] * pltpu.roll(dconv_ext, n_ext - 1, 0)[:ts]
                + cw_ref[0:1, cols] * pltpu.roll(dconv_ext, n_ext - 2, 0)[:ts]
            )
            dproj_ref[:, dc + cc * cg : dc + (cc + 1) * cg] = (du * xt_now).astype(BF16)
            dproj_ref[:, 2 * dc + cc * cg : 2 * dc + (cc + 1) * cg] = (du * c_now).astype(BF16)
            dcw_ref[0:1, cols] += jnp.sum(dconv * u2, axis=0, keepdims=True)
            dcw_ref[1:2, cols] += jnp.sum(dconv * u1, axis=0, keepdims=True)
            dcw_ref[2:3, cols] += jnp.sum(dconv * u0, axis=0, keepdims=True)

        for gi, window in enumerate(POOL_WINDOWS):
            cols = slice(gi * cg, (gi + 1) * cg)
            v_ext = jnp.concatenate(
                [jnp.where(has_past, vh_ref[:, cols].astype(F32), 0.0), v_ref[:, cols].astype(F32)], axis=0
            )
            mean = _trailing_sums(v_ext, window)[HALO:] * _inverse_counts(i * ts, ts, window)
            diff = (mean - v_ext[HALO:]).astype(BF16)
            z = jnp.dot(diff, pw_ref[gi], preferred_element_type=F32)
            dyp = dyp_ref[:, cols]
            dps_ref[:, cols] += jnp.sum(dyp * z, axis=0, keepdims=True)
            scale = ps_ref[:, cols]
            dz_ext = jnp.concatenate([dyp * scale, jnp.where(has_next, dypn_ref[:, cols] * scale, 0.0)], axis=0)
            dz_ext = dz_ext.astype(BF16)
            dpw_ref[gi] += lax.dot_general(
                diff, dz_ext[:ts], (((0,), (0,)), ((), ())), preferred_element_type=F32
            )
            ddiff_ext = lax.dot_general(
                dz_ext, pw_ref[gi], (((1,), (1,)), ((), ())), preferred_element_type=F32
            )
            q_ext = ddiff_ext * _inverse_counts(i * ts, n_ext, window)
            dv = _leading_sums(q_ext, window)[:ts] - ddiff_ext[:ts]
            dproj_ref[:, 3 * dc + gi * cg : 3 * dc + (gi + 1) * cg] = dv.astype(BF16)

    def col(jc):
        return pl.BlockSpec((ts, dc), lambda i: (i, jc))

    def past(jc):
        return pl.BlockSpec((HALO, dc), lambda i: (jnp.maximum(i * halo_blocks - 1, 0), jc))

    def following(jc):
        return pl.BlockSpec((HALO, dc), lambda i: (jnp.minimum((i + 1) * halo_blocks, n_halo_blocks - 1), jc))

    return pl.pallas_call(
        _behind(body, deps),
        name=name,
        grid=(n_tiles,),
        in_specs=[ANY_SPEC] * len(deps) + [
            col(0), col(1), following(0), following(1),
            col(0), col(1), col(2), col(3), following(0), past(1), past(2), past(3),
            pl.BlockSpec((8, dc), lambda i: (0, 0)),
            pl.BlockSpec((n_groups, cg, cg), lambda i: (0, 0, 0)),
            pl.BlockSpec((1, dp), lambda i: (0, 0)),
        ],
        out_specs=[
            pl.BlockSpec((ts, e), lambda i: (i, 0)),
            pl.BlockSpec((8, dc), lambda i: (0, 0)),
            pl.BlockSpec((n_groups, cg, cg), lambda i: (0, 0, 0)),
            pl.BlockSpec((1, dp), lambda i: (0, 0)),
        ],
        out_shape=[_shape((s, e), BF16), _shape((8, dc), F32), _shape((n_groups, cg, cg), F32), _shape((1, dp), F32)],
        compiler_params=_compute_params("arbitrary"),
    )(*deps, dy, dy, dy, dy, proj, proj, proj, proj, proj, proj, proj, proj, conv_w, pool_w, pool_scale)


def _cast_bf16(a, layer, *, name):
    _, rows, cols = a.shape
    tr = _tile(rows, 512, SUBLANES_BF16)

    def body(a_ref, o_ref):
        o_ref[...] = a_ref[...].astype(BF16)

    return pl.pallas_call(
        body, name=name, grid=(rows // tr,),
        in_specs=[pl.BlockSpec((None, tr, cols), lambda i: (layer, i, 0))],
        out_specs=pl.BlockSpec((None, tr, cols), lambda i: (0, i, 0)),
        out_shape=_shape((1, rows, cols), BF16),
        compiler_params=_compute_params("parallel"),
    )(a)


def _core_index():
    return lax.axis_index("c").astype(jnp.int32).reshape((1,))


def _add_sibling_half(grads, received, *, name, deps=()):
    nsh, r, c = grads.shape
    hr = r // 2
    tr = _tile(hr, 1024, SUBLANES_BF16)
    tiles = hr // tr

    def body(core_ref, *refs):
        g_ref, r_ref, o_ref = refs[len(deps):]
        o_ref[...] = (g_ref[...].astype(F32) + r_ref[...].astype(F32)).astype(BF16)

    half = pl.BlockSpec((None, tr, c), lambda sh, t, core: (sh, t, 0))
    return pl.pallas_call(
        body,
        name=name,
        grid_spec=pltpu.PrefetchScalarGridSpec(
            num_scalar_prefetch=1,
            grid=(nsh, tiles),
            in_specs=[ANY_SPEC] * len(deps)
            + [pl.BlockSpec((None, tr, c), lambda sh, t, core: (sh, core[0] * tiles + t, 0)), half],
            out_specs=half,
        ),
        out_shape=_shape((nsh, hr, c), BF16),
        compiler_params=_compute_params("parallel", "parallel"),
    )(_core_index(), *deps, grads, received)


def _chip_index_operand():
    return _chip_index(lax.axis_index("x"), lax.axis_index("y")).astype(jnp.int32).reshape((1,))


def _sum_chip_partials(own, received, *, name, deps=()):
    nsh, hr, c = received.shape
    tr = _tile(hr, 512, SUBLANES_BF16)

    def body(chip_ref, *refs):
        own_ref, p_ref, o_ref = refs[len(deps):]
        mine = chip_ref[0]
        total = None
        for sh in range(nsh):
            term = jnp.where(mine == sh, own_ref[...], p_ref[sh]).astype(F32)
            total = term if total is None else total + term
        o_ref[...] = total

    return pl.pallas_call(
        body,
        name=name,
        grid_spec=pltpu.PrefetchScalarGridSpec(
            num_scalar_prefetch=1,
            grid=(hr // tr,),
            in_specs=[ANY_SPEC] * len(deps) + [
                pl.BlockSpec((None, tr, c), lambda t, chip: (chip[0], t, 0)),
                pl.BlockSpec((nsh, tr, c), lambda t, chip: (0, t, 0)),
            ],
            out_specs=pl.BlockSpec((tr, c), lambda t, chip: (t, 0)),
        ),
        out_shape=_shape((hr, c), F32),
        compiler_params=_compute_params("parallel"),
    )(_chip_index_operand(), *deps, own, received)


def _adamw(grad, w, m, v, layer, carried, *, name, deps=()):
    n_layers, r, c = w.shape
    in_halves = isinstance(grad, (tuple, list))
    tr = _tile(r // 2 if in_halves else r, 256, 8)
    half_tiles = (r // 2) // tr if in_halves else 0
    bias1 = 1.0 - ADAM_B1 ** ADAM_STEP
    bias2 = 1.0 - ADAM_B2 ** ADAM_STEP
    n_grads = 2 if in_halves else 1

    def body(core_ref, *refs):
        refs = refs[len(deps):]
        w_ref, m_ref, v_ref = refs[n_grads : n_grads + 3]
        go_ref, d_ref, mo_ref, vo_ref, done_ref = refs[-5:]
        done_ref[...] = jnp.zeros_like(done_ref)
        if in_halves:
            in_my_half = pl.program_id(0) // half_tiles == core_ref[0]
            g = jnp.where(in_my_half, refs[0][...], refs[1][...])
        else:
            g = refs[0][...]
        m_new = ADAM_B1 * m_ref[...] + (1.0 - ADAM_B1) * g
        v_new = ADAM_B2 * v_ref[...] + (1.0 - ADAM_B2) * (g * g)
        go_ref[...] = g
        mo_ref[...] = m_new
        vo_ref[...] = v_new
        d_ref[...] = -ADAM_LR * ((m_new / bias1) / (jnp.sqrt(v_new / bias2) + ADAM_EPS) + ADAM_WD * w_ref[...])

    def half_spec(mine):
        def index(t, core):
            first = (core[0] if mine else 1 - core[0]) * half_tiles
            return (jnp.clip(t - first, 0, half_tiles - 1), 0)

        return pl.BlockSpec((tr, c), index)

    layer_tile = pl.BlockSpec((None, tr, c), lambda t, core: (layer, t, 0))
    if in_halves:
        grad_specs, grads = [half_spec(True), half_spec(False)], list(grad)
    else:
        grad_specs, grads = [pl.BlockSpec((tr, c), lambda t, core: (t, 0))], [grad]
    in_specs = [ANY_SPEC] * len(deps) + grad_specs + [layer_tile] * 3
    args = list(deps) + grads + [w, m, v]
    aliases = {}
    if carried is not None:
        aliases = {1 + len(args) + n: n for n in range(4)}
        in_specs += [HBM_SPEC] * 4
        args += list(carried)
    *outs, done = pl.pallas_call(
        body,
        name=name,
        grid_spec=pltpu.PrefetchScalarGridSpec(
            num_scalar_prefetch=1,
            grid=(r // tr,),
            in_specs=in_specs,
            out_specs=[layer_tile] * 4 + [pl.BlockSpec((8, LANES), lambda t, core: (0, 0))],
        ),
        out_shape=[_shape((n_layers, r, c), F32)] * 4 + [_shape((8, LANES), F32)],
        input_output_aliases=aliases,
        compiler_params=_compute_params("arbitrary"),
    )(_core_index(), *args)
    return outs, done


def _place():
    x, y, c = (lax.axis_index(a) for a in MESH_AXES)
    other_chips = [(1 - x, y), (x, 1 - y), (1 - x, 1 - y)]
    return x, y, c, other_chips


def _chip_index(x, y):
    return 2 * x + y


def _gather_small(conv_rows, pool_rows, *, name):
    blocks = (conv_rows, pool_rows)
    n = len(blocks)

    def body(*refs):
        ins, outs = refs[:n], refs[n : 2 * n]
        send_sems, recv_sems = refs[2 * n :]
        x, y, c, other_chips = _place()
        mine = _chip_index(x, y)
        peers = [(x, y, 1 - c)] + [(px, py, c) for px, py in other_chips]
        _handshake(peers)
        sends = []
        for a in range(n):
            for j, peer in enumerate(peers):
                cp = pltpu.make_async_remote_copy(
                    src_ref=ins[a], dst_ref=outs[a].at[mine], send_sem=send_sems.at[a, j], recv_sem=recv_sems.at[a, j],
                    device_id=peer, device_id_type=MESH_ID,
                )
                cp.start()
                sends.append(cp)
        for a in range(n):
            for j, (px, py, _) in enumerate(peers):
                pltpu.make_async_remote_copy(
                    src_ref=ins[a], dst_ref=outs[a].at[_chip_index(px, py)], send_sem=send_sems.at[a, j],
                    recv_sem=recv_sems.at[a, j], device_id=peers[j], device_id_type=MESH_ID,
                ).wait_recv()
        for cp in sends:
            cp.wait_send()

    return _sequencer_call(
        body,
        [_shape((N_CHIPS,) + blk.shape, F32) for blk in blocks],
        [pltpu.SemaphoreType.DMA((n, 4)), pltpu.SemaphoreType.DMA((n, 4))],
        GATHER_SMALL_COLLECTIVE_ID,
        name,
    )(*blocks)


def _allreduce_small(vec, *, name):
    rows, n = vec.shape
    n_dev = 8

    def body(v_ref, o_ref, slots, send_sems, recv_sems):
        x, y, c, _ = _place()
        me = 4 * x + 2 * y + c
        slots[me] = v_ref[...]
        sends = []
        for mask in range(1, n_dev):
            fx, fy, fc = (mask >> 2) & 1, (mask >> 1) & 1, mask & 1
            peer = (x ^ fx, y ^ fy, c ^ fc)
            cp = pltpu.make_async_remote_copy(
                src_ref=v_ref, dst_ref=slots.at[me], send_sem=send_sems.at[mask - 1], recv_sem=recv_sems.at[mask - 1],
                device_id=peer, device_id_type=MESH_ID,
            )
            cp.start()
            sends.append(cp)
        for mask in range(1, n_dev):
            fx, fy, fc = (mask >> 2) & 1, (mask >> 1) & 1, mask & 1
            peer = (x ^ fx, y ^ fy, c ^ fc)
            pltpu.make_async_remote_copy(
                src_ref=v_ref, dst_ref=slots.at[4 * peer[0] + 2 * peer[1] + peer[2]], send_sem=send_sems.at[mask - 1],
                recv_sem=recv_sems.at[mask - 1], device_id=peer, device_id_type=MESH_ID,
            ).wait_recv()
        total = slots[0]
        for dev in range(1, n_dev):
            total = total + slots[dev]
        o_ref[...] = total
        for cp in sends:
            cp.wait_send()

    return pl.pallas_call(
        body,
        name=name,
        in_specs=[VMEM_SPEC],
        out_specs=VMEM_SPEC,
        out_shape=_shape((rows, n), F32),
        scratch_shapes=[
            pltpu.VMEM((n_dev, rows, n), F32),
            pltpu.SemaphoreType.DMA((n_dev - 1,)),
            pltpu.SemaphoreType.DMA((n_dev - 1,)),
        ],
    )(vec)


def _handshake(peers):
    barrier = pltpu.get_barrier_semaphore()
    for peer in peers:
        pl.semaphore_signal(barrier, inc=1, device_id=peer, device_id_type=MESH_ID)
    pl.semaphore_wait(barrier, len(peers))


def _sequencer_call(body, out_type, scratch_types, collective_id, name):
    return pl.kernel(
        body,
        name=name,
        out_type=out_type,
        mesh=plsc.ScalarSubcoreMesh(axis_name="sequencer", num_cores=1),
        scratch_types=scratch_types,
        compiler_params=pltpu.CompilerParams(collective_id=collective_id),
    )


GATHER_COLLECTIVE_ID = 1
EXCHANGE_COLLECTIVE_ID = 2
SCATTER_COLLECTIVE_ID = 3
SHARE_COLLECTIVE_ID = 4
GATHER_SMALL_COLLECTIVE_ID = 5


def _gather_weights(stacked, layer, *, name):
    n = len(stacked)

    def body(*refs):
        ins, outs = refs[:n], refs[n : 2 * n]
        own_sems, send_sems, recv_sems, pass_send_sems, pass_recv_sems = refs[2 * n :]
        x, y, c, other_chips = _place()
        mine = _chip_index(x, y)
        sibling = (x, y, 1 - c)
        _handshake([sibling] + [(px, py, c) for px, py in other_chips])
        pending = []

        def send_my_half(a):
            hr = ins[a].shape[1] // 2
            my_half = pl.ds(c * hr, hr)
            cp = pltpu.make_async_remote_copy(
                src_ref=ins[a].at[layer], dst_ref=outs[a].at[mine], send_sem=own_sems.at[0, a],
                recv_sem=own_sems.at[1, a], device_id=sibling, device_id_type=MESH_ID,
            )
            cp.start()
            pending.append(cp.wait)
            for j, (px, py) in enumerate(other_chips):
                cp = pltpu.make_async_remote_copy(
                    src_ref=ins[a].at[layer, my_half], dst_ref=outs[a].at[mine, my_half], send_sem=send_sems.at[a, j],
                    recv_sem=recv_sems.at[a, j], device_id=(px, py, c), device_id_type=MESH_ID,
                )
                cp.start()
                pending.append(cp.wait_send)

        for a in range(min(2, n)):
            send_my_half(a)
        for a in range(n):
            hr = ins[a].shape[1] // 2
            my_half = pl.ds(c * hr, hr)
            for j, (px, py) in enumerate(other_chips):
                landed = outs[a].at[_chip_index(px, py), my_half]
                pltpu.make_async_remote_copy(
                    src_ref=landed, dst_ref=landed, send_sem=send_sems.at[a, j], recv_sem=recv_sems.at[a, j],
                    device_id=(px, py, c), device_id_type=MESH_ID,
                ).wait_recv()
                cp = pltpu.make_async_remote_copy(
                    src_ref=landed, dst_ref=landed, send_sem=pass_send_sems.at[a, j], recv_sem=pass_recv_sems.at[a, j],
                    device_id=sibling, device_id_type=MESH_ID,
                )
                cp.start()
                pending.append(cp.wait_send)
            if a + 2 < n:
                send_my_half(a + 2)
        for a in range(n):
            hr = ins[a].shape[1] // 2
            sibling_half = pl.ds((1 - c) * hr, hr)
            for j, (px, py) in enumerate(other_chips):
                passed = outs[a].at[_chip_index(px, py), sibling_half]
                pltpu.make_async_remote_copy(
                    src_ref=passed, dst_ref=passed, send_sem=pass_send_sems.at[a, j], recv_sem=pass_recv_sems.at[a, j],
                    device_id=sibling, device_id_type=MESH_ID,
                ).wait_recv()
        for wait in pending:
            wait()

    return _sequencer_call(
        body,
        [_shape((N_CHIPS,) + a.shape[1:], BF16) for a in stacked],
        [pltpu.SemaphoreType.DMA((2, n))] + [pltpu.SemaphoreType.DMA((n, 3))] * 4,
        GATHER_COLLECTIVE_ID,
        name,
    )(*stacked)


def _exchange_halves(grads, *, name):
    n = len(grads)

    def body(*refs):
        ins, outs = refs[:n], refs[n : 2 * n]
        send_sems, recv_sems = refs[2 * n :]
        x, y, c, _ = _place()
        sibling = (x, y, 1 - c)
        _handshake([sibling])
        copies = []
        for a in range(n):
            hr = ins[a].shape[1] // 2
            cp = pltpu.make_async_remote_copy(
                src_ref=ins[a].at[:, pl.ds((1 - c) * hr, hr), :], dst_ref=outs[a], send_sem=send_sems.at[a],
                recv_sem=recv_sems.at[a], device_id=sibling, device_id_type=MESH_ID,
            )
            cp.start()
            copies.append(cp)
        for cp in copies:
            cp.wait()

    return _sequencer_call(
        body,
        [_shape((g.shape[0], g.shape[1] // 2, g.shape[2]), BF16) for g in grads],
        [pltpu.SemaphoreType.DMA((n,)), pltpu.SemaphoreType.DMA((n,))],
        EXCHANGE_COLLECTIVE_ID,
        name,
    )(*grads)


def _scatter_partials(partials, *, name):
    n = len(partials)

    def body(*refs):
        ins, outs = refs[:n], refs[n : 2 * n]
        send_sems, recv_sems = refs[2 * n :]
        x, y, c, other_chips = _place()
        mine = _chip_index(x, y)
        _handshake([(px, py, c) for px, py in other_chips])
        pending = []
        for a in range(n):
            for j, (px, py) in enumerate(other_chips):
                cp = pltpu.make_async_remote_copy(
                    src_ref=ins[a].at[_chip_index(px, py)], dst_ref=outs[a].at[mine], send_sem=send_sems.at[a, j],
                    recv_sem=recv_sems.at[a, j], device_id=(px, py, c), device_id_type=MESH_ID,
                )
                cp.start()
                pending.append(cp.wait_send)
        for a in range(n):
            for j, (px, py) in enumerate(other_chips):
                landed = outs[a].at[_chip_index(px, py)]
                pltpu.make_async_remote_copy(
                    src_ref=landed, dst_ref=landed, send_sem=send_sems.at[a, j], recv_sem=recv_sems.at[a, j],
                    device_id=(px, py, c), device_id_type=MESH_ID,
                ).wait_recv()
        for wait in pending:
            wait()

    return _sequencer_call(
        body,
        [_shape(p.shape, BF16) for p in partials],
        [pltpu.SemaphoreType.DMA((n, 3)), pltpu.SemaphoreType.DMA((n, 3))],
        SCATTER_COLLECTIVE_ID,
        name,
    )(*partials)


def _share_halves(halves, *, name):
    n = len(halves)

    def body(*refs):
        ins, outs = refs[:n], refs[n : 2 * n]
        send_sems, recv_sems = refs[2 * n :]
        x, y, c, _ = _place()
        sibling = (x, y, 1 - c)
        _handshake([sibling])
        copies = []
        for a in range(n):
            cp = pltpu.make_async_remote_copy(
                src_ref=ins[a], dst_ref=outs[a], send_sem=send_sems.at[a], recv_sem=recv_sems.at[a],
                device_id=sibling, device_id_type=MESH_ID,
            )
            cp.start()
            copies.append(cp)
        for cp in copies:
            cp.wait()

    return _sequencer_call(
        body,
        [_shape(h.shape, F32) for h in halves],
        [pltpu.SemaphoreType.DMA((n,)), pltpu.SemaphoreType.DMA((n,))],
        SHARE_COLLECTIVE_ID,
        name,
    )(*halves)


class _ReduceToOwner:
    def __init__(self, grads, tag):
        self.grads, self.tag = grads, tag
        self.received = _exchange_halves(grads, name=f"exchange_halves_{tag}")

    def add_sibling(self, after):
        partials = [
            _add_sibling_half(g, r, name=f"add_sibling_{self.tag}_{a}", deps=after)
            for a, (g, r) in enumerate(zip(self.grads, self.received))
        ]
        self.partials = partials
        self.slots = _scatter_partials(partials, name=f"scatter_partials_{self.tag}")
        return partials

    def sum_chips(self, after):
        halves = [
            _sum_chip_partials(p, s, name=f"sum_partials_{self.tag}_{a}", deps=after)
            for a, (p, s) in enumerate(zip(self.partials, self.slots))
        ]
        self.halves = halves
        self.theirs = _share_halves(halves, name=f"share_halves_{self.tag}")
        return halves

    def totals(self):
        return list(zip(self.halves, self.theirs))


def kernel(x, w_in, conv_w, pool_w, pool_scale, w_out, norm_mix, norm_mlp, w_up, w_down, norm_final, loss_target, m_w_in, m_conv_w, m_pool_w, m_pool_scale, m_w_out, m_norm_mix, m_norm_mlp, m_w_up, m_w_down, m_norm_final, v_w_in, v_conv_w, v_pool_w, v_pool_scale, v_w_out, v_norm_mix, v_norm_mlp, v_w_up, v_w_down, v_norm_final):
    n_layers, d, _ = w_in.shape
    s = x.shape[1]
    dc = conv_w.shape[2] * N_CHIPS
    n_groups, cg_rows, cg = pool_w.shape[1:]
    dp = n_groups * cg
    x0 = x.reshape(s, d)
    target = loss_target.reshape(s, d)

    big = [w_in, w_out, w_up, w_down]

    conv_rows = jnp.pad(conv_w, ((0, 0), (0, 8 - CONV_TAPS), (0, 0))).reshape(n_layers * 8, -1)
    pool_rows = pool_w.reshape(n_layers * n_groups * cg_rows, cg)
    conv_all, pool_all = _gather_small(conv_rows, pool_rows, name="gather_small")
    conv_full = conv_all.reshape(N_CHIPS, n_layers, 8, -1).transpose(1, 2, 0, 3).reshape(n_layers, 8, dc)
    pool_full = (
        pool_all.reshape(N_CHIPS, n_layers, n_groups, cg_rows, cg).transpose(1, 2, 0, 3, 4)
        .reshape(n_layers, n_groups, cg, cg).astype(BF16)
    )

    saved = []
    xl = x0
    for l in range(n_layers):
        win_g, wout_g, wup_g, wdown_g = [
            _gather_weights([_cast_bf16(w, l, name=f"cast_l{l}_{t}")], 0, name=f"gather_weights_l{l}_{t}")[0]
            for t, w in enumerate(big)
        ]
        gain_mix = norm_mix[l].reshape(1, d)
        gain_mlp = norm_mlp[l].reshape(1, d)
        scale = pool_scale[l].reshape(1, dp)
        h1 = _rmsnorm(xl, gain_mix, name=f"norm_mix_l{l}")
        proj = _matmul_cols(h1, win_g, relu2=False, name=f"in_proj_l{l}")
        y = _mixer_fwd(proj, conv_full[l], pool_full[l], scale, name=f"mixer_fwd_l{l}")
        x_mid = _matmul_residual(y, wout_g.reshape(-1, d), xl, name=f"out_proj_l{l}")
        h2 = _rmsnorm(x_mid, gain_mlp, name=f"norm_mlp_l{l}")
        u, u2 = _matmul_cols(h2, wup_g, relu2=True, name=f"up_proj_l{l}")
        x_next = _matmul_residual(u2, wdown_g.reshape(-1, d), x_mid, name=f"down_proj_l{l}")
        saved.append((xl, h1, proj, y, x_mid, h2, u, u2, win_g, wout_g, wup_g, wdown_g, gain_mix, gain_mlp, scale))
        xl = x_next

    loss_part, dx, dx_bf16, d_norm_final = _loss_and_grad(xl, norm_final.reshape(1, d), target, name="loss_head")
    loss = lax.psum(loss_part[0, 0], MESH_AXES)

    small_grads = [None] * n_layers
    carried = [None] * 5
    pool_params = tuple(p.reshape(n_layers, n_groups * cg_rows, cg) for p in (pool_w, m_pool_w, v_pool_w))
    params = [(w_in, m_w_in, v_w_in), (w_out, m_w_out, v_w_out), (w_up, m_w_up, v_w_up), (w_down, m_w_down, v_w_down),
              pool_params]
    DOWN, UP_OUT, IN_POOL = (3,), (2, 1), (0, 4)

    def update(reduce, which, layer, after):
        dones = []
        for a, total in zip(which, reduce.totals()):
            w, m, v = params[a]
            carried[a], done = _adamw(total, w, m, v, layer, carried[a], name=f"adamw_{a}_l{layer}", deps=after)
            dones.append(done)
        return dones

    up_out_above = in_pool_above = None
    for l in reversed(range(n_layers)):
        xl, h1, proj, y, x_mid, h2, u, u2, win_g, wout_g, wup_g, wdown_g, gain_mix, gain_mlp, scale = saved[l]
        above = up_out_above is not None
        g_down = _matmul_tn(u2, dx_bf16, n_shards=N_CHIPS, shard_cols=False, name=f"grad_w_down_l{l}")
        down = _ReduceToOwner([g_down], f"down_l{l}")
        deps = up_out_above.sum_chips([g_down]) if above else [g_down]
        da = _matmul_nt(dx_bf16, wdown_g.reshape(-1, d), u=u, name=f"grad_act_l{l}", deps=deps)
        g_up = _matmul_tn(
            h2, da, n_shards=N_CHIPS, shard_cols=True, name=f"grad_w_up_l{l}", deps=down.add_sibling([da])
        )
        deps = in_pool_above.sum_chips([g_up]) if above else [g_up]
        dx_mid, dx_mid_bf16, d_gain_mlp = _matmul_nt_norm_bwd(
            da, wup_g, x_mid, gain_mlp, dx, name=f"grad_mid_l{l}", deps=deps
        )
        g_out = _matmul_tn(
            y, dx_mid_bf16, n_shards=N_CHIPS, shard_cols=False, name=f"grad_w_out_l{l}",
            deps=down.sum_chips([dx_mid_bf16]),
        )
        up_out = _ReduceToOwner([g_up, g_out], f"up_out_l{l}")
        deps = update(up_out_above, UP_OUT, l + 1, [g_out]) if above else [g_out]
        dy = _matmul_nt(dx_mid_bf16, wout_g.reshape(-1, d), name=f"grad_mixed_l{l}", deps=deps)
        dproj, d_conv, d_pool, d_scale = _mixer_bwd(
            dy, proj, conv_full[l], pool_full[l], scale, name=f"mixer_bwd_l{l}", deps=up_out.add_sibling([dy])
        )
        g_in = _matmul_tn(h1, dproj, n_shards=N_CHIPS, shard_cols=True, name=f"grad_w_in_l{l}")
        g_pool = (
            d_pool.reshape(n_groups, N_CHIPS, cg_rows, cg).transpose(1, 0, 2, 3)
            .reshape(N_CHIPS, n_groups * cg_rows, cg).astype(BF16)
        )
        in_pool = _ReduceToOwner([g_in, g_pool], f"in_pool_l{l}")
        deps = update(down, DOWN, l, [g_in])
        if above:
            deps = update(in_pool_above, IN_POOL, l + 1, deps)
        dx, dx_bf16, d_gain_mix = _matmul_nt_norm_bwd(
            dproj, win_g, xl, gain_mix, dx_mid, name=f"grad_x_l{l}", deps=in_pool.add_sibling(deps)
        )
        small_grads[l] = jnp.concatenate(
            [d_conv[:CONV_TAPS].reshape(-1), d_scale.reshape(-1), d_gain_mix.reshape(-1), d_gain_mlp.reshape(-1)]
        )
        up_out_above, in_pool_above = up_out, in_pool
    deps = up_out_above.sum_chips([dx_bf16])
    deps = in_pool_above.sum_chips(deps)
    deps = update(up_out_above, UP_OUT, 0, deps)
    update(in_pool_above, IN_POOL, 0, deps)

    vec = jnp.concatenate(small_grads + [d_norm_final.reshape(-1)])
    vec = _allreduce_small(vec.reshape(8, -1), name="allreduce_small").reshape(-1)
    per_layer = vec[: n_layers * (CONV_TAPS * dc + dp + 2 * d)].reshape(n_layers, -1)
    chip = _chip_index(lax.axis_index("x"), lax.axis_index("y"))
    dcs = dc // N_CHIPS
    g_conv = lax.dynamic_slice_in_dim(per_layer[:, : CONV_TAPS * dc].reshape(n_layers, CONV_TAPS, dc), chip * dcs, dcs, axis=2)
    g_scale = per_layer[:, CONV_TAPS * dc : CONV_TAPS * dc + dp]
    g_mix = per_layer[:, CONV_TAPS * dc + dp : CONV_TAPS * dc + dp + d]
    g_mlp = per_layer[:, CONV_TAPS * dc + dp + d :]
    g_final = vec[n_layers * (CONV_TAPS * dc + dp + 2 * d) :]

    def small_adamw(g, w, m, v, tag):
        flat = lambda t: t.reshape(1, -1, t.shape[-1])
        out, _ = _adamw(flat(g)[0], flat(w), flat(m), flat(v), 0, None, name=f"adamw_{tag}")
        return [o.reshape(w.shape) for o in out]

    o_conv = small_adamw(g_conv, conv_w, m_conv_w, v_conv_w, "conv_w")
    o_scale = small_adamw(g_scale, pool_scale, m_pool_scale, v_pool_scale, "pool_scale")
    o_mix = small_adamw(g_mix, norm_mix, m_norm_mix, v_norm_mix, "norm_mix")
    o_mlp = small_adamw(g_mlp, norm_mlp, m_norm_mlp, v_norm_mlp, "norm_mlp")
    o_final = small_adamw(g_final, norm_final, m_norm_final, v_norm_final, "norm_final")
    o_in, o_out, o_up, o_down, o_pool = carried
    o_pool = [o.reshape(pool_w.shape) for o in o_pool]

    ordered = [o_in, o_conv, o_pool, o_scale, o_out, o_mix, o_mlp, o_up, o_down, o_final]
    return (loss, dx.reshape(x.shape), *[o[0] for o in ordered], *[o[1] for o in ordered], *[o[2] for o in ordered],
            *[o[3] for o in ordered])
```

```python
import functools

import jax
import jax.numpy as jnp
from jax import lax
from jax.experimental import pallas as pl
from jax.experimental.pallas import tpu as pltpu
from jax.experimental.pallas import tpu_sc as plsc

F32 = jnp.float32
BF16 = jnp.bfloat16

EPS = 1e-6
POOL_WINDOWS = (2, 4, 8, 16)
CONV_TAPS = 3
HALO = 16

ADAM_LR = 0.001
ADAM_B1 = 0.9
ADAM_B2 = 0.999
ADAM_EPS = 1e-08
ADAM_WD = 0.01
ADAM_STEP = 10

N_CHIPS = 4
MESH_AXES = ("x", "y", "c")
V7X_VMEM_LIMIT_BYTES = 56 * 1024 * 1024
SUBLANES_BF16 = 16
LANES = 128

HBM_SPEC = pl.BlockSpec(memory_space=pltpu.HBM)
VMEM_SPEC = pl.BlockSpec(memory_space=pltpu.VMEM)
MESH_ID = pl.DeviceIdType.MESH


def _tile(dim, target, align):
    if dim <= target:
        return dim
    t = (target // align) * align
    while dim % t:
        t -= align
    assert t > 0, (dim, target, align)
    return t


def _compute_params(*semantics):
    return pltpu.CompilerParams(dimension_semantics=semantics, vmem_limit_bytes=V7X_VMEM_LIMIT_BYTES)


def _shape(shape, dtype):
    return jax.ShapeDtypeStruct(shape, dtype)


ANY_SPEC = pl.BlockSpec(memory_space=pl.ANY)


def _behind(body, deps):
    return lambda *refs: body(*refs[len(deps):])


def _rmsnorm_bwd(dh, x, gain, dres):
    r = lax.rsqrt(jnp.mean(x * x, axis=-1, keepdims=True) + EPS)
    xn = x * r
    dgain = jnp.sum(dh * xn, axis=0, keepdims=True)
    dxn = dh * gain
    dx = r * (dxn - xn * jnp.mean(dxn * xn, axis=-1, keepdims=True))
    if dres is not None:
        dx = dx + dres
    return dx, dgain


MXU_ROWS = 512


def _rmsnorm(x, gain, *, name):
    s, d = x.shape
    tm = _tile(s, 512, SUBLANES_BF16)

    def body(x_ref, g_ref, h_ref):
        xf = x_ref[...]
        r = lax.rsqrt(jnp.mean(xf * xf, axis=-1, keepdims=True) + EPS)
        h_ref[...] = ((xf * r) * g_ref[...]).astype(BF16)

    row_tile = pl.BlockSpec((tm, d), lambda i: (i, 0))
    return pl.pallas_call(
        body, name=name, grid=(s // tm,), in_specs=[row_tile, pl.BlockSpec((1, d), lambda i: (0, 0))],
        out_specs=row_tile, out_shape=_shape((s, d), BF16), compiler_params=_compute_params("parallel"),
    )(x, gain)


def _matmul_cols(h, w, *, relu2, name):
    s, k = h.shape
    nsh, _, c = w.shape
    tm = _tile(s, 2048, MXU_ROWS)
    tn = _tile(c, 1024, LANES)
    rows = min(MXU_ROWS, tm)
    per_shard = c // tn

    def body(h_ref, w_ref, *out_refs):
        def step(t, carry):
            rs = pl.ds(pl.multiple_of(t * rows, rows), rows)
            acc = jnp.dot(h_ref[rs, :], w_ref[...], preferred_element_type=F32)
            if relu2:
                u = jnp.maximum(acc, 0.0)
                out_refs[0][rs, :] = u.astype(BF16)
                out_refs[1][rs, :] = (u * u).astype(BF16)
            else:
                out_refs[0][rs, :] = acc.astype(BF16)
            return carry

        lax.fori_loop(0, tm // rows, step, 0)

    tile_out = pl.BlockSpec((tm, tn), lambda i, j: (i, j))
    n = nsh * c
    outs = [_shape((s, n), BF16)] * (2 if relu2 else 1)
    result = pl.pallas_call(
        body,
        name=name,
        grid=(s // tm, n // tn),
        in_specs=[
            pl.BlockSpec((tm, k), lambda i, j: (i, 0)),
            pl.BlockSpec((None, k, tn), lambda i, j: (j // per_shard, 0, j % per_shard)),
        ],
        out_specs=[tile_out] * len(outs),
        out_shape=outs,
        compiler_params=_compute_params("parallel", "arbitrary"),
    )(h, w)
    return result if relu2 else result[0]


def _matmul_residual(a, w, res, *, name):
    s, k = a.shape
    _, n = w.shape
    tm = _tile(s, 1024, MXU_ROWS)
    tn = _tile(n, 1024, LANES)
    tk = _tile(k, 2048, LANES)
    rows = min(MXU_ROWS, tm)

    def body(a_ref, w_ref, r_ref, o_ref):
        kk = pl.program_id(2)

        @pl.when(kk == 0)
        def _():
            o_ref[...] = r_ref[...]

        def step(t, carry):
            rs = pl.ds(pl.multiple_of(t * rows, rows), rows)
            o_ref[rs, :] += jnp.dot(a_ref[rs, :], w_ref[...], preferred_element_type=F32)
            return carry

        lax.fori_loop(0, tm // rows, step, 0, unroll=True)

    return pl.pallas_call(
        body,
        name=name,
        grid=(s // tm, n // tn, k // tk),
        in_specs=[
            pl.BlockSpec((tm, tk), lambda i, j, kk: (i, kk)),
            pl.BlockSpec((tk, tn), lambda i, j, kk: (kk, j)),
            pl.BlockSpec((tm, tn), lambda i, j, kk: (i, j)),
        ],
        out_specs=pl.BlockSpec((tm, tn), lambda i, j, kk: (i, j)),
        out_shape=_shape((s, n), F32),
        compiler_params=_compute_params("parallel", "parallel", "arbitrary"),
    )(a, w, res)


def _matmul_tn(a, g, *, n_shards, shard_cols, name, deps=()):
    m, kd = a.shape
    _, n = g.shape
    r, c = (kd, n // n_shards) if shard_cols else (kd // n_shards, n)
    tr = _tile(kd, 2048, LANES)
    tn = _tile(c, 1024, LANES)
    tm = _tile(m, 2048, SUBLANES_BF16)
    nm = m // tm
    cols_per_shard = c // tn

    out_rows = min(MXU_ROWS, tr)

    def body(a_ref, g_ref, o_ref, acc_ref):
        mm = pl.program_id(2)

        @pl.when(mm == 0)
        def _():
            acc_ref[...] = jnp.zeros_like(acc_ref)

        for t in range(tr // out_rows):
            rs = slice(t * out_rows, (t + 1) * out_rows)
            acc_ref[rs, :] += lax.dot_general(
                a_ref[:, rs], g_ref[...], (((0,), (0,)), ((), ())), preferred_element_type=F32
            )

        @pl.when(mm == nm - 1)
        def _():
            o_ref[...] = acc_ref[...].astype(BF16)

    if shard_cols:
        out_spec = pl.BlockSpec((None, tr, tn), lambda i, j, mm: (j // cols_per_shard, i, j % cols_per_shard))
        out_shape = _shape((n_shards, kd, c), BF16)
    else:
        out_spec = pl.BlockSpec((tr, tn), lambda i, j, mm: (i, j))
        out_shape = _shape((kd, n), BF16)
    out = pl.pallas_call(
        _behind(body, deps),
        name=name,
        grid=(kd // tr, n // tn, nm),
        in_specs=[ANY_SPEC] * len(deps) + [
            pl.BlockSpec((tm, tr), lambda i, j, mm: (mm, i)),
            pl.BlockSpec((tm, tn), lambda i, j, mm: (mm, j)),
        ],
        out_specs=out_spec,
        out_shape=out_shape,
        scratch_shapes=[pltpu.VMEM((tr, tn), F32)],
        compiler_params=_compute_params("parallel", "parallel", "arbitrary"),
    )(*deps, a, g)
    return out.reshape(n_shards, r, c)


def _matmul_nt(g, w, *, u=None, name, deps=()):
    m, n = g.shape
    kd, _ = w.shape
    tm = _tile(m, 2048, MXU_ROWS)
    tj = _tile(kd, 1024, LANES)
    rows = min(MXU_ROWS, tm)

    def body(*refs):
        if u is None:
            g_ref, w_ref, o_ref = refs
        else:
            g_ref, w_ref, u_ref, o_ref = refs

        def step(t, carry):
            rs = pl.ds(pl.multiple_of(t * rows, rows), rows)
            prod = lax.dot_general(g_ref[rs, :], w_ref[...], (((1,), (1,)), ((), ())), preferred_element_type=F32)
            if u is None:
                o_ref[rs, :] = prod
            else:
                o_ref[rs, :] = (prod * (2.0 * u_ref[rs, :].astype(F32))).astype(BF16)
            return carry

        lax.fori_loop(0, tm // rows, step, 0)

    tile_out = pl.BlockSpec((tm, tj), lambda i, j: (i, j))
    in_specs = [pl.BlockSpec((tm, n), lambda i, j: (i, 0)), pl.BlockSpec((tj, n), lambda i, j: (j, 0))]
    args = [g, w]
    if u is not None:
        in_specs.append(tile_out)
        args.append(u)
    return pl.pallas_call(
        _behind(body, deps),
        name=name,
        grid=(m // tm, kd // tj),
        in_specs=[ANY_SPEC] * len(deps) + in_specs,
        out_specs=tile_out,
        out_shape=_shape((m, kd), F32 if u is None else BF16),
        compiler_params=_compute_params("parallel", "arbitrary"),
    )(*deps, *args)


EPILOGUE_ROWS = 128


def _matmul_nt_norm_bwd(g, w, x, gain, dres, *, name, deps=()):
    s, n = g.shape
    nsh, d, c = w.shape
    tm = _tile(s, 1024, MXU_ROWS)
    tk = _tile(c, 1024, LANES)
    per_shard = c // tk
    nk = n // tk
    n_blocks = s // tm
    mxu_rows = min(MXU_ROWS, tm)
    rows = min(EPILOGUE_ROWS, tm)
    piece = tm // nk
    assert piece * nk == tm and piece % 8 == 0

    def body(g_ref, w_ref, gain_ref, x_hbm, dres_hbm, dx_hbm, dxb_hbm, dg_ref, acc_ref, x_buf, dres_buf, dx_buf,
             dxb_buf, in_sems, out_sems):
        i = pl.program_id(0)
        kk = pl.program_id(1)

        @pl.when((kk == 0) & (i == 0))
        def _():
            dg_ref[...] = jnp.zeros_like(dg_ref)

        block_rows = pl.ds(pl.multiple_of(i * tm, tm), tm)
        pieces = pl.ds(pl.multiple_of(kk * piece, piece), piece)
        piece_rows = pl.ds(pl.multiple_of(i * tm + kk * piece, piece), piece)
        pltpu.make_async_copy(x_hbm.at[piece_rows, :], x_buf.at[pieces, :], in_sems.at[0]).start()
        pltpu.make_async_copy(dres_hbm.at[piece_rows, :], dres_buf.at[pieces, :], in_sems.at[1]).start()

        @pl.when(kk == 0)
        def _():
            acc_ref[...] = jnp.zeros_like(acc_ref)

        def accumulate(t, carry):
            rs = pl.ds(pl.multiple_of(t * mxu_rows, mxu_rows), mxu_rows)
            acc_ref[rs, :] += lax.dot_general(
                g_ref[rs, :], w_ref[...], (((1,), (1,)), ((), ())), preferred_element_type=F32
            )
            return carry

        lax.fori_loop(0, tm // mxu_rows, accumulate, 0, unroll=True)

        def stores():
            return (
                pltpu.make_async_copy(dx_buf, dx_hbm.at[block_rows, :], out_sems.at[0]),
                pltpu.make_async_copy(dxb_buf, dxb_hbm.at[block_rows, :], out_sems.at[1]),
            )

        @pl.when(kk == nk - 1)
        def _():
            pltpu.make_async_copy(x_hbm.at[block_rows, :], x_buf, in_sems.at[0]).wait()
            pltpu.make_async_copy(dres_hbm.at[block_rows, :], dres_buf, in_sems.at[1]).wait()

            @pl.when(i > 0)
            def _():
                for cp in stores():
                    cp.wait()

            def step(j, carry):
                rs = pl.ds(pl.multiple_of(j * rows, rows), rows)
                dx, dgain = _rmsnorm_bwd(acc_ref[rs, :], x_buf[rs, :], gain_ref[...], dres_buf[rs, :])
                dx_buf[rs, :] = dx
                dxb_buf[rs, :] = dx.astype(BF16)
                dg_ref[...] += dgain
                return carry

            lax.fori_loop(0, tm // rows, step, 0)
            for cp in stores():
                cp.start()

            @pl.when(i == n_blocks - 1)
            def _():
                for cp in stores():
                    cp.wait()

    vec = pl.BlockSpec((1, d), lambda i, kk: (0, 0))
    return pl.pallas_call(
        _behind(body, deps),
        name=name,
        grid=(n_blocks, nk),
        in_specs=[ANY_SPEC] * len(deps) + [
            pl.BlockSpec((tm, tk), lambda i, kk: (i, kk)),
            pl.BlockSpec((None, d, tk), lambda i, kk: (kk // per_shard, 0, kk % per_shard)),
            vec,
            ANY_SPEC,
            ANY_SPEC,
        ],
        out_specs=[ANY_SPEC, ANY_SPEC, vec],
        out_shape=[_shape((s, d), F32), _shape((s, d), BF16), _shape((1, d), F32)],
        scratch_shapes=[
            pltpu.VMEM((tm, d), F32),
            pltpu.VMEM((tm, d), F32),
            pltpu.VMEM((tm, d), F32),
            pltpu.VMEM((tm, d), F32),
            pltpu.VMEM((tm, d), BF16),
            pltpu.SemaphoreType.DMA((2,)),
            pltpu.SemaphoreType.DMA((2,)),
        ],
        compiler_params=_compute_params("arbitrary", "arbitrary"),
    )(*deps, g, w, gain, x, dres)


def _loss_and_grad(x, gain, target, *, name):
    s, d = x.shape
    tm = _tile(s, 256, SUBLANES_BF16)

    def body(x_ref, gain_ref, t_ref, loss_ref, dx_ref, dxb_ref, dg_ref):
        @pl.when(pl.program_id(0) == 0)
        def _():
            loss_ref[...] = jnp.zeros_like(loss_ref)
            dg_ref[...] = jnp.zeros_like(dg_ref)

        xf = x_ref[...]
        r = lax.rsqrt(jnp.mean(xf * xf, axis=-1, keepdims=True) + EPS)
        err = (xf * r) * gain_ref[...] - t_ref[...]
        loss_ref[...] += 0.5 * jnp.sum(jnp.mean(err * err, axis=-1, keepdims=True))
        dx, dgain = _rmsnorm_bwd(err * (1.0 / d), xf, gain_ref[...], None)
        dx_ref[...] = dx
        dxb_ref[...] = dx.astype(BF16)
        dg_ref[...] += dgain

    row_tile = pl.BlockSpec((tm, d), lambda i: (i, 0))
    vec = pl.BlockSpec((1, d), lambda i: (0, 0))
    return pl.pallas_call(
        body,
        name=name,
        grid=(s // tm,),
        in_specs=[row_tile, vec, row_tile],
        out_specs=[pl.BlockSpec((1, LANES), lambda i: (0, 0)), row_tile, row_tile, vec],
        out_shape=[_shape((1, LANES), F32), _shape((s, d), F32), _shape((s, d), BF16), _shape((1, d), F32)],
        compiler_params=_compute_params("arbitrary"),
    )(x, gain, target)


def _trailing_sums(v_ext, window):
    acc, span = v_ext, 1
    while span < window:
        acc = acc + pltpu.roll(acc, span, 0)
        span *= 2
    return acc


def _leading_sums(q_ext, window):
    n = q_ext.shape[0]
    acc, span = q_ext, 1
    while span < window:
        acc = acc + pltpu.roll(acc, n - span, 0)
        span *= 2
    return acc


def _inverse_counts(first_token, rows, window):
    t = first_token + lax.broadcasted_iota(jnp.int32, (rows, 1), 0)
    return 1.0 / jnp.minimum(t + 1, window).astype(F32)


def _mixer_fwd(proj, conv_w, pool_w, pool_scale, *, name):
    s, _ = proj.shape
    dc = conv_w.shape[1]
    n_groups, cg, _ = pool_w.shape
    dp = n_groups * cg
    assert dc == dp and all(w & (w - 1) == 0 and w <= HALO for w in POOL_WINDOWS)
    ts = _tile(s, 128, HALO)
    halo_blocks = ts // HALO

    def body(b_ref, c_ref, xt_ref, v_ref, ch_ref, xth_ref, vh_ref, cw_ref, pw_ref, ps_ref, y_ref):
        i = pl.program_id(0)
        has_past = i > 0
        for cc in range(dc // cg):
            cols = slice(cc * cg, (cc + 1) * cg)
            u_ext = jnp.concatenate(
                [
                    jnp.where(has_past, ch_ref[:, cols].astype(F32) * xth_ref[:, cols].astype(F32), 0.0),
                    c_ref[:, cols].astype(F32) * xt_ref[:, cols].astype(F32),
                ],
                axis=0,
            )
            conv = (
                cw_ref[2:3, cols] * u_ext[HALO:]
                + cw_ref[1:2, cols] * pltpu.roll(u_ext, 1, 0)[HALO:]
                + cw_ref[0:1, cols] * pltpu.roll(u_ext, 2, 0)[HALO:]
            )
            y_ref[:, cols] = (b_ref[:, cols].astype(F32) * conv).astype(BF16)
        for gi, window in enumerate(POOL_WINDOWS):
            cols = slice(gi * cg, (gi + 1) * cg)
            v_ext = jnp.concatenate(
                [jnp.where(has_past, vh_ref[:, cols].astype(F32), 0.0), v_ref[:, cols].astype(F32)], axis=0
            )
            mean = _trailing_sums(v_ext, window)[HALO:] * _inverse_counts(i * ts, ts, window)
            diff = (mean - v_ext[HALO:]).astype(BF16)
            z = jnp.dot(diff, pw_ref[gi], preferred_element_type=F32)
            y_ref[:, dc + gi * cg : dc + (gi + 1) * cg] = (z * ps_ref[:, cols]).astype(BF16)

    def col(jc):
        return pl.BlockSpec((ts, dc), lambda i: (i, jc))

    def past(jc):
        return pl.BlockSpec((HALO, dc), lambda i: (jnp.maximum(i * halo_blocks - 1, 0), jc))

    return pl.pallas_call(
        body,
        name=name,
        grid=(s // ts,),
        in_specs=[
            col(0), col(1), col(2), col(3), past(1), past(2), past(3),
            pl.BlockSpec((8, dc), lambda i: (0, 0)),
            pl.BlockSpec((n_groups, cg, cg), lambda i: (0, 0, 0)),
            pl.BlockSpec((1, dp), lambda i: (0, 0)),
        ],
        out_specs=pl.BlockSpec((ts, dc + dp), lambda i: (i, 0)),
        out_shape=_shape((s, dc + dp), BF16),
        compiler_params=_compute_params("parallel"),
    )(proj, proj, proj, proj, proj, proj, proj, conv_w, pool_w, pool_scale)


def _mixer_bwd(dy, proj, conv_w, pool_w, pool_scale, *, name, deps=()):
    s, e = proj.shape
    dc = conv_w.shape[1]
    n_groups, cg, _ = pool_w.shape
    dp = n_groups * cg
    ts = _tile(s, 128, HALO)
    halo_blocks = ts // HALO
    n_tiles = s // ts
    n_halo_blocks = s // HALO
    n_ext = ts + HALO

    def body(dyc_ref, dyp_ref, dycn_ref, dypn_ref, b_ref, c_ref, xt_ref, v_ref, bn_ref, ch_ref, xth_ref, vh_ref,
             cw_ref, pw_ref, ps_ref, dproj_ref, dcw_ref, dpw_ref, dps_ref):
        i = pl.program_id(0)
        has_past = i > 0
        has_next = i < n_tiles - 1

        @pl.when(i == 0)
        def _():
            dcw_ref[...] = jnp.zeros_like(dcw_ref)
            dpw_ref[...] = jnp.zeros_like(dpw_ref)
            dps_ref[...] = jnp.zeros_like(dps_ref)

        for cc in range(dc // cg):
            cols = slice(cc * cg, (cc + 1) * cg)
            c_now, xt_now = c_ref[:, cols].astype(F32), xt_ref[:, cols].astype(F32)
            u_ext = jnp.concatenate(
                [jnp.where(has_past, ch_ref[:, cols].astype(F32) * xth_ref[:, cols].astype(F32), 0.0), c_now * xt_now],
                axis=0,
            )
            u0 = u_ext[HALO:]
            u1 = pltpu.roll(u_ext, 1, 0)[HALO:]
            u2 = pltpu.roll(u_ext, 2, 0)[HALO:]
            dyc = dyc_ref[:, cols]
            conv = cw_ref[2:3, cols] * u0 + cw_ref[1:2, cols] * u1 + cw_ref[0:1, cols] * u2
            dproj_ref[:, cols] = (dyc * conv).astype(BF16)
            dconv = dyc * b_ref[:, cols].astype(F32)
            dconv_ext = jnp.concatenate(
                [dconv, jnp.where(has_next, dycn_ref[:, cols] * bn_ref[:, cols].astype(F32), 0.0)], axis=0
            )
            du = (
                cw_ref[2:3, cols] * dconv
                + cw_ref[1:2, cols] * pltpu.roll(dconv_ext, n_ext - 1, 0)[:ts]
                + cw_ref[0:1, cols] * pltpu.roll(dconv_ext, n_ext - 2, 0)[:ts]
            )
            dproj_ref[:, dc + cc * cg : dc + (cc + 1) * cg] = (du * xt_now).astype(BF16)
            dproj_ref[:, 2 * dc + cc * cg : 2 * dc + (cc + 1) * cg] = (du * c_now).astype(BF16)
            dcw_ref[0:1, cols] += jnp.sum(dconv * u2, axis=0, keepdims=True)
            dcw_ref[1:2, cols] += jnp.sum(dconv * u1, axis=0, keepdims=True)
            dcw_ref[2:3, cols] += jnp.sum(dconv * u0, axis=0, keepdims=True)

        for gi, window in enumerate(POOL_WINDOWS):
            cols = slice(gi * cg, (gi + 1) * cg)
            v_ext = jnp.concatenate(
                [jnp.where(has_past, vh_ref[:, cols].astype(F32), 0.0), v_ref[:, cols].astype(F32)], axis=0
            )
            mean = _trailing_sums(v_ext, window)[HALO:] * _inverse_counts(i * ts, ts, window)
            diff = (mean - v_ext[HALO:]).astype(BF16)
            z = jnp.dot(diff, pw_ref[gi], preferred_element_type=F32)
            dyp = dyp_ref[:, cols]
            dps_ref[:, cols] += jnp.sum(dyp * z, axis=0, keepdims=True)
            scale = ps_ref[:, cols]
            dz_ext = jnp.concatenate([dyp * scale, jnp.where(has_next, dypn_ref[:, cols] * scale, 0.0)], axis=0)
            dz_ext = dz_ext.astype(BF16)
            dpw_ref[gi] += lax.dot_general(
                diff, dz_ext[:ts], (((0,), (0,)), ((), ())), preferred_element_type=F32
            )
            ddiff_ext = lax.dot_general(
                dz_ext, pw_ref[gi], (((1,), (1,)), ((), ())), preferred_element_type=F32
            )
            q_ext = ddiff_ext * _inverse_counts(i * ts, n_ext, window)
            dv = _leading_sums(q_ext, window)[:ts] - ddiff_ext[:ts]
            dproj_ref[:, 3 * dc + gi * cg : 3 * dc + (gi + 1) * cg] = dv.astype(BF16)

    def col(jc):
        return pl.BlockSpec((ts, dc), lambda i: (i, jc))

    def past(jc):
        return pl.BlockSpec((HALO, dc), lambda i: (jnp.maximum(i * halo_blocks - 1, 0), jc))

    def following(jc):
        return pl.BlockSpec((HALO, dc), lambda i: (jnp.minimum((i + 1) * halo_blocks, n_halo_blocks - 1), jc))

    return pl.pallas_call(
        _behind(body, deps),
        name=name,
        grid=(n_tiles,),
        in_specs=[ANY_SPEC] * len(deps) + [
            col(0), col(1), following(0), following(1),
            col(0), col(1), col(2), col(3), following(0), past(1), past(2), past(3),
            pl.BlockSpec((8, dc), lambda i: (0, 0)),
            pl.BlockSpec((n_groups, cg, cg), lambda i: (0, 0, 0)),
            pl.BlockSpec((1, dp), lambda i: (0, 0)),
        ],
        out_specs=[
            pl.BlockSpec((ts, e), lambda i: (i, 0)),
            pl.BlockSpec((8, dc), lambda i: (0, 0)),
            pl.BlockSpec((n_groups, cg, cg), lambda i: (0, 0, 0)),
            pl.BlockSpec((1, dp), lambda i: (0, 0)),
        ],
        out_shape=[_shape((s, e), BF16), _shape((8, dc), F32), _shape((n_groups, cg, cg), F32), _shape((1, dp), F32)],
        compiler_params=_compute_params("arbitrary"),
    )(*deps, dy, dy, dy, dy, proj, proj, proj, proj, proj, proj, proj, proj, conv_w, pool_w, pool_scale)


def _cast_bf16(a, layer, *, name):
    _, rows, cols = a.shape
    tr = _tile(rows, 512, SUBLANES_BF16)

    def body(a_ref, o_ref):
        o_ref[...] = a_ref[...].astype(BF16)

    return pl.pallas_call(
        body, name=name, grid=(rows // tr,),
        in_specs=[pl.BlockSpec((None, tr, cols), lambda i: (layer, i, 0))],
        out_specs=pl.BlockSpec((None, tr, cols), lambda i: (0, i, 0)),
        out_shape=_shape((1, rows, cols), BF16),
        compiler_params=_compute_params("parallel"),
    )(a)


def _core_index():
    return lax.axis_index("c").astype(jnp.int32).reshape((1,))


def _add_sibling_half(grads, received, *, name, deps=()):
    nsh, r, c = grads.shape
    hr = r // 2
    tr = _tile(hr, 1024, SUBLANES_BF16)
    tiles = hr // tr

    def body(core_ref, *refs):
        g_ref, r_ref, o_ref = refs[len(deps):]
        o_ref[...] = (g_ref[...].astype(F32) + r_ref[...].astype(F32)).astype(BF16)

    half = pl.BlockSpec((None, tr, c), lambda sh, t, core: (sh, t, 0))
    return pl.pallas_call(
        body,
        name=name,
        grid_spec=pltpu.PrefetchScalarGridSpec(
            num_scalar_prefetch=1,
            grid=(nsh, tiles),
            in_specs=[ANY_SPEC] * len(deps)
            + [pl.BlockSpec((None, tr, c), lambda sh, t, core: (sh, core[0] * tiles + t, 0)), half],
            out_specs=half,
        ),
        out_shape=_shape((nsh, hr, c), BF16),
        compiler_params=_compute_params("parallel", "parallel"),
    )(_core_index(), *deps, grads, received)


def _chip_index_operand():
    return _chip_index(lax.axis_index("x"), lax.axis_index("y")).astype(jnp.int32).reshape((1,))


def _sum_chip_partials(own, received, *, name, deps=()):
    nsh, hr, c = received.shape
    tr = _tile(hr, 512, SUBLANES_BF16)

    def body(chip_ref, *refs):
        own_ref, p_ref, o_ref = refs[len(deps):]
        mine = chip_ref[0]
        total = None
        for sh in range(nsh):
            term = jnp.where(mine == sh, own_ref[...], p_ref[sh]).astype(F32)
            total = term if total is None else total + term
        o_ref[...] = total

    return pl.pallas_call(
        body,
        name=name,
        grid_spec=pltpu.PrefetchScalarGridSpec(
            num_scalar_prefetch=1,
            grid=(hr // tr,),
            in_specs=[ANY_SPEC] * len(deps) + [
                pl.BlockSpec((None, tr, c), lambda t, chip: (chip[0], t, 0)),
                pl.BlockSpec((nsh, tr, c), lambda t, chip: (0, t, 0)),
            ],
            out_specs=pl.BlockSpec((tr, c), lambda t, chip: (t, 0)),
        ),
        out_shape=_shape((hr, c), F32),
        compiler_params=_compute_params("parallel"),
    )(_chip_index_operand(), *deps, own, received)


def _adamw(grad, w, m, v, layer, carried, *, name, deps=()):
    n_layers, r, c = w.shape
    in_halves = isinstance(grad, (tuple, list))
    tr = _tile(r // 2 if in_halves else r, 256, 8)
    half_tiles = (r // 2) // tr if in_halves else 0
    bias1 = 1.0 - ADAM_B1 ** ADAM_STEP
    bias2 = 1.0 - ADAM_B2 ** ADAM_STEP
    n_grads = 2 if in_halves else 1

    def body(core_ref, *refs):
        refs = refs[len(deps):]
        w_ref, m_ref, v_ref = refs[n_grads : n_grads + 3]
        go_ref, d_ref, mo_ref, vo_ref, done_ref = refs[-5:]
        done_ref[...] = jnp.zeros_like(done_ref)
        if in_halves:
            in_my_half = pl.program_id(0) // half_tiles == core_ref[0]
            g = jnp.where(in_my_half, refs[0][...], refs[1][...])
        else:
            g = refs[0][...]
        m_new = ADAM_B1 * m_ref[...] + (1.0 - ADAM_B1) * g
        v_new = ADAM_B2 * v_ref[...] + (1.0 - ADAM_B2) * (g * g)
        go_ref[...] = g
        mo_ref[...] = m_new
        vo_ref[...] = v_new
        d_ref[...] = -ADAM_LR * ((m_new / bias1) / (jnp.sqrt(v_new / bias2) + ADAM_EPS) + ADAM_WD * w_ref[...])

    def half_spec(mine):
        def index(t, core):
            first = (core[0] if mine else 1 - core[0]) * half_tiles
            return (jnp.clip(t - first, 0, half_tiles - 1), 0)

        return pl.BlockSpec((tr, c), index)

    layer_tile = pl.BlockSpec((None, tr, c), lambda t, core: (layer, t, 0))
    if in_halves:
        grad_specs, grads = [half_spec(True), half_spec(False)], list(grad)
    else:
        grad_specs, grads = [pl.BlockSpec((tr, c), lambda t, core: (t, 0))], [grad]
    in_specs = [ANY_SPEC] * len(deps) + grad_specs + [layer_tile] * 3
    args = list(deps) + grads + [w, m, v]
    aliases = {}
    if carried is not None:
        aliases = {1 + len(args) + n: n for n in range(4)}
        in_specs += [HBM_SPEC] * 4
        args += list(carried)
    *outs, done = pl.pallas_call(
        body,
        name=name,
        grid_spec=pltpu.PrefetchScalarGridSpec(
            num_scalar_prefetch=1,
            grid=(r // tr,),
            in_specs=in_specs,
            out_specs=[layer_tile] * 4 + [pl.BlockSpec((8, LANES), lambda t, core: (0, 0))],
        ),
        out_shape=[_shape((n_layers, r, c), F32)] * 4 + [_shape((8, LANES), F32)],
        input_output_aliases=aliases,
        compiler_params=_compute_params("arbitrary"),
    )(_core_index(), *args)
    return outs, done


def _place():
    x, y, c = (lax.axis_index(a) for a in MESH_AXES)
    other_chips = [(1 - x, y), (x, 1 - y), (1 - x, 1 - y)]
    return x, y, c, other_chips


def _chip_index(x, y):
    return 2 * x + y


def _gather_small(conv_rows, pool_rows, *, name):
    blocks = (conv_rows, pool_rows)
    n = len(blocks)

    def body(*refs):
        ins, outs = refs[:n], refs[n : 2 * n]
        send_sems, recv_sems = refs[2 * n :]
        x, y, c, other_chips = _place()
        mine = _chip_index(x, y)
        peers = [(x, y, 1 - c)] + [(px, py, c) for px, py in other_chips]
        _handshake(peers)
        sends = []
        for a in range(n):
            for j, peer in enumerate(peers):
                cp = pltpu.make_async_remote_copy(
                    src_ref=ins[a], dst_ref=outs[a].at[mine], send_sem=send_sems.at[a, j], recv_sem=recv_sems.at[a, j],
                    device_id=peer, device_id_type=MESH_ID,
                )
                cp.start()
                sends.append(cp)
        for a in range(n):
            for j, (px, py, _) in enumerate(peers):
                pltpu.make_async_remote_copy(
                    src_ref=ins[a], dst_ref=outs[a].at[_chip_index(px, py)], send_sem=send_sems.at[a, j],
                    recv_sem=recv_sems.at[a, j], device_id=peers[j], device_id_type=MESH_ID,
                ).wait_recv()
        for cp in sends:
            cp.wait_send()

    return _sequencer_call(
        body,
        [_shape((N_CHIPS,) + blk.shape, F32) for blk in blocks],
        [pltpu.SemaphoreType.DMA((n, 4)), pltpu.SemaphoreType.DMA((n, 4))],
        GATHER_SMALL_COLLECTIVE_ID,
        name,
    )(*blocks)


def _allreduce_small(vec, *, name):
    rows, n = vec.shape
    n_dev = 8

    def body(v_ref, o_ref, slots, send_sems, recv_sems):
        x, y, c, _ = _place()
        me = 4 * x + 2 * y + c
        slots[me] = v_ref[...]
        sends = []
        for mask in range(1, n_dev):
            fx, fy, fc = (mask >> 2) & 1, (mask >> 1) & 1, mask & 1
            peer = (x ^ fx, y ^ fy, c ^ fc)
            cp = pltpu.make_async_remote_copy(
                src_ref=v_ref, dst_ref=slots.at[me], send_sem=send_sems.at[mask - 1], recv_sem=recv_sems.at[mask - 1],
                device_id=peer, device_id_type=MESH_ID,
            )
            cp.start()
            sends.append(cp)
        for mask in range(1, n_dev):
            fx, fy, fc = (mask >> 2) & 1, (mask >> 1) & 1, mask & 1
            peer = (x ^ fx, y ^ fy, c ^ fc)
            pltpu.make_async_remote_copy(
                src_ref=v_ref, dst_ref=slots.at[4 * peer[0] + 2 * peer[1] + peer[2]], send_sem=send_sems.at[mask - 1],
                recv_sem=recv_sems.at[mask - 1], device_id=peer, device_id_type=MESH_ID,
            ).wait_recv()
        total = slots[0]
        for dev in range(1, n_dev):
            total = total + slots[dev]
        o_ref[...] = total
        for cp in sends:
            cp.wait_send()

    return pl.pallas_call(
        body,
        name=name,
        in_specs=[VMEM_SPEC],
        out_specs=VMEM_SPEC,
        out_shape=_shape((rows, n), F32),
        scratch_shapes=[
            pltpu.VMEM((n_dev, rows, n), F32),
            pltpu.SemaphoreType.DMA((n_dev - 1,)),
            pltpu.SemaphoreType.DMA((n_dev - 1,)),
        ],
    )(vec)


def _handshake(peers):
    barrier = pltpu.get_barrier_semaphore()
    for peer in peers:
        pl.semaphore_signal(barrier, inc=1, device_id=peer, device_id_type=MESH_ID)
    pl.semaphore_wait(barrier, len(peers))


def _sequencer_call(body, out_type, scratch_types, collective_id, name):
    return pl.kernel(
        body,
        name=name,
        out_type=out_type,
        mesh=plsc.ScalarSubcoreMesh(axis_name="sequencer", num_cores=1),
        scratch_types=scratch_types,
        compiler_params=pltpu.CompilerParams(collective_id=collective_id),
    )


GATHER_COLLECTIVE_ID = 1
EXCHANGE_COLLECTIVE_ID = 2
SCATTER_COLLECTIVE_ID = 3
SHARE_COLLECTIVE_ID = 4
GATHER_SMALL_COLLECTIVE_ID = 5


def _gather_weights(stacked, layer, *, name):
    n = len(stacked)

    def body(*refs):
        ins, outs = refs[:n], refs[n : 2 * n]
        own_sems, send_sems, recv_sems, pass_send_sems, pass_recv_sems = refs[2 * n :]
        x, y, c, other_chips = _place()
        mine = _chip_index(x, y)
        sibling = (x, y, 1 - c)
        _handshake([sibling] + [(px, py, c) for px, py in other_chips])
        pending = []

        def send_my_half(a):
            hr = ins[a].shape[1] // 2
            my_half = pl.ds(c * hr, hr)
            cp = pltpu.make_async_remote_copy(
                src_ref=ins[a].at[layer], dst_ref=outs[a].at[mine], send_sem=own_sems.at[0, a],
                recv_sem=own_sems.at[1, a], device_id=sibling, device_id_type=MESH_ID,
            )
            cp.start()
            pending.append(cp.wait)
            for j, (px, py) in enumerate(other_chips):
                cp = pltpu.make_async_remote_copy(
                    src_ref=ins[a].at[layer, my_half], dst_ref=outs[a].at[mine, my_half], send_sem=send_sems.at[a, j],
                    recv_sem=recv_sems.at[a, j], device_id=(px, py, c), device_id_type=MESH_ID,
                )
                cp.start()
                pending.append(cp.wait_send)

        for a in range(min(2, n)):
            send_my_half(a)
        for a in range(n):
            hr = ins[a].shape[1] // 2
            my_half = pl.ds(c * hr, hr)
            for j, (px, py) in enumerate(other_chips):
                landed = outs[a].at[_chip_index(px, py), my_half]
                pltpu.make_async_remote_copy(
                    src_ref=landed, dst_ref=landed, send_sem=send_sems.at[a, j], recv_sem=recv_sems.at[a, j],
                    device_id=(px, py, c), device_id_type=MESH_ID,
                ).wait_recv()
                cp = pltpu.make_async_remote_copy(
                    src_ref=landed, dst_ref=landed, send_sem=pass_send_sems.at[a, j], recv_sem=pass_recv_sems.at[a, j],
                    device_id=sibling, device_id_type=MESH_ID,
                )
                cp.start()
                pending.append(cp.wait_send)
            if a + 2 < n:
                send_my_half(a + 2)
        for a in range(n):
            hr = ins[a].shape[1] // 2
            sibling_half = pl.ds((1 - c) * hr, hr)
            for j, (px, py) in enumerate(other_chips):
                passed = outs[a].at[_chip_index(px, py), sibling_half]
                pltpu.make_async_remote_copy(
                    src_ref=passed, dst_ref=passed, send_sem=pass_send_sems.at[a, j], recv_sem=pass_recv_sems.at[a, j],
                    device_id=sibling, device_id_type=MESH_ID,
                ).wait_recv()
        for wait in pending:
            wait()

    return _sequencer_call(
        body,
        [_shape((N_CHIPS,) + a.shape[1:], BF16) for a in stacked],
        [pltpu.SemaphoreType.DMA((2, n))] + [pltpu.SemaphoreType.DMA((n, 3))] * 4,
        GATHER_COLLECTIVE_ID,
        name,
    )(*stacked)


def _exchange_halves(grads, *, name):
    n = len(grads)

    def body(*refs):
        ins, outs = refs[:n], refs[n : 2 * n]
        send_sems, recv_sems = refs[2 * n :]
        x, y, c, _ = _place()
        sibling = (x, y, 1 - c)
        _handshake([sibling])
        copies = []
        for a in range(n):
            hr = ins[a].shape[1] // 2
            cp = pltpu.make_async_remote_copy(
                src_ref=ins[a].at[:, pl.ds((1 - c) * hr, hr), :], dst_ref=outs[a], send_sem=send_sems.at[a],
                recv_sem=recv_sems.at[a], device_id=sibling, device_id_type=MESH_ID,
            )
            cp.start()
            copies.append(cp)
        for cp in copies:
            cp.wait()

    return _sequencer_call(
        body,
        [_shape((g.shape[0], g.shape[1] // 2, g.shape[2]), BF16) for g in grads],
        [pltpu.SemaphoreType.DMA((n,)), pltpu.SemaphoreType.DMA((n,))],
        EXCHANGE_COLLECTIVE_ID,
        name,
    )(*grads)


def _scatter_partials(partials, *, name):
    n = len(partials)

    def body(*refs):
        ins, outs = refs[:n], refs[n : 2 * n]
        send_sems, recv_sems = refs[2 * n :]
        x, y, c, other_chips = _place()
        mine = _chip_index(x, y)
        _handshake([(px, py, c) for px, py in other_chips])
        pending = []
        for a in range(n):
            for j, (px, py) in enumerate(other_chips):
                cp = pltpu.make_async_remote_copy(
                    src_ref=ins[a].at[_chip_index(px, py)], dst_ref=outs[a].at[mine], send_sem=send_sems.at[a, j],
                    recv_sem=recv_sems.at[a, j], device_id=(px, py, c), device_id_type=MESH_ID,
                )
                cp.start()
                pending.append(cp.wait_send)
        for a in range(n):
            for j, (px, py) in enumerate(other_chips):
                landed = outs[a].at[_chip_index(px, py)]
                pltpu.make_async_remote_copy(
                    src_ref=landed, dst_ref=landed, send_sem=send_sems.at[a, j], recv_sem=recv_sems.at[a, j],
                    device_id=(px, py, c), device_id_type=MESH_ID,
                ).wait_recv()
        for wait in pending:
            wait()

    return _sequencer_call(
        body,
        [_shape(p.shape, BF16) for p in partials],
        [pltpu.SemaphoreType.DMA((n, 3)), pltpu.SemaphoreType.DMA((n, 3))],
        SCATTER_COLLECTIVE_ID,
        name,
    )(*partials)


def _share_halves(halves, *, name):
    n = len(halves)

    def body(*refs):
        ins, outs = refs[:n], refs[n : 2 * n]
        send_sems, recv_sems = refs[2 * n :]
        x, y, c, _ = _place()
        sibling = (x, y, 1 - c)
        _handshake([sibling])
        copies = []
        for a in range(n):
            cp = pltpu.make_async_remote_copy(
                src_ref=ins[a], dst_ref=outs[a], send_sem=send_sems.at[a], recv_sem=recv_sems.at[a],
                device_id=sibling, device_id_type=MESH_ID,
            )
            cp.start()
            copies.append(cp)
        for cp in copies:
            cp.wait()

    return _sequencer_call(
        body,
        [_shape(h.shape, F32) for h in halves],
        [pltpu.SemaphoreType.DMA((n,)), pltpu.SemaphoreType.DMA((n,))],
        SHARE_COLLECTIVE_ID,
        name,
    )(*halves)


class _ReduceToOwner:
    def __init__(self, grads, tag):
        self.grads, self.tag = grads, tag
        self.received = _exchange_halves(grads, name=f"exchange_halves_{tag}")

    def add_sibling(self, after):
        partials = [
            _add_sibling_half(g, r, name=f"add_sibling_{self.tag}_{a}", deps=after)
            for a, (g, r) in enumerate(zip(self.grads, self.received))
        ]
        self.partials = partials
        self.slots = _scatter_partials(partials, name=f"scatter_partials_{self.tag}")
        return partials

    def sum_chips(self, after):
        halves = [
            _sum_chip_partials(p, s, name=f"sum_partials_{self.tag}_{a}", deps=after)
            for a, (p, s) in enumerate(zip(self.partials, self.slots))
        ]
        self.halves = halves
        self.theirs = _share_halves(halves, name=f"share_halves_{self.tag}")
        return halves

    def totals(self):
        return list(zip(self.halves, self.theirs))


def kernel(x, w_in, conv_w, pool_w, pool_scale, w_out, norm_mix, norm_mlp, w_up, w_down, norm_final, loss_target, m_w_in, m_conv_w, m_pool_w, m_pool_scale, m_w_out, m_norm_mix, m_norm_mlp, m_w_up, m_w_down, m_norm_final, v_w_in, v_conv_w, v_pool_w, v_pool_scale, v_w_out, v_norm_mix, v_norm_mlp, v_w_up, v_w_down, v_norm_final):
    n_layers, d, _ = w_in.shape
    s = x.shape[1]
    dc = conv_w.shape[2] * N_CHIPS
    n_groups, cg_rows, cg = pool_w.shape[1:]
    dp = n_groups * cg
    x0 = x.reshape(s, d)
    target = loss_target.reshape(s, d)

    big = [w_in, w_out, w_up, w_down]

    conv_rows = jnp.pad(conv_w, ((0, 0), (0, 8 - CONV_TAPS), (0, 0))).reshape(n_layers * 8, -1)
    pool_rows = pool_w.reshape(n_layers * n_groups * cg_rows, cg)
    conv_all, pool_all = _gather_small(conv_rows, pool_rows, name="gather_small")
    conv_full = conv_all.reshape(N_CHIPS, n_layers, 8, -1).transpose(1, 2, 0, 3).reshape(n_layers, 8, dc)
    pool_full = (
        pool_all.reshape(N_CHIPS, n_layers, n_groups, cg_rows, cg).transpose(1, 2, 0, 3, 4)
        .reshape(n_layers, n_groups, cg, cg).astype(BF16)
    )

    saved = []
    xl = x0
    for l in range(n_layers):
        win_g, wout_g, wup_g, wdown_g = [
            _gather_weights([_cast_bf16(w, l, name=f"cast_l{l}_{t}")], 0, name=f"gather_weights_l{l}_{t}")[0]
            for t, w in enumerate(big)
        ]
        gain_mix = norm_mix[l].reshape(1, d)
        gain_mlp = norm_mlp[l].reshape(1, d)
        scale = pool_scale[l].reshape(1, dp)
        h1 = _rmsnorm(xl, gain_mix, name=f"norm_mix_l{l}")
        proj = _matmul_cols(h1, win_g, relu2=False, name=f"in_proj_l{l}")
        y = _mixer_fwd(proj, conv_full[l], pool_full[l], scale, name=f"mixer_fwd_l{l}")
        x_mid = _matmul_residual(y, wout_g.reshape(-1, d), xl, name=f"out_proj_l{l}")
        h2 = _rmsnorm(x_mid, gain_mlp, name=f"norm_mlp_l{l}")
        u, u2 = _matmul_cols(h2, wup_g, relu2=True, name=f"up_proj_l{l}")
        x_next = _matmul_residual(u2, wdown_g.reshape(-1, d), x_mid, name=f"down_proj_l{l}")
        saved.append((xl, h1, proj, y, x_mid, h2, u, u2, win_g, wout_g, wup_g, wdown_g, gain_mix, gain_mlp, scale))
        xl = x_next

    loss_part, dx, dx_bf16, d_norm_final = _loss_and_grad(xl, norm_final.reshape(1, d), target, name="loss_head")
    loss = lax.psum(loss_part[0, 0], MESH_AXES)

    small_grads = [None] * n_layers
    carried = [None] * 5
    pool_params = tuple(p.reshape(n_layers, n_groups * cg_rows, cg) for p in (pool_w, m_pool_w, v_pool_w))
    params = [(w_in, m_w_in, v_w_in), (w_out, m_w_out, v_w_out), (w_up, m_w_up, v_w_up), (w_down, m_w_down, v_w_down),
              pool_params]
    DOWN, UP_OUT, IN_POOL = (3,), (2, 1), (0, 4)

    def update(reduce, which, layer, after):
        dones = []
        for a, total in zip(which, reduce.totals()):
            w, m, v = params[a]
            carried[a], done = _adamw(total, w, m, v, layer, carried[a], name=f"adamw_{a}_l{layer}", deps=after)
            dones.append(done)
        return dones

    up_out_above = in_pool_above = None
    for l in reversed(range(n_layers)):
        xl, h1, proj, y, x_mid, h2, u, u2, win_g, wout_g, wup_g, wdown_g, gain_mix, gain_mlp, scale = saved[l]
        above = up_out_above is not None
        g_down = _matmul_tn(u2, dx_bf16, n_shards=N_CHIPS, shard_cols=False, name=f"grad_w_down_l{l}")
        down = _ReduceToOwner([g_down], f"down_l{l}")
        deps = up_out_above.sum_chips([g_down]) if above else [g_down]
        da = _matmul_nt(dx_bf16, wdown_g.reshape(-1, d), u=u, name=f"grad_act_l{l}", deps=deps)
        g_up = _matmul_tn(
            h2, da, n_shards=N_CHIPS, shard_cols=True, name=f"grad_w_up_l{l}", deps=down.add_sibling([da])
        )
        deps = in_pool_above.sum_chips([g_up]) if above else [g_up]
        dx_mid, dx_mid_bf16, d_gain_mlp = _matmul_nt_norm_bwd(
            da, wup_g, x_mid, gain_mlp, dx, name=f"grad_mid_l{l}", deps=deps
        )
        g_out = _matmul_tn(
            y, dx_mid_bf16, n_shards=N_CHIPS, shard_cols=False, name=f"grad_w_out_l{l}",
            deps=down.sum_chips([dx_mid_bf16]),
        )
        up_out = _ReduceToOwner([g_up, g_out], f"up_out_l{l}")
        deps = update(up_out_above, UP_OUT, l + 1, [g_out]) if above else [g_out]
        dy = _matmul_nt(dx_mid_bf16, wout_g.reshape(-1, d), name=f"grad_mixed_l{l}", deps=deps)
        dproj, d_conv, d_pool, d_scale = _mixer_bwd(
            dy, proj, conv_full[l], pool_full[l], scale, name=f"mixer_bwd_l{l}", deps=up_out.add_sibling([dy])
        )
        g_in = _matmul_tn(h1, dproj, n_shards=N_CHIPS, shard_cols=True, name=f"grad_w_in_l{l}")
        g_pool = (
            d_pool.reshape(n_groups, N_CHIPS, cg_rows, cg).transpose(1, 0, 2, 3)
            .reshape(N_CHIPS, n_groups * cg_rows, cg).astype(BF16)
        )
        in_pool = _ReduceToOwner([g_in, g_pool], f"in_pool_l{l}")
        deps = update(down, DOWN, l, [g_in])
        if above:
            deps = update(in_pool_above, IN_POOL, l + 1, deps)
        dx, dx_bf16, d_gain_mix = _matmul_nt_norm_bwd(
            dproj, win_g, xl, gain_mix, dx_mid, name=f"grad_x_l{l}", deps=in_pool.add_sibling(deps)
        )
        small_grads[l] = jnp.concatenate(
            [d_conv[:CONV_TAPS].reshape(-1), d_scale.reshape(-1), d_gain_mix.reshape(-1), d_gain_mlp.reshape(-1)]
        )
        up_out_above, in_pool_above = up_out, in_pool
    deps = up_out_above.sum_chips([dx_bf16])
    deps = in_pool_above.sum_chips(deps)
    deps = update(up_out_above, UP_OUT, 0, deps)
    update(in_pool_above, IN_POOL, 0, deps)

    vec = jnp.concatenate(small_grads + [d_norm_final.reshape(-1)])
    vec = _allreduce_small(vec.reshape(8, -1), name="allreduce_small").reshape(-1)
    per_layer = vec[: n_layers * (CONV_TAPS * dc + dp + 2 * d)].reshape(n_layers, -1)
    chip = _chip_index(lax.axis_index("x"), lax.axis_index("y"))
    dcs = dc // N_CHIPS
    g_conv = lax.dynamic_slice_in_dim(per_layer[:, : CONV_TAPS * dc].reshape(n_layers, CONV_TAPS, dc), chip * dcs, dcs, axis=2)
    g_scale = per_layer[:, CONV_TAPS * dc : CONV_TAPS * dc + dp]
    g_mix = per_layer[:, CONV_TAPS * dc + dp : CONV_TAPS * dc + dp + d]
    g_mlp = per_layer[:, CONV_TAPS * dc + dp + d :]
    g_final = vec[n_layers * (CONV_TAPS * dc + dp + 2 * d) :]

    def small_adamw(g, w, m, v, tag):
        flat = lambda t: t.reshape(1, -1, t.shape[-1])
        out, _ = _adamw(flat(g)[0], flat(w), flat(m), flat(v), 0, None, name=f"adamw_{tag}")
        return [o.reshape(w.shape) for o in out]

    o_conv = small_adamw(g_conv, conv_w, m_conv_w, v_conv_w, "conv_w")
    o_scale = small_adamw(g_scale, pool_scale, m_pool_scale, v_pool_scale, "pool_scale")
    o_mix = small_adamw(g_mix, norm_mix, m_norm_mix, v_norm_mix, "norm_mix")
    o_mlp = small_adamw(g_mlp, norm_mlp, m_norm_mlp, v_norm_mlp, "norm_mlp")
    o_final = small_adamw(g_final, norm_final, m_norm_final, v_norm_final, "norm_final")
    o_in, o_out, o_up, o_down, o_pool = carried
    o_pool = [o.reshape(pool_w.shape) for o in o_pool]

    ordered = [o_in, o_conv, o_pool, o_scale, o_out, o_mix, o_mlp, o_up, o_down, o_final]
    return (loss, dx.reshape(x.shape), *[o[0] for o in ordered], *[o[1] for o in ordered], *[o[2] for o in ordered],
            *[o[3] for o in ordered])
```
